```python
import math
import jax, jax.numpy as jnp
from jax import lax
import numpy as np


D_MODEL = 1024
BATCH = 16
SEQ = 2048
DEPTH = 4

POOL_GROUPS = 4
POOL_GROUP_CH = D_MODEL // 8
POOL_WIDTH = POOL_GROUPS * POOL_GROUP_CH
POOL_WINDOWS = (2, 4, 8, 16)
N_Q_HEADS = 8
N_KV_HEADS = 2
GQA_GROUP = N_Q_HEADS // N_KV_HEADS
HEAD_DIM = 64
Q_WIDTH = N_Q_HEADS * HEAD_DIM
KV_WIDTH = N_KV_HEADS * HEAD_DIM
WINDOW = 128
ATT_BLOCK = 128
D_IN = POOL_WIDTH + Q_WIDTH + 2 * KV_WIDTH + 2 * D_MODEL
V_START = POOL_WIDTH + Q_WIDTH + KV_WIDTH
V_END = V_START + KV_WIDTH
SPLIT_POINTS = (POOL_WIDTH, POOL_WIDTH + Q_WIDTH, POOL_WIDTH + Q_WIDTH + KV_WIDTH, V_END, V_END + D_MODEL)
N_EXPERTS = 64
EXPERT_FF = D_MODEL // 4
TOP_K = 8
N_EXPERT_GROUPS = 8
EXPERTS_PER_GROUP = N_EXPERTS // N_EXPERT_GROUPS
TOPK_GROUPS = 4
ROUTED_SCALE = 2.5
MOE_BLOCK = 256
ALPHA = (2.0 * DEPTH) ** 0.25
BETA = (8.0 * DEPTH) ** -0.25
LN_EPS = 1e-5

kernel_name = 'hybrid_pool_swa_moe_deepnorm'


def _layer_norm(x, g, b):
    xf = x.astype(jnp.float32)
    mu = xf.mean(-1, keepdims=True)
    var = jnp.square(xf - mu).mean(-1, keepdims=True)
    y = (xf - mu) * lax.rsqrt(var + LN_EPS)
    return (y * g.astype(jnp.float32) + b.astype(jnp.float32)).astype(x.dtype)


def _alibi_slopes():
    return jnp.asarray(2.0 ** (-8.0 * np.arange(1, N_Q_HEADS + 1) / N_Q_HEADS), dtype=jnp.float32)


def _pool_mixer(u, w_pool, pool_scale):
    B, S, _ = u.shape
    c = jnp.cumsum(u.astype(jnp.float32), axis=1)
    c = jnp.concatenate([jnp.zeros((B, 1, POOL_WIDTH), jnp.float32), c], axis=1)
    t1 = jnp.arange(1, S + 1, dtype=jnp.float32)[:, None]
    means = []
    for gi, w in enumerate(POOL_WINDOWS):
        cg = c[:, :, gi * POOL_GROUP_CH:(gi + 1) * POOL_GROUP_CH]
        lo = jnp.concatenate([jnp.zeros((B, w - 1, POOL_GROUP_CH), jnp.float32), cg[:, :S + 1 - w]], axis=1)
        means.append((cg[:, 1:] - lo) / jnp.minimum(t1, float(w)))
    pooled = jnp.stack(means, axis=2)
    ug = u.reshape(B, S, POOL_GROUPS, POOL_GROUP_CH).astype(jnp.float32)
    mixed = jnp.einsum('bsgc,gcd->bsgd', (pooled - ug).astype(u.dtype), w_pool)
    return mixed.reshape(B, S, POOL_WIDTH) * pool_scale


def _sliding_window_attention(q, k, v, sinks):
    B, S = q.shape[0], q.shape[1]
    nb = S // ATT_BLOCK
    qb = q.reshape(B, nb, ATT_BLOCK, N_KV_HEADS, GQA_GROUP, HEAD_DIM)

    def banded(t):
        cur = t.reshape(B, nb, ATT_BLOCK, N_KV_HEADS, HEAD_DIM)
        prev = jnp.concatenate([jnp.zeros_like(cur[:, :1]), cur[:, :-1]], axis=1)
        return jnp.concatenate([prev, cur], axis=2)

    kb, vb = banded(k), banded(v)
    s = jnp.einsum('bnqhgd,bnkhd->bnhgqk', qb, kb).astype(jnp.float32) * (HEAD_DIM ** -0.5)
    qi = jnp.arange(ATT_BLOCK)[:, None]
    kj = jnp.arange(2 * ATT_BLOCK)[None, :]
    dist = ATT_BLOCK + qi - kj
    key_pos = jnp.arange(nb)[:, None, None] * ATT_BLOCK - ATT_BLOCK + kj[None]
    valid = (dist >= 0) & (dist < WINDOW) & (key_pos >= 0)
    slopes = _alibi_slopes().reshape(N_KV_HEADS, GQA_GROUP, 1, 1)
    s = s - slopes * dist.astype(jnp.float32)
    s = jnp.where(valid[None, :, None, None], s, -jnp.inf)
    sink = sinks.astype(jnp.float32).reshape(1, 1, N_KV_HEADS, GQA_GROUP, 1, 1)
    m = jnp.maximum(s.max(-1, keepdims=True), sink)
    p = jnp.exp(s - m)
    denom = p.sum(-1, keepdims=True) + jnp.exp(sink - m)
    o = jnp.einsum('bnhgqk,bnkhd->bnqhgd', (p / denom).astype(v.dtype), vb)
    return o.reshape(B, S, Q_WIDTH)


def _mixer(x, w_in, b_in, w_pool, pool_scale, sinks, w_br_pool, w_br_attn, w_out):
    B, S, _ = x.shape
    h = jnp.einsum('bsd,de->bse', x, w_in) + b_in
    u, q, k, v, g_pool, g_attn = jnp.split(h, list(SPLIT_POINTS), axis=-1)
    y_pool = jnp.einsum('bsp,pd->bsd', _pool_mixer(u, w_pool, pool_scale), w_br_pool)
    o = _sliding_window_attention(q.reshape(B, S, N_Q_HEADS, HEAD_DIM),
                                  k.reshape(B, S, N_KV_HEADS, HEAD_DIM),
                                  v.reshape(B, S, N_KV_HEADS, HEAD_DIM), sinks)
    y_attn = jnp.einsum('bse,ed->bsd', o, w_br_attn)
    merged = jax.nn.sigmoid(g_pool) * y_pool + jax.nn.sigmoid(g_attn) * y_attn
    return jnp.einsum('bsd,de->bse', merged, w_out)


def _swiglu(x, wg, wu, wd):
    return jnp.dot(jax.nn.silu(jnp.dot(x, wg)) * jnp.dot(x, wu), wd)


def _moe(x, w_router, router_bias, w_eg, w_eu, w_ed, w_sg, w_su, w_sd):
    B, S, D = x.shape
    T = B * S
    xt = x.reshape(T, D)
    scores = jax.nn.sigmoid(jnp.dot(xt, w_router).astype(jnp.float32))
    biased = scores + router_bias.astype(jnp.float32)
    grp_score = lax.top_k(biased.reshape(T, N_EXPERT_GROUPS, EXPERTS_PER_GROUP), 2)[0].sum(-1)
    _, top_groups = lax.top_k(grp_score, TOPK_GROUPS)
    grp_keep = jax.nn.one_hot(top_groups, N_EXPERT_GROUPS, dtype=jnp.float32).sum(1) > 0
    expert_keep = jnp.repeat(grp_keep, EXPERTS_PER_GROUP, axis=1)
    _, top_e = lax.top_k(jnp.where(expert_keep, biased, -jnp.inf), TOP_K)
    sel = jnp.take_along_axis(scores, top_e, axis=1)
    gates = sel / sel.sum(-1, keepdims=True) * ROUTED_SCALE
    A = T * TOP_K
    e_flat = top_e.reshape(A).astype(jnp.int32)
    tok_flat = jnp.repeat(jnp.arange(T, dtype=jnp.int32), TOP_K)
    g_flat = gates.reshape(A)
    order = jnp.argsort(e_flat, stable=True)
    e_s, tok_s, g_s = e_flat[order], tok_flat[order], g_flat[order]
    counts = jnp.bincount(e_flat, length=N_EXPERTS)
    starts = jnp.cumsum(counts) - counts
    pcounts = (counts + MOE_BLOCK - 1) // MOE_BLOCK * MOE_BLOCK
    pends = jnp.cumsum(pcounts)
    pstarts = pends - pcounts
    dest = pstarts[e_s] + (jnp.arange(A, dtype=jnp.int32) - starts[e_s])
    n_pad = -(-A // MOE_BLOCK) * MOE_BLOCK + N_EXPERTS * MOE_BLOCK
    n_blocks = n_pad // MOE_BLOCK
    buf_tok = jnp.full((n_pad,), T, jnp.int32).at[dest].set(tok_s)
    buf_gate = jnp.zeros((n_pad,), jnp.float32).at[dest].set(g_s)
    blk_expert = jnp.minimum(jnp.searchsorted(pends, jnp.arange(n_blocks, dtype=jnp.int32) * MOE_BLOCK, side='right'),
                             N_EXPERTS - 1).astype(jnp.int32)
    x_pad = jnp.concatenate([xt, jnp.zeros((1, D), xt.dtype)], axis=0)

    def expert_block(acc, blk):
        tok_b, gate_b, e = blk
        xb = x_pad[tok_b]
        yb = _swiglu(xb, w_eg[e], w_eu[e], w_ed[e]) * gate_b[:, None].astype(xb.dtype)
        return acc.at[tok_b].add(yb), None

    acc, _ = lax.scan(expert_block, jnp.zeros((T + 1, D), x.dtype),
                      (buf_tok.reshape(n_blocks, MOE_BLOCK), buf_gate.reshape(n_blocks, MOE_BLOCK), blk_expert))
    y = acc[:T] + _swiglu(xt, w_sg, w_su, w_sd)
    return y.reshape(B, S, D)


def setup_inputs(seed: int = 0) -> dict:
    key = jax.random.key(seed)
    ks = jax.random.split(key, 24)

    def nrm(k, shape, scale):
        return jax.random.normal(k, shape, jnp.float32) * scale

    col_scale = np.ones((D_IN,), np.float32)
    col_scale[V_START:V_END] = BETA
    return {
        'x': nrm(ks[0], (BATCH, SEQ, D_MODEL), 1.0),
        'ln0_g': 1.0 + nrm(ks[1], (D_MODEL,), 0.02),
        'ln0_b': nrm(ks[2], (D_MODEL,), 0.02),
        'w_in': nrm(ks[3], (DEPTH, D_MODEL, D_IN), D_MODEL ** -0.5) * jnp.asarray(col_scale),
        'b_in': nrm(ks[4], (DEPTH, D_IN), 0.02),
        'w_pool': nrm(ks[5], (DEPTH, POOL_GROUPS, POOL_GROUP_CH, POOL_GROUP_CH), POOL_GROUP_CH ** -0.5),
        'pool_scale': 1.0 + nrm(ks[6], (DEPTH, POOL_WIDTH), 0.02),
        'attn_sinks': nrm(ks[7], (DEPTH, N_Q_HEADS), 0.5),
        'w_br_pool': nrm(ks[8], (DEPTH, POOL_WIDTH, D_MODEL), POOL_WIDTH ** -0.5),
        'w_br_attn': nrm(ks[9], (DEPTH, Q_WIDTH, D_MODEL), Q_WIDTH ** -0.5),
        'w_out': nrm(ks[10], (DEPTH, D_MODEL, D_MODEL), D_MODEL ** -0.5 * BETA),
        'ln1_g': 1.0 + nrm(ks[11], (DEPTH, D_MODEL), 0.02),
        'ln1_b': nrm(ks[12], (DEPTH, D_MODEL), 0.02),
        'w_router': nrm(ks[13], (DEPTH, D_MODEL, N_EXPERTS), D_MODEL ** -0.5),
        'router_bias': nrm(ks[14], (DEPTH, N_EXPERTS), 0.01),
        'w_exp_gate': nrm(ks[15], (DEPTH, N_EXPERTS, D_MODEL, EXPERT_FF), D_MODEL ** -0.5),
        'w_exp_up': nrm(ks[16], (DEPTH, N_EXPERTS, D_MODEL, EXPERT_FF), D_MODEL ** -0.5),
        'w_exp_down': nrm(ks[17], (DEPTH, N_EXPERTS, EXPERT_FF, D_MODEL), EXPERT_FF ** -0.5 * BETA),
        'w_sh_gate': nrm(ks[18], (DEPTH, D_MODEL, EXPERT_FF), D_MODEL ** -0.5),
        'w_sh_up': nrm(ks[19], (DEPTH, D_MODEL, EXPERT_FF), D_MODEL ** -0.5),
        'w_sh_down': nrm(ks[20], (DEPTH, EXPERT_FF, D_MODEL), EXPERT_FF ** -0.5 * BETA),
        'ln2_g': 1.0 + nrm(ks[21], (DEPTH, D_MODEL), 0.02),
        'ln2_b': nrm(ks[22], (DEPTH, D_MODEL), 0.02),
    }


def reference(x, ln0_g, ln0_b, w_in, b_in, w_pool, pool_scale, attn_sinks, w_br_pool, w_br_attn, w_out,
              ln1_g, ln1_b, w_router, router_bias, w_exp_gate, w_exp_up, w_exp_down,
              w_sh_gate, w_sh_up, w_sh_down, ln2_g, ln2_b):
    x = _layer_norm(x, ln0_g, ln0_b)
    for l in range(DEPTH):
        mix = _mixer(x, w_in[l], b_in[l], w_pool[l], pool_scale[l], attn_sinks[l],
                     w_br_pool[l], w_br_attn[l], w_out[l])
        x = _layer_norm(ALPHA * x + mix, ln1_g[l], ln1_b[l])
        ffn = _moe(x, w_router[l], router_bias[l], w_exp_gate[l], w_exp_up[l], w_exp_down[l],
                   w_sh_gate[l], w_sh_up[l], w_sh_down[l])
        x = _layer_norm(ALPHA * x + ffn, ln2_g[l], ln2_b[l])
    return x
```

```python
import functools

import numpy as np
import jax
import jax.numpy as jnp
from jax import lax
from jax.experimental import pallas as pl
from jax.experimental.pallas import tpu as pltpu

D_MODEL = 1024
DEPTH = 4
POOL_GROUPS = 4
POOL_GROUP_CH = 128
POOL_WIDTH = POOL_GROUPS * POOL_GROUP_CH
POOL_WINDOWS = (2, 4, 8, 16)
POOL_HALO = 16
N_Q_HEADS = 8
N_KV_HEADS = 2
HEAD_DIM = 64
Q_WIDTH = N_Q_HEADS * HEAD_DIM
KV_WIDTH = N_KV_HEADS * HEAD_DIM
WINDOW = 128
ATT_BLOCK = 128
D_IN = POOL_WIDTH + Q_WIDTH + 2 * KV_WIDTH + 2 * D_MODEL
QKV_START = POOL_WIDTH
GATE_START = POOL_WIDTH + Q_WIDTH + 2 * KV_WIDTH
N_EXPERTS = 64
EXPERT_FF = 256
TOP_K = 8
N_EXPERT_GROUPS = 8
EXPERTS_PER_GROUP = N_EXPERTS // N_EXPERT_GROUPS
TOPK_GROUPS = 4
ROUTED_SCALE = 2.5
ALPHA = (2.0 * DEPTH) ** 0.25
LN_EPS = 1e-5
ALIBI_SLOPES = tuple(float(2.0 ** (-8.0 * h / N_Q_HEADS)) for h in range(1, N_Q_HEADS + 1))

LANES = 128
MIXER_TILE = 256
ROUTER_TILE = 512
EXPERT_BLOCK = 256
COMBINE_TILE = 128
DMA_UNROLL = 8
IDX_SUBLANES = 8
VMEM_LIMIT = 48 * 1024 * 1024

BF16 = jnp.bfloat16
F32 = jnp.float32


def _dot(a, b):
    return jnp.dot(a, b, preferred_element_type=F32)


def _dot_nt(a, b):
    return lax.dot_general(a, b, (((1,), (1,)), ((), ())), preferred_element_type=F32)


def _layer_norm(x, g, b):
    mu = jnp.mean(x, axis=-1, keepdims=True)
    xc = x - mu
    var = jnp.mean(xc * xc, axis=-1, keepdims=True)
    return xc * lax.rsqrt(var + LN_EPS) * g + b


def _sigmoid(x):
    return 1.0 / (1.0 + jnp.exp(-x))


def _mixer_kernel(sinks_ref, x_ref, ln0g_ref, ln0b_ref, w_in_ref, b_in_ref, w_pool_ref, pscale_ref,
                  w_brp_ref, w_bra_ref, w_out_ref, ln1g_ref, ln1b_ref, o_ref, ubuf, kvbuf,
                  *, layer, pre_ln, tq):
    s = pl.program_id(1)

    @pl.when(s == 0)
    def _():
        ubuf[0:POOL_HALO, :] = jnp.zeros((POOL_HALO, POOL_WIDTH), F32)
        kvbuf[0:ATT_BLOCK, :] = jnp.zeros((ATT_BLOCK, 8 * LANES), BF16)

    x = x_ref[...]
    if pre_ln:
        x = _layer_norm(x, ln0g_ref[...], ln0b_ref[...])
    xb = x.astype(BF16)

    u = _dot(xb, w_in_ref[:, 0:POOL_WIDTH]) + b_in_ref[:, 0:POOL_WIDTH]
    ubuf[POOL_HALO:POOL_HALO + tq, :] = u
    pos = (s * tq + lax.broadcasted_iota(jnp.int32, (tq, 1), 0)).astype(F32)
    mixed_parts = []
    for g, w in enumerate(POOL_WINDOWS):
        sl = slice(g * POOL_GROUP_CH, (g + 1) * POOL_GROUP_CH)
        cur = ubuf[POOL_HALO:POOL_HALO + tq, sl]
        acc = cur
        for j in range(1, w):
            acc = acc + ubuf[POOL_HALO - j:POOL_HALO - j + tq, sl]
        inv_cnt = 1.0 / jnp.minimum(pos + 1.0, float(w))
        d = (acc * inv_cnt - cur).astype(BF16)
        mixed_parts.append(_dot(d, w_pool_ref[g]) * pscale_ref[:, sl])
    mixed = jnp.concatenate(mixed_parts, axis=1).astype(BF16)
    y_pool = _dot(mixed, w_brp_ref[...])
    ubuf[0:POOL_HALO, :] = ubuf[tq:tq + POOL_HALO, :]

    qkv = _dot(xb, w_in_ref[:, QKV_START:GATE_START]) + b_in_ref[:, QKV_START:GATE_START]
    q = qkv[:, 0:Q_WIDTH].astype(BF16)
    k = qkv[:, Q_WIDTH:Q_WIDTH + KV_WIDTH]
    v = qkv[:, Q_WIDTH + KV_WIDTH:Q_WIDTH + 2 * KV_WIDTH]
    lo = lax.broadcasted_iota(jnp.int32, (tq, LANES), 1) < HEAD_DIM
    k_sw = pltpu.roll(k, HEAD_DIM, axis=1)
    v_sw = pltpu.roll(v, HEAD_DIM, axis=1)
    zero = jnp.zeros((tq, LANES), F32)
    slabs = (
        jnp.where(lo, k, zero), jnp.where(lo, zero, k_sw),
        jnp.where(lo, k_sw, zero), jnp.where(lo, zero, k),
        jnp.where(lo, v, zero), jnp.where(lo, zero, v_sw),
        jnp.where(lo, v_sw, zero), jnp.where(lo, zero, v),
    )
    for i, slab in enumerate(slabs):
        kvbuf[ATT_BLOCK:ATT_BLOCK + tq, i * LANES:(i + 1) * LANES] = slab.astype(BF16)

    qi = lax.broadcasted_iota(jnp.int32, (ATT_BLOCK, 2 * ATT_BLOCK), 0)
    kj = lax.broadcasted_iota(jnp.int32, (ATT_BLOCK, 2 * ATT_BLOCK), 1)
    dist = ATT_BLOCK + qi - kj
    band_ok = (dist >= 0) & (dist < WINDOW)
    distf = dist.astype(F32)
    o_blocks = []
    for qb in range(tq // ATT_BLOCK):
        r0 = qb * ATT_BLOCK
        first_key_pos = s * tq + r0 - ATT_BLOCK
        valid = band_ok & (kj + first_key_pos >= 0)
        o_pairs = []
        for hk in range(N_KV_HEADS):
            k_slabs = [kvbuf[r0:r0 + 2 * ATT_BLOCK, (2 * hk + i) * LANES:(2 * hk + i + 1) * LANES] for i in range(2)]
            v_slabs = [kvbuf[r0:r0 + 2 * ATT_BLOCK, (4 + 2 * hk + i) * LANES:(4 + 2 * hk + i + 1) * LANES] for i in range(2)]
            for pj in range(2):
                pair = hk * 2 + pj
                qp = q[r0:r0 + ATT_BLOCK, pair * LANES:(pair + 1) * LANES]
                o_pair = None
                for half in range(2):
                    h = pair * 2 + half
                    sc = _dot_nt(qp, k_slabs[half]) * (HEAD_DIM ** -0.5) - ALIBI_SLOPES[h] * distf
                    sc = jnp.where(valid, sc, -jnp.inf)
                    sink = sinks_ref[layer, h]
                    m = jnp.maximum(jnp.max(sc, axis=1, keepdims=True), sink)
                    p = jnp.exp(sc - m)
                    den = jnp.sum(p, axis=1, keepdims=True) + jnp.exp(sink - m)
                    pn = (p * (1.0 / den)).astype(BF16)
                    contrib = _dot(pn, v_slabs[half])
                    o_pair = contrib if o_pair is None else o_pair + contrib
                o_pairs.append(o_pair)
        o_blocks.append(jnp.concatenate(o_pairs, axis=1))
    o = jnp.concatenate(o_blocks, axis=0).astype(BF16)
    y_attn = _dot(o, w_bra_ref[...])
    kvbuf[0:ATT_BLOCK, :] = kvbuf[tq:tq + ATT_BLOCK, :]

    gates = _dot(xb, w_in_ref[:, GATE_START:D_IN]) + b_in_ref[:, GATE_START:D_IN]
    merged = _sigmoid(gates[:, 0:D_MODEL]) * y_pool + _sigmoid(gates[:, D_MODEL:2 * D_MODEL]) * y_attn
    mix = _dot(merged.astype(BF16), w_out_ref[...])
    o_ref[...] = _layer_norm(ALPHA * x + mix, ln1g_ref[...], ln1b_ref[...])


def _mixer(x, sinks, ln0_g, ln0_b, w_in, b_in, w_pool, pool_scale, w_br_pool, w_br_attn, w_out, ln1_g, ln1_b,
           *, layer, pre_ln):
    B, S, D = x.shape
    tq = MIXER_TILE
    per_layer3 = lambda *shape: pl.BlockSpec((None,) + shape, lambda b, s: (layer,) + (0,) * len(shape))
    return pl.pallas_call(
        functools.partial(_mixer_kernel, layer=layer, pre_ln=pre_ln, tq=tq),
        grid=(B, S // tq),
        in_specs=[
            pl.BlockSpec(memory_space=pltpu.SMEM),
            pl.BlockSpec((None, tq, D), lambda b, s: (b, s, 0)),
            pl.BlockSpec((1, D), lambda b, s: (0, 0)),
            pl.BlockSpec((1, D), lambda b, s: (0, 0)),
            per_layer3(D, D_IN),
            per_layer3(1, D_IN),
            per_layer3(POOL_GROUPS, POOL_GROUP_CH, POOL_GROUP_CH),
            per_layer3(1, POOL_WIDTH),
            per_layer3(POOL_WIDTH, D),
            per_layer3(Q_WIDTH, D),
            per_layer3(D, D),
            per_layer3(1, D),
            per_layer3(1, D),
        ],
        out_specs=pl.BlockSpec((None, tq, D), lambda b, s: (b, s, 0)),
        out_shape=jax.ShapeDtypeStruct((B, S, D), F32),
        scratch_shapes=[
            pltpu.VMEM((POOL_HALO + tq, POOL_WIDTH), F32),
            pltpu.VMEM((ATT_BLOCK + tq, 8 * LANES), BF16),
        ],
        compiler_params=pltpu.CompilerParams(
            dimension_semantics=("arbitrary", "arbitrary"), vmem_limit_bytes=VMEM_LIMIT),
        name=f"mixer_l{layer}",
    )(sinks, x, ln0_g, ln0_b, w_in, b_in, w_pool, pool_scale, w_br_pool, w_br_attn, w_out, ln1_g, ln1_b)


def _first_index_of_max(vals, iota, n):
    m = jnp.max(vals, axis=0, keepdims=True)
    idx = jnp.min(jnp.where(vals == m, iota, n), axis=0, keepdims=True)
    return m, idx


def _router_kernel(x_ref, wr_ref, bias_ref, ek_ref, rk_ref, gk_ref, cnt_ref, carry, *, tr):
    @pl.when(pl.program_id(0) == 0)
    def _():
        carry[...] = jnp.zeros_like(carry)

    logits = _dot_nt(wr_ref[...], x_ref[...].astype(BF16))
    scores = _sigmoid(logits)
    biased = scores + bias_ref[...]
    neg_inf = -jnp.inf

    io8 = lax.broadcasted_iota(jnp.int32, (EXPERTS_PER_GROUP, tr), 0)
    group_rows = []
    for g in range(N_EXPERT_GROUPS):
        blk = biased[g * EXPERTS_PER_GROUP:(g + 1) * EXPERTS_PER_GROUP]
        m1, i1 = _first_index_of_max(blk, io8, EXPERTS_PER_GROUP)
        m2 = jnp.max(jnp.where(io8 == i1, neg_inf, blk), axis=0, keepdims=True)
        group_rows.append(m1 + m2)
    gscore = jnp.concatenate(group_rows, axis=0)
    iog = lax.broadcasted_iota(jnp.int32, (N_EXPERT_GROUPS, tr), 0)
    keep = jnp.zeros((N_EXPERT_GROUPS, tr), F32)
    for _ in range(TOPK_GROUPS):
        _, gi = _first_index_of_max(gscore, iog, N_EXPERT_GROUPS)
        hit = iog == gi
        keep = jnp.where(hit, 1.0, keep)
        gscore = jnp.where(hit, neg_inf, gscore)
    masked = jnp.concatenate(
        [jnp.where(keep[g:g + 1] > 0.0, biased[g * EXPERTS_PER_GROUP:(g + 1) * EXPERTS_PER_GROUP], neg_inf)
         for g in range(N_EXPERT_GROUPS)], axis=0)

    ioe = lax.broadcasted_iota(jnp.int32, (N_EXPERTS, tr), 0)
    sel = jnp.zeros((N_EXPERTS, tr), F32)
    e_rows, s_rows, hits = [], [], []
    for _ in range(TOP_K):
        _, ei = _first_index_of_max(masked, ioe, N_EXPERTS)
        hit = ioe == ei
        sel = jnp.where(hit, 1.0, sel)
        masked = jnp.where(hit, neg_inf, masked)
        e_rows.append(ei)
        hits.append(hit)
        s_rows.append(jnp.sum(jnp.where(hit, scores, 0.0), axis=0, keepdims=True))
    sel_scores = jnp.concatenate(s_rows, axis=0)
    gk_ref[...] = sel_scores / jnp.sum(sel_scores, axis=0, keepdims=True) * ROUTED_SCALE
    ek_ref[...] = jnp.concatenate(e_rows, axis=0)

    before = (lax.broadcasted_iota(jnp.int32, (tr, tr), 0) < lax.broadcasted_iota(jnp.int32, (tr, tr), 1))
    prefix = _dot(sel.astype(BF16), jnp.where(before, 1.0, 0.0).astype(BF16))
    rank_full = prefix + carry[...]
    rk_ref[...] = jnp.concatenate(
        [jnp.sum(jnp.where(hit, rank_full, 0.0), axis=0, keepdims=True) for hit in hits], axis=0).astype(jnp.int32)
    total = carry[...] + jnp.sum(sel, axis=1, keepdims=True)
    carry[...] = total
    cnt_ref[...] = jnp.broadcast_to(total, (N_EXPERTS, LANES))


def _router(x1, w_router_t, bias_col, *, layer):
    T, D = x1.shape
    tr = ROUTER_TILE
    row_spec = pl.BlockSpec((TOP_K, tr), lambda i: (0, i))
    return pl.pallas_call(
        functools.partial(_router_kernel, tr=tr),
        grid=(T // tr,),
        in_specs=[
            pl.BlockSpec((tr, D), lambda i: (i, 0)),
            pl.BlockSpec((None, N_EXPERTS, D), lambda i: (layer, 0, 0)),
            pl.BlockSpec((None, N_EXPERTS, 1), lambda i: (layer, 0, 0)),
        ],
        out_specs=[row_spec, row_spec, row_spec, pl.BlockSpec((N_EXPERTS, LANES), lambda i: (0, 0))],
        out_shape=[
            jax.ShapeDtypeStruct((TOP_K, T), jnp.int32),
            jax.ShapeDtypeStruct((TOP_K, T), jnp.int32),
            jax.ShapeDtypeStruct((TOP_K, T), F32),
            jax.ShapeDtypeStruct((N_EXPERTS, LANES), F32),
        ],
        scratch_shapes=[pltpu.VMEM((N_EXPERTS, 1), F32)],
        compiler_params=pltpu.CompilerParams(dimension_semantics=("arbitrary",), vmem_limit_bytes=VMEM_LIMIT),
        name=f"router_l{layer}",
    )(x1, w_router_t, bias_col)


def _row_copy(src_hbm, dst_vmem, sem, src_row, dst_row):
    return pltpu.make_async_copy(src_hbm.at[pl.ds(src_row, 1), :], dst_vmem.at[pl.ds(dst_row, 1), :], sem)


def _start_row_gather(src_hbm, idx_smem, dst_vmem, sem, n_rows):
    chunks_per_row = LANES // DMA_UNROLL
    shift = chunks_per_row.bit_length() - 1
    assert chunks_per_row == 1 << shift and n_rows <= IDX_SUBLANES * LANES

    def body(c, carry):
        idx_row = lax.shift_right_logical(c, shift)
        col0 = (c & (chunks_per_row - 1)) * DMA_UNROLL
        for j in range(DMA_UNROLL):
            _row_copy(src_hbm, dst_vmem, sem, idx_smem[idx_row, col0 + j], c * DMA_UNROLL + j).start()
        return carry
    lax.fori_loop(0, n_rows // DMA_UNROLL, body, 0)


def _wait_row_gather(src_hbm, dst_vmem, sem, n_rows):
    pltpu.make_async_copy(src_hbm.at[pl.ds(0, n_rows), :], dst_vmem, sem).wait()


def _expert_kernel(be_ref, nu_ref, inv_hbm, x_hbm, wg_ref, wu_ref, wd_ref, y_ref,
                   idx_smem, xbuf, wgu_bf, wd_bf, sem_idx, sem_x, *, blk):
    i = pl.program_id(0)
    n_used = nu_ref[0]
    slot = lax.rem(i, 2)
    nslot = 1 - slot

    def idx_copy(block, sl):
        return pltpu.make_async_copy(inv_hbm.at[block], idx_smem.at[sl], sem_idx.at[sl])

    @pl.when(i == 0)
    def _():
        idx_copy(0, 0).start()
        idx_copy(0, 0).wait()
        _start_row_gather(x_hbm, idx_smem.at[0], xbuf.at[0], sem_x.at[0], blk)

        @pl.when(n_used > 1)
        def _():
            idx_copy(1, 1).start()

    @pl.when(i + 1 < n_used)
    def _():
        idx_copy(i + 1, nslot).wait()
        _start_row_gather(x_hbm, idx_smem.at[nslot], xbuf.at[nslot], sem_x.at[nslot], blk)

    @pl.when(i + 2 < n_used)
    def _():
        idx_copy(i + 2, slot).start()

    @pl.when(i < n_used)
    def _():
        new_expert = (i == 0) | (be_ref[i] != be_ref[jnp.maximum(i - 1, 0)])

        @pl.when(new_expert)
        def _():
            wgu_bf[:, 0:EXPERT_FF] = wg_ref[...].astype(BF16)
            wgu_bf[:, EXPERT_FF:2 * EXPERT_FF] = wu_ref[...].astype(BF16)
            wd_bf[...] = wd_ref[...].astype(BF16)

        _wait_row_gather(x_hbm, xbuf.at[slot], sem_x.at[slot], blk)
        h = _dot(xbuf[slot].astype(BF16), wgu_bf[...])
        g = h[:, 0:EXPERT_FF]
        a = (g * _sigmoid(g) * h[:, EXPERT_FF:2 * EXPERT_FF]).astype(BF16)
        y_ref[...] = _dot(a, wd_bf[...])

    @pl.when(i >= n_used)
    def _():
        y_ref[...] = jnp.zeros_like(y_ref)


def _experts(blk_expert, n_used, inv_tok, x1, w_gate, w_up, w_down, *, layer):
    T, D = x1.shape
    nb = inv_tok.shape[0]
    blk = EXPERT_BLOCK
    w_spec = lambda *shape: pl.BlockSpec((None,) + shape, lambda i, be, nu: (layer * N_EXPERTS + be[i], 0, 0))
    return pl.pallas_call(
        functools.partial(_expert_kernel, blk=blk),
        grid_spec=pltpu.PrefetchScalarGridSpec(
            num_scalar_prefetch=2,
            grid=(nb,),
            in_specs=[
                pl.BlockSpec(memory_space=pl.ANY),
                pl.BlockSpec(memory_space=pl.ANY),
                w_spec(D, EXPERT_FF),
                w_spec(D, EXPERT_FF),
                w_spec(EXPERT_FF, D),
            ],
            out_specs=pl.BlockSpec((blk, D), lambda i, be, nu: (i, 0)),
            scratch_shapes=[
                pltpu.SMEM((2, IDX_SUBLANES, LANES), jnp.int32),
                pltpu.VMEM((2, blk, D), F32),
                pltpu.VMEM((D, 2 * EXPERT_FF), BF16),
                pltpu.VMEM((EXPERT_FF, D), BF16),
                pltpu.SemaphoreType.DMA((2,)),
                pltpu.SemaphoreType.DMA((2,)),
            ],
        ),
        out_shape=jax.ShapeDtypeStruct((nb * blk, D), F32),
        compiler_params=pltpu.CompilerParams(dimension_semantics=("arbitrary",), vmem_limit_bytes=VMEM_LIMIT),
        name=f"experts_l{layer}",
    )(blk_expert, n_used, inv_tok, x1, w_gate, w_up, w_down)


def _combine_kernel(pos_hbm, y_hbm, x_ref, gate_ref, wsg_ref, wsu_ref, wsd_ref, ln2g_ref, ln2b_ref, o_ref,
                    idx_smem, ybuf, sem_idx, sem_y, *, tc, n_tiles):
    i = pl.program_id(0)
    slot = lax.rem(i, 2)
    nslot = 1 - slot
    n_rows = TOP_K * tc

    def idx_copy(tile, sl):
        return pltpu.make_async_copy(pos_hbm.at[tile], idx_smem.at[sl], sem_idx.at[sl])

    @pl.when(i == 0)
    def _():
        idx_copy(0, 0).start()
        idx_copy(0, 0).wait()
        _start_row_gather(y_hbm, idx_smem.at[0], ybuf.at[0], sem_y.at[0], n_rows)
        if n_tiles > 1:
            idx_copy(1, 1).start()

    @pl.when(i + 1 < n_tiles)
    def _():
        idx_copy(i + 1, nslot).wait()
        _start_row_gather(y_hbm, idx_smem.at[nslot], ybuf.at[nslot], sem_y.at[nslot], n_rows)

    @pl.when(i + 2 < n_tiles)
    def _():
        idx_copy(i + 2, slot).start()

    x = x_ref[...]
    xb = x.astype(BF16)
    g = _dot(xb, wsg_ref[...])
    a = (g * _sigmoid(g) * _dot(xb, wsu_ref[...])).astype(BF16)
    shared = _dot(a, wsd_ref[...])

    _wait_row_gather(y_hbm, ybuf.at[slot], sem_y.at[slot], n_rows)
    gate = gate_ref[...]
    routed = gate[:, 0:1] * ybuf[slot, 0:tc, :]
    for kk in range(1, TOP_K):
        routed = routed + gate[:, kk:kk + 1] * ybuf[slot, kk * tc:(kk + 1) * tc, :]
    o_ref[...] = _layer_norm(ALPHA * x + (routed + shared), ln2g_ref[...], ln2b_ref[...])


def _combine(pos_tiles, y_sorted, x1, gate_t, w_sg, w_su, w_sd, ln2_g, ln2_b, *, layer):
    T, D = x1.shape
    tc = COMBINE_TILE
    n_tiles = T // tc
    per_layer = lambda *shape: pl.BlockSpec((None,) + shape, lambda i: (layer,) + (0,) * len(shape))
    return pl.pallas_call(
        functools.partial(_combine_kernel, tc=tc, n_tiles=n_tiles),
        grid=(n_tiles,),
        in_specs=[
            pl.BlockSpec(memory_space=pl.ANY),
            pl.BlockSpec(memory_space=pl.ANY),
            pl.BlockSpec((tc, D), lambda i: (i, 0)),
            pl.BlockSpec((tc, TOP_K), lambda i: (i, 0)),
            per_layer(D, EXPERT_FF),
            per_layer(D, EXPERT_FF),
            per_layer(EXPERT_FF, D),
            per_layer(1, D),
            per_layer(1, D),
        ],
        out_specs=pl.BlockSpec((tc, D), lambda i: (i, 0)),
        out_shape=jax.ShapeDtypeStruct((T, D), F32),
        scratch_shapes=[
            pltpu.SMEM((2, IDX_SUBLANES, LANES), jnp.int32),
            pltpu.VMEM((2, TOP_K * tc, D), F32),
            pltpu.SemaphoreType.DMA((2,)),
            pltpu.SemaphoreType.DMA((2,)),
        ],
        compiler_params=pltpu.CompilerParams(dimension_semantics=("arbitrary",), vmem_limit_bytes=VMEM_LIMIT),
        name=f"combine_l{layer}",
    )(pos_tiles, y_sorted, x1, gate_t, w_sg, w_su, w_sd, ln2_g, ln2_b)


def _dispatch_plan(ek, rk, counts_f):
    T = ek.shape[1]
    blk = EXPERT_BLOCK
    nb = (T * TOP_K) // blk + N_EXPERTS
    counts = counts_f[:, 0].astype(jnp.int32)
    pcounts = (counts + blk - 1) // blk * blk
    pends = jnp.cumsum(pcounts)
    pstarts = pends - pcounts
    pos = jnp.take(pstarts, ek, axis=0) + rk
    n_used = (pends[-1] // blk).astype(jnp.int32).reshape(1)
    blk_expert = jnp.minimum(
        jnp.searchsorted(pends, jnp.arange(nb, dtype=jnp.int32) * blk, side="right"), N_EXPERTS - 1).astype(jnp.int32)
    last_used = jnp.take(blk_expert, jnp.maximum(n_used[0] - 1, 0))
    blk_expert = jnp.where(jnp.arange(nb) < n_used[0], blk_expert, last_used)
    tok = jnp.broadcast_to(jnp.arange(T, dtype=jnp.int32)[None, :], (TOP_K, T))
    inv_tok = jnp.zeros((nb * blk,), jnp.int32).at[pos.reshape(-1)].set(tok.reshape(-1))
    inv_tok = jnp.pad(inv_tok.reshape(nb, blk // LANES, LANES), ((0, 0), (0, IDX_SUBLANES - blk // LANES), (0, 0)))
    tc = COMBINE_TILE
    pos_tiles = pos.reshape(TOP_K, T // tc, tc).transpose(1, 0, 2).reshape(T // tc, IDX_SUBLANES, LANES)
    return blk_expert, n_used, inv_tok, pos_tiles


def kernel(x, ln0_g, ln0_b, w_in, b_in, w_pool, pool_scale, attn_sinks, w_br_pool, w_br_attn, w_out, ln1_g, ln1_b,
           w_router, router_bias, w_exp_gate, w_exp_up, w_exp_down, w_sh_gate, w_sh_up, w_sh_down, ln2_g, ln2_b):
    B, S, D = x.shape
    T = B * S
    depth = w_in.shape[0]
    row = lambda a: a.reshape(a.shape[0], 1, a.shape[1])
    w_in_b, w_pool_b = w_in.astype(BF16), w_pool.astype(BF16)
    w_brp_b, w_bra_b, w_out_b = w_br_pool.astype(BF16), w_br_attn.astype(BF16), w_out.astype(BF16)
    w_router_t = jnp.swapaxes(w_router, 1, 2).astype(BF16)
    bias_col = router_bias.reshape(depth, N_EXPERTS, 1)
    w_sg_b, w_su_b, w_sd_b = w_sh_gate.astype(BF16), w_sh_up.astype(BF16), w_sh_down.astype(BF16)
    w_eg = w_exp_gate.reshape(depth * N_EXPERTS, D, EXPERT_FF)
    w_eu = w_exp_up.reshape(depth * N_EXPERTS, D, EXPERT_FF)
    w_ed = w_exp_down.reshape(depth * N_EXPERTS, EXPERT_FF, D)
    ln0_g2, ln0_b2 = ln0_g.reshape(1, D), ln0_b.reshape(1, D)

    for l in range(depth):
        x = _mixer(x, attn_sinks, ln0_g2, ln0_b2, w_in_b, row(b_in), w_pool_b, row(pool_scale), w_brp_b, w_bra_b,
                   w_out_b, row(ln1_g), row(ln1_b), layer=l, pre_ln=(l == 0))
        x1 = x.reshape(T, D)
        ek, rk, gk, counts_f = _router(x1, w_router_t, bias_col, layer=l)
        blk_expert, n_used, inv_tok, pos_tiles = _dispatch_plan(ek, rk, counts_f)
        y_sorted = _experts(blk_expert, n_used, inv_tok, x1, w_eg, w_eu, w_ed, layer=l)
        x = _combine(pos_tiles, y_sorted, x1, gk.T, w_sg_b, w_su_b, w_sd_b, row(ln2_g), row(ln2_b),
                     layer=l).reshape(B, S, D)
    return x
```

```python
import functools

import jax
import jax.numpy as jnp
from jax import lax
from jax.experimental import pallas as pl
from jax.experimental.pallas import tpu as pltpu

D_MODEL = 1024
DEPTH = 4
POOL_GROUPS = 4
POOL_GROUP_CH = 128
POOL_WIDTH = POOL_GROUPS * POOL_GROUP_CH
POOL_WINDOWS = (2, 4, 8, 16)
POOL_HALO = 16
N_Q_HEADS = 8
N_KV_HEADS = 2
HEAD_DIM = 64
Q_WIDTH = N_Q_HEADS * HEAD_DIM
KV_WIDTH = N_KV_HEADS * HEAD_DIM
WINDOW = 128
ATT_BLOCK = 128
D_IN = POOL_WIDTH + Q_WIDTH + 2 * KV_WIDTH + 2 * D_MODEL
QKV_START = POOL_WIDTH
GATE_START = POOL_WIDTH + Q_WIDTH + 2 * KV_WIDTH
N_EXPERTS = 64
EXPERT_FF = 256
TOP_K = 8
N_EXPERT_GROUPS = 8
EXPERTS_PER_GROUP = N_EXPERTS // N_EXPERT_GROUPS
TOPK_GROUPS = 4
ROUTED_SCALE = 2.5
ALPHA = (2.0 * DEPTH) ** 0.25
LN_EPS = 1e-5
ALIBI_SLOPES = tuple(float(2.0 ** (-8.0 * h / N_Q_HEADS)) for h in range(1, N_Q_HEADS + 1))

LANES = 128
SUBLANES = 8
TILE_SUBLANES = D_MODEL // LANES
MIXER_TILE = 256
ROUTER_TILE = 512
EXPERT_BLOCK = 256
TOKEN_TILE = 128
VMEM_LIMIT = 48 * 1024 * 1024

BF16 = jnp.bfloat16
F32 = jnp.float32

assert TILE_SUBLANES == SUBLANES and TOP_K == SUBLANES and TOKEN_TILE == LANES


def _dot(a, b):
    return jnp.dot(a, b, preferred_element_type=F32)


def _dot_nt(a, b):
    return lax.dot_general(a, b, (((1,), (1,)), ((), ())), preferred_element_type=F32)


def _layer_norm(x, g, b):
    mu = jnp.mean(x, axis=-1, keepdims=True)
    xc = x - mu
    var = jnp.mean(xc * xc, axis=-1, keepdims=True)
    return xc * lax.rsqrt(var + LN_EPS) * g + b


def _sigmoid(x):
    return 1.0 / (1.0 + jnp.exp(-x))


def _load_token_tiles(ref, rows, row0=0):
    return jnp.concatenate(
        [ref[pl.ds(row0 * TILE_SUBLANES + j, rows, stride=TILE_SUBLANES), :] for j in range(TILE_SUBLANES)], axis=1)


def _store_token_tiles(ref, value):
    for j in range(TILE_SUBLANES):
        ref[pl.ds(j, value.shape[0], stride=TILE_SUBLANES), :] = value[:, j * LANES:(j + 1) * LANES]


def _token_tile(ref, row):
    return ref.at[pl.ds(pl.multiple_of(row * TILE_SUBLANES, TILE_SUBLANES), TILE_SUBLANES), :]


def _mixer_kernel(sinks_ref, x_ref, ln0g_ref, ln0b_ref, w_in_ref, b_in_ref, w_pool_ref, pscale_ref,
                  w_brp_ref, w_bra_ref, w_out_ref, ln1g_ref, ln1b_ref, o_ref, ubuf, kvbuf,
                  *, layer, pre_ln, tq):
    s = pl.program_id(1)

    @pl.when(s == 0)
    def _():
        ubuf[0:POOL_HALO, :] = jnp.zeros((POOL_HALO, POOL_WIDTH), F32)
        kvbuf[0:ATT_BLOCK, :] = jnp.zeros((ATT_BLOCK, 8 * LANES), BF16)

    x = x_ref[...]
    if pre_ln:
        x = _layer_norm(x, ln0g_ref[...], ln0b_ref[...])
    xb = x.astype(BF16)

    u = _dot(xb, w_in_ref[:, 0:POOL_WIDTH]) + b_in_ref[:, 0:POOL_WIDTH]
    ubuf[POOL_HALO:POOL_HALO + tq, :] = u
    pos = (s * tq + lax.broadcasted_iota(jnp.int32, (tq, 1), 0)).astype(F32)
    mixed_parts = []
    for g, w in enumerate(POOL_WINDOWS):
        sl = slice(g * POOL_GROUP_CH, (g + 1) * POOL_GROUP_CH)
        cur = ubuf[POOL_HALO:POOL_HALO + tq, sl]
        acc = cur
        for j in range(1, w):
            acc = acc + ubuf[POOL_HALO - j:POOL_HALO - j + tq, sl]
        inv_cnt = 1.0 / jnp.minimum(pos + 1.0, float(w))
        d = (acc * inv_cnt - cur).astype(BF16)
        mixed_parts.append(_dot(d, w_pool_ref[g]) * pscale_ref[:, sl])
    mixed = jnp.concatenate(mixed_parts, axis=1).astype(BF16)
    y_pool = _dot(mixed, w_brp_ref[...])
    ubuf[0:POOL_HALO, :] = ubuf[tq:tq + POOL_HALO, :]

    qkv = _dot(xb, w_in_ref[:, QKV_START:GATE_START]) + b_in_ref[:, QKV_START:GATE_START]
    q = qkv[:, 0:Q_WIDTH].astype(BF16)
    k = qkv[:, Q_WIDTH:Q_WIDTH + KV_WIDTH]
    v = qkv[:, Q_WIDTH + KV_WIDTH:Q_WIDTH + 2 * KV_WIDTH]
    lo = lax.broadcasted_iota(jnp.int32, (tq, LANES), 1) < HEAD_DIM
    k_sw = pltpu.roll(k, HEAD_DIM, axis=1)
    v_sw = pltpu.roll(v, HEAD_DIM, axis=1)
    zero = jnp.zeros((tq, LANES), F32)
    slabs = (
        jnp.where(lo, k, zero), jnp.where(lo, zero, k_sw),
        jnp.where(lo, k_sw, zero), jnp.where(lo, zero, k),
        jnp.where(lo, v, zero), jnp.where(lo, zero, v_sw),
        jnp.where(lo, v_sw, zero), jnp.where(lo, zero, v),
    )
    for i, slab in enumerate(slabs):
        kvbuf[ATT_BLOCK:ATT_BLOCK + tq, i * LANES:(i + 1) * LANES] = slab.astype(BF16)

    qi = lax.broadcasted_iota(jnp.int32, (ATT_BLOCK, 2 * ATT_BLOCK), 0)
    kj = lax.broadcasted_iota(jnp.int32, (ATT_BLOCK, 2 * ATT_BLOCK), 1)
    dist = ATT_BLOCK + qi - kj
    band_ok = (dist >= 0) & (dist < WINDOW)
    distf = dist.astype(F32)
    o_blocks = []
    for qb in range(tq // ATT_BLOCK):
        r0 = qb * ATT_BLOCK
        first_key_pos = s * tq + r0 - ATT_BLOCK
        valid = band_ok & (kj + first_key_pos >= 0)
        o_pairs = []
        for hk in range(N_KV_HEADS):
            k_slabs = [kvbuf[r0:r0 + 2 * ATT_BLOCK, (2 * hk + i) * LANES:(2 * hk + i + 1) * LANES] for i in range(2)]
            v_slabs = [kvbuf[r0:r0 + 2 * ATT_BLOCK, (4 + 2 * hk + i) * LANES:(4 + 2 * hk + i + 1) * LANES] for i in range(2)]
            for pj in range(2):
                pair = hk * 2 + pj
                qp = q[r0:r0 + ATT_BLOCK, pair * LANES:(pair + 1) * LANES]
                o_pair = None
                for half in range(2):
                    h = pair * 2 + half
                    sc = _dot_nt(qp, k_slabs[half]) * (HEAD_DIM ** -0.5) - ALIBI_SLOPES[h] * distf
                    sc = jnp.where(valid, sc, -jnp.inf)
                    sink = sinks_ref[layer, h]
                    m = jnp.maximum(jnp.max(sc, axis=1, keepdims=True), sink)
                    p = jnp.exp(sc - m)
                    den = jnp.sum(p, axis=1, keepdims=True) + jnp.exp(sink - m)
                    pn = (p * (1.0 / den)).astype(BF16)
                    contrib = _dot(pn, v_slabs[half])
                    o_pair = contrib if o_pair is None else o_pair + contrib
                o_pairs.append(o_pair)
        o_blocks.append(jnp.concatenate(o_pairs, axis=1))
    o = jnp.concatenate(o_blocks, axis=0).astype(BF16)
    y_attn = _dot(o, w_bra_ref[...])
    kvbuf[0:ATT_BLOCK, :] = kvbuf[tq:tq + ATT_BLOCK, :]

    gates = _dot(xb, w_in_ref[:, GATE_START:D_IN]) + b_in_ref[:, GATE_START:D_IN]
    merged = _sigmoid(gates[:, 0:D_MODEL]) * y_pool + _sigmoid(gates[:, D_MODEL:2 * D_MODEL]) * y_attn
    mix = _dot(merged.astype(BF16), w_out_ref[...])
    _store_token_tiles(o_ref, _layer_norm(ALPHA * x + mix, ln1g_ref[...], ln1b_ref[...]))


def _mixer(x, sinks, ln0_g, ln0_b, w_in, b_in, w_pool, pool_scale, w_br_pool, w_br_attn, w_out, ln1_g, ln1_b,
           *, layer, pre_ln):
    B, S, D = x.shape
    tq = MIXER_TILE
    per_layer3 = lambda *shape: pl.BlockSpec((None,) + shape, lambda b, s: (layer,) + (0,) * len(shape))
    return pl.pallas_call(
        functools.partial(_mixer_kernel, layer=layer, pre_ln=pre_ln, tq=tq),
        grid=(B, S // tq),
        in_specs=[
            pl.BlockSpec(memory_space=pltpu.SMEM),
            pl.BlockSpec((None, tq, D), lambda b, s: (b, s, 0)),
            pl.BlockSpec((1, D), lambda b, s: (0, 0)),
            pl.BlockSpec((1, D), lambda b, s: (0, 0)),
            per_layer3(D, D_IN),
            per_layer3(1, D_IN),
            per_layer3(POOL_GROUPS, POOL_GROUP_CH, POOL_GROUP_CH),
            per_layer3(1, POOL_WIDTH),
            per_layer3(POOL_WIDTH, D),
            per_layer3(Q_WIDTH, D),
            per_layer3(D, D),
            per_layer3(1, D),
            per_layer3(1, D),
        ],
        out_specs=pl.BlockSpec((tq * TILE_SUBLANES, LANES), lambda b, s: (b * (S // tq) + s, 0)),
        out_shape=jax.ShapeDtypeStruct((B * S * TILE_SUBLANES, LANES), F32),
        scratch_shapes=[
            pltpu.VMEM((POOL_HALO + tq, POOL_WIDTH), F32),
            pltpu.VMEM((ATT_BLOCK + tq, 8 * LANES), BF16),
        ],
        compiler_params=pltpu.CompilerParams(
            dimension_semantics=("arbitrary", "arbitrary"), vmem_limit_bytes=VMEM_LIMIT),
        name=f"mixer_l{layer}",
    )(sinks, x, ln0_g, ln0_b, w_in, b_in, w_pool, pool_scale, w_br_pool, w_br_attn, w_out, ln1_g, ln1_b)


def _first_index_of_max(vals, iota, n):
    m = jnp.max(vals, axis=0, keepdims=True)
    idx = jnp.min(jnp.where(vals == m, iota, n), axis=0, keepdims=True)
    return m, idx


def _router_kernel(x_ref, wr_ref, bias_ref, ek_ref, rk_ref, gk_ref, cnt_ref, carry, *, tr):
    @pl.when(pl.program_id(0) == 0)
    def _():
        carry[...] = jnp.zeros_like(carry)

    logits = _dot_nt(wr_ref[...], _load_token_tiles(x_ref, tr).astype(BF16))
    scores = _sigmoid(logits)
    biased = scores + bias_ref[...]
    neg_inf = -jnp.inf

    io8 = lax.broadcasted_iota(jnp.int32, (EXPERTS_PER_GROUP, tr), 0)
    group_rows = []
    for g in range(N_EXPERT_GROUPS):
        blk = biased[g * EXPERTS_PER_GROUP:(g + 1) * EXPERTS_PER_GROUP]
        m1, i1 = _first_index_of_max(blk, io8, EXPERTS_PER_GROUP)
        m2 = jnp.max(jnp.where(io8 == i1, neg_inf, blk), axis=0, keepdims=True)
        group_rows.append(m1 + m2)
    gscore = jnp.concatenate(group_rows, axis=0)
    iog = lax.broadcasted_iota(jnp.int32, (N_EXPERT_GROUPS, tr), 0)
    keep = jnp.zeros((N_EXPERT_GROUPS, tr), F32)
    for _ in range(TOPK_GROUPS):
        _, gi = _first_index_of_max(gscore, iog, N_EXPERT_GROUPS)
        hit = iog == gi
        keep = jnp.where(hit, 1.0, keep)
        gscore = jnp.where(hit, neg_inf, gscore)
    masked = jnp.concatenate(
        [jnp.where(keep[g:g + 1] > 0.0, biased[g * EXPERTS_PER_GROUP:(g + 1) * EXPERTS_PER_GROUP], neg_inf)
         for g in range(N_EXPERT_GROUPS)], axis=0)

    ioe = lax.broadcasted_iota(jnp.int32, (N_EXPERTS, tr), 0)
    sel = jnp.zeros((N_EXPERTS, tr), F32)
    e_rows, s_rows, hits = [], [], []
    for _ in range(TOP_K):
        _, ei = _first_index_of_max(masked, ioe, N_EXPERTS)
        hit = ioe == ei
        sel = jnp.where(hit, 1.0, sel)
        masked = jnp.where(hit, neg_inf, masked)
        e_rows.append(ei)
        hits.append(hit)
        s_rows.append(jnp.sum(jnp.where(hit, scores, 0.0), axis=0, keepdims=True))
    sel_scores = jnp.concatenate(s_rows, axis=0)
    gk_ref[...] = sel_scores / jnp.sum(sel_scores, axis=0, keepdims=True) * ROUTED_SCALE
    ek_ref[...] = jnp.concatenate(e_rows, axis=0)

    before = (lax.broadcasted_iota(jnp.int32, (tr, tr), 0) < lax.broadcasted_iota(jnp.int32, (tr, tr), 1))
    prefix = _dot(sel.astype(BF16), jnp.where(before, 1.0, 0.0).astype(BF16))
    rank_full = prefix + carry[...]
    rk_ref[...] = jnp.concatenate(
        [jnp.sum(jnp.where(hit, rank_full, 0.0), axis=0, keepdims=True) for hit in hits], axis=0).astype(jnp.int32)
    total = carry[...] + jnp.sum(sel, axis=1, keepdims=True)
    carry[...] = total
    cnt_ref[...] = jnp.broadcast_to(total, (N_EXPERTS, LANES))


def _router(x1, w_router_t, bias_col, *, layer):
    T = x1.shape[0] // TILE_SUBLANES
    tr = ROUTER_TILE
    row_spec = pl.BlockSpec((TOP_K, tr), lambda i: (0, i))
    return pl.pallas_call(
        functools.partial(_router_kernel, tr=tr),
        grid=(T // tr,),
        in_specs=[
            pl.BlockSpec((tr * TILE_SUBLANES, LANES), lambda i: (i, 0)),
            pl.BlockSpec((None, N_EXPERTS, D_MODEL), lambda i: (layer, 0, 0)),
            pl.BlockSpec((None, N_EXPERTS, 1), lambda i: (layer, 0, 0)),
        ],
        out_specs=[row_spec, row_spec, row_spec, pl.BlockSpec((N_EXPERTS, LANES), lambda i: (0, 0))],
        out_shape=[
            jax.ShapeDtypeStruct((TOP_K, T), jnp.int32),
            jax.ShapeDtypeStruct((TOP_K, T), jnp.int32),
            jax.ShapeDtypeStruct((TOP_K, T), F32),
            jax.ShapeDtypeStruct((N_EXPERTS, LANES), F32),
        ],
        scratch_shapes=[pltpu.VMEM((N_EXPERTS, 1), F32)],
        compiler_params=pltpu.CompilerParams(dimension_semantics=("arbitrary",), vmem_limit_bytes=VMEM_LIMIT),
        name=f"router_l{layer}",
    )(x1, w_router_t, bias_col)


def _for_each_assignment(fn):
    for k in range(TOP_K):
        for j in range(TOKEN_TILE):
            fn(k, j)


def _dispatch_kernel(fill_ref, nu_ref, pos_hbm, x_ref, xs_hbm, idx_smem, zbuf, sem_idx, sem_out, sem_fill,
                     *, n_tiles, n_blocks):
    i = pl.program_id(0)
    slot = lax.rem(i, 2)

    def idx_copy(tile, sl):
        return pltpu.make_async_copy(pos_hbm.at[tile], idx_smem.at[sl], sem_idx.at[sl])

    def fill_copy(row0, rows):
        return pltpu.make_async_copy(zbuf.at[pl.ds(0, rows)], xs_hbm.at[pl.ds(row0, rows)], sem_fill)

    def pad_fill(e, wait):
        row0 = fill_ref[e]
        n_pad = (0 - row0) & (EXPERT_BLOCK - 1)
        piece = EXPERT_BLOCK // 2
        while piece >= 1:
            has = (n_pad & piece) != 0

            @pl.when(has)
            def _(row0=row0, piece=piece):
                fill_copy(0 if wait else row0, piece).wait() if wait else fill_copy(row0, piece).start()
            row0 = row0 + jnp.where(has, piece, 0)
            piece //= 2

    @pl.when(i == 0)
    def _():
        idx_copy(0, 0).start()
        zbuf[...] = jnp.zeros_like(zbuf)
        lax.fori_loop(0, N_EXPERTS, lambda e, c: (pad_fill(e, False), c)[1], 0)
        lax.fori_loop(nu_ref[0], n_blocks, lambda b, c: (fill_copy(b * EXPERT_BLOCK, EXPERT_BLOCK).start(), c)[1], 0)
        lax.fori_loop(0, N_EXPERTS, lambda e, c: (pad_fill(e, True), c)[1], 0)
        lax.fori_loop(nu_ref[0], n_blocks, lambda b, c: (fill_copy(0, EXPERT_BLOCK).wait(), c)[1], 0)

    idx_copy(i, slot).wait()

    @pl.when(i + 1 < n_tiles)
    def _():
        idx_copy(i + 1, 1 - slot).start()

    def send(k, j):
        pltpu.make_async_copy(_token_tile(x_ref, j), xs_hbm.at[idx_smem[slot, k, j]], sem_out).start()
    _for_each_assignment(send)

    for _ in range(TOP_K):
        pltpu.make_async_copy(x_ref, x_ref, sem_out).wait()


def _dispatch(fill_start, n_used, pos_tiles, x1, n_sorted_rows, *, layer):
    T = x1.shape[0] // TILE_SUBLANES
    n_tiles = T // TOKEN_TILE
    return pl.pallas_call(
        functools.partial(_dispatch_kernel, n_tiles=n_tiles, n_blocks=n_sorted_rows // EXPERT_BLOCK),
        grid_spec=pltpu.PrefetchScalarGridSpec(
            num_scalar_prefetch=2,
            grid=(n_tiles,),
            in_specs=[
                pl.BlockSpec(memory_space=pl.ANY),
                pl.BlockSpec((TOKEN_TILE * TILE_SUBLANES, LANES), lambda i, fill, nu: (i, 0)),
            ],
            out_specs=pl.BlockSpec(memory_space=pl.ANY),
            scratch_shapes=[
                pltpu.SMEM((2, TOP_K, TOKEN_TILE), jnp.int32),
                pltpu.VMEM((EXPERT_BLOCK, TILE_SUBLANES, LANES), F32),
                pltpu.SemaphoreType.DMA((2,)),
                pltpu.SemaphoreType.DMA(()),
                pltpu.SemaphoreType.DMA(()),
            ],
        ),
        out_shape=jax.ShapeDtypeStruct((n_sorted_rows, TILE_SUBLANES, LANES), F32),
        compiler_params=pltpu.CompilerParams(dimension_semantics=("arbitrary",), vmem_limit_bytes=VMEM_LIMIT),
        name=f"dispatch_l{layer}",
    )(fill_start, n_used, pos_tiles, x1)


def _expert_kernel(be_ref, nu_ref, xs_ref, wg_ref, wu_ref, wd_ref, y_ref, wgu_bf, wd_bf):
    i = pl.program_id(0)

    @pl.when(i < nu_ref[0])
    def _():
        new_expert = (i == 0) | (be_ref[i] != be_ref[jnp.maximum(i - 1, 0)])

        @pl.when(new_expert)
        def _():
            wgu_bf[:, 0:EXPERT_FF] = wg_ref[...].astype(BF16)
            wgu_bf[:, EXPERT_FF:2 * EXPERT_FF] = wu_ref[...].astype(BF16)
            wd_bf[...] = wd_ref[...].astype(BF16)

        h = _dot(_load_token_tiles(xs_ref, EXPERT_BLOCK).astype(BF16), wgu_bf[...])
        g = h[:, 0:EXPERT_FF]
        a = (g * _sigmoid(g) * h[:, EXPERT_FF:2 * EXPERT_FF]).astype(BF16)
        _store_token_tiles(y_ref, _dot(a, wd_bf[...]))

    @pl.when(i >= nu_ref[0])
    def _():
        y_ref[...] = jnp.zeros_like(y_ref)


def _experts(blk_expert, n_used, xs, w_gate, w_up, w_down, *, layer):
    nb = blk_expert.shape[0]
    blk = EXPERT_BLOCK
    used = lambda i, nu: jnp.minimum(i, nu[0] - 1)
    w_spec = lambda *shape: pl.BlockSpec(
        (None,) + shape, lambda i, be, nu: (layer * N_EXPERTS + be[used(i, nu)], 0, 0))
    in_row_spec = pl.BlockSpec((blk * TILE_SUBLANES, LANES), lambda i, be, nu: (used(i, nu), 0))
    return pl.pallas_call(
        _expert_kernel,
        grid_spec=pltpu.PrefetchScalarGridSpec(
            num_scalar_prefetch=2,
            grid=(nb,),
            in_specs=[in_row_spec, w_spec(D_MODEL, EXPERT_FF), w_spec(D_MODEL, EXPERT_FF), w_spec(EXPERT_FF, D_MODEL)],
            out_specs=pl.BlockSpec((blk * TILE_SUBLANES, LANES), lambda i, be, nu: (i, 0)),
            scratch_shapes=[
                pltpu.VMEM((D_MODEL, 2 * EXPERT_FF), BF16),
                pltpu.VMEM((EXPERT_FF, D_MODEL), BF16),
            ],
        ),
        out_shape=jax.ShapeDtypeStruct((nb * blk * TILE_SUBLANES, LANES), F32),
        compiler_params=pltpu.CompilerParams(dimension_semantics=("arbitrary",), vmem_limit_bytes=VMEM_LIMIT),
        name=f"experts_l{layer}",
    )(blk_expert, n_used, xs, w_gate, w_up, w_down)


def _combine_kernel(pos_hbm, y_hbm, x_ref, gate_ref, wsg_ref, wsu_ref, wsd_ref, ln2g_ref, ln2b_ref, o_ref,
                    idx_smem, ybuf, sem_idx, sem_y, *, n_tiles):
    i = pl.program_id(0)
    slot = lax.rem(i, 2)
    nslot = 1 - slot
    rows_per_slot = TOP_K * TOKEN_TILE

    def idx_copy(tile, sl):
        return pltpu.make_async_copy(pos_hbm.at[tile], idx_smem.at[sl], sem_idx.at[sl])

    def start_gather(sl):
        def fetch(k, j):
            pltpu.make_async_copy(y_hbm.at[idx_smem[sl, k, j]],
                                  _token_tile(ybuf, sl * rows_per_slot + k * TOKEN_TILE + j), sem_y.at[sl]).start()
        _for_each_assignment(fetch)

    @pl.when(i == 0)
    def _():
        idx_copy(0, 0).start()
        idx_copy(0, 0).wait()
        start_gather(0)
        if n_tiles > 1:
            idx_copy(1, 1).start()

    @pl.when(i + 1 < n_tiles)
    def _():
        idx_copy(i + 1, nslot).wait()
        start_gather(nslot)

    @pl.when(i + 2 < n_tiles)
    def _():
        idx_copy(i + 2, slot).start()

    x = _load_token_tiles(x_ref, TOKEN_TILE)
    xb = x.astype(BF16)
    g = _dot(xb, wsg_ref[...])
    a = (g * _sigmoid(g) * _dot(xb, wsu_ref[...])).astype(BF16)
    shared = _dot(a, wsd_ref[...])

    base = slot * rows_per_slot
    slot_rows = ybuf.at[pl.ds(pl.multiple_of(base * TILE_SUBLANES, TILE_SUBLANES), rows_per_slot * TILE_SUBLANES), :]
    pltpu.make_async_copy(slot_rows, slot_rows, sem_y.at[slot]).wait()
    gate = gate_ref[...]
    routed = None
    for k in range(TOP_K):
        term = gate[:, k:k + 1] * _load_token_tiles(ybuf, TOKEN_TILE, base + k * TOKEN_TILE)
        routed = term if routed is None else routed + term
    o_ref[...] = _layer_norm(ALPHA * x + (routed + shared), ln2g_ref[...], ln2b_ref[...])


def _combine(pos_tiles, y_sorted, x1, gate_t, w_sg, w_su, w_sd, ln2_g, ln2_b, *, layer):
    T = x1.shape[0] // TILE_SUBLANES
    tc = TOKEN_TILE
    n_tiles = T // tc
    per_layer = lambda *shape: pl.BlockSpec((None,) + shape, lambda i: (layer,) + (0,) * len(shape))
    return pl.pallas_call(
        functools.partial(_combine_kernel, n_tiles=n_tiles),
        grid=(n_tiles,),
        in_specs=[
            pl.BlockSpec(memory_space=pl.ANY),
            pl.BlockSpec(memory_space=pl.ANY),
            pl.BlockSpec((tc * TILE_SUBLANES, LANES), lambda i: (i, 0)),
            pl.BlockSpec((tc, TOP_K), lambda i: (i, 0)),
            per_layer(D_MODEL, EXPERT_FF),
            per_layer(D_MODEL, EXPERT_FF),
            per_layer(EXPERT_FF, D_MODEL),
            per_layer(1, D_MODEL),
            per_layer(1, D_MODEL),
        ],
        out_specs=pl.BlockSpec((tc, D_MODEL), lambda i: (i, 0)),
        out_shape=jax.ShapeDtypeStruct((T, D_MODEL), F32),
        scratch_shapes=[
            pltpu.SMEM((2, TOP_K, TOKEN_TILE), jnp.int32),
            pltpu.VMEM((2 * TOP_K * tc * TILE_SUBLANES, LANES), F32),
            pltpu.SemaphoreType.DMA((2,)),
            pltpu.SemaphoreType.DMA((2,)),
        ],
        compiler_params=pltpu.CompilerParams(dimension_semantics=("arbitrary",), vmem_limit_bytes=VMEM_LIMIT),
        name=f"combine_l{layer}",
    )(pos_tiles, y_sorted, x1, gate_t, w_sg, w_su, w_sd, ln2_g, ln2_b)


def _dispatch_plan(ek, rk, counts_f):
    T = ek.shape[1]
    blk = EXPERT_BLOCK
    nb = (T * TOP_K) // blk + N_EXPERTS
    experts = jnp.arange(N_EXPERTS, dtype=jnp.int32)
    counts = counts_f[:, 0].astype(jnp.int32)
    pcounts = (counts + blk - 1) // blk * blk
    pends = jnp.sum(jnp.where(experts[None, :] <= experts[:, None], pcounts[None, :], 0), axis=1)
    pstarts = pends - pcounts
    pos = jnp.sum(jnp.where(ek[None] == experts[:, None, None], pstarts[:, None, None], 0), axis=0) + rk
    n_used = (pends[-1] // blk).reshape(1)
    block_row0 = jnp.arange(nb, dtype=jnp.int32) * blk
    blk_expert = jnp.minimum(jnp.sum((pends[None, :] <= block_row0[:, None]).astype(jnp.int32), axis=1), N_EXPERTS - 1)
    fill_start = pstarts + counts
    pos_tiles = pos.reshape(TOP_K, T // TOKEN_TILE, TOKEN_TILE).transpose(1, 0, 2)
    return blk_expert, n_used, fill_start, pos_tiles, nb * blk


def kernel(x, ln0_g, ln0_b, w_in, b_in, w_pool, pool_scale, attn_sinks, w_br_pool, w_br_attn, w_out, ln1_g, ln1_b,
           w_router, router_bias, w_exp_gate, w_exp_up, w_exp_down, w_sh_gate, w_sh_up, w_sh_down, ln2_g, ln2_b):
    B, S, D = x.shape
    depth = w_in.shape[0]
    row = lambda a: a.reshape(a.shape[0], 1, a.shape[1])
    w_in_b, w_pool_b = w_in.astype(BF16), w_pool.astype(BF16)
    w_brp_b, w_bra_b, w_out_b = w_br_pool.astype(BF16), w_br_attn.astype(BF16), w_out.astype(BF16)
    w_router_t = jnp.swapaxes(w_router, 1, 2).astype(BF16)
    bias_col = router_bias.reshape(depth, N_EXPERTS, 1)
    w_sg_b, w_su_b, w_sd_b = w_sh_gate.astype(BF16), w_sh_up.astype(BF16), w_sh_down.astype(BF16)
    w_eg = w_exp_gate.reshape(depth * N_EXPERTS, D, EXPERT_FF)
    w_eu = w_exp_up.reshape(depth * N_EXPERTS, D, EXPERT_FF)
    w_ed = w_exp_down.reshape(depth * N_EXPERTS, EXPERT_FF, D)
    ln0_g2, ln0_b2 = ln0_g.reshape(1, D), ln0_b.reshape(1, D)

    for l in range(depth):
        x1 = _mixer(x, attn_sinks, ln0_g2, ln0_b2, w_in_b, row(b_in), w_pool_b, row(pool_scale), w_brp_b, w_bra_b,
                    w_out_b, row(ln1_g), row(ln1_b), layer=l, pre_ln=(l == 0))
        ek, rk, gk, counts_f = _router(x1, w_router_t, bias_col, layer=l)
        blk_expert, n_used, fill_start, pos_tiles, n_sorted = _dispatch_plan(ek, rk, counts_f)
        xs = _dispatch(fill_start, n_used, pos_tiles, x1, n_sorted + EXPERT_BLOCK, layer=l)
        y_sorted = _experts(blk_expert, n_used, xs.reshape(-1, LANES), w_eg, w_eu, w_ed, layer=l)
        x = _combine(pos_tiles, y_sorted.reshape(-1, TILE_SUBLANES, LANES), x1, gk.T, w_sg_b, w_su_b, w_sd_b,
                     row(ln2_g), row(ln2_b), layer=l).reshape(B, S, D)
    return x
```

```python
import functools

import jax
import jax.numpy as jnp
from jax import lax
from jax.experimental import pallas as pl
from jax.experimental.pallas import tpu as pltpu

D_MODEL = 1024
DEPTH = 4
POOL_GROUPS = 4
POOL_GROUP_CH = 128
POOL_WIDTH = POOL_GROUPS * POOL_GROUP_CH
POOL_WINDOWS = (2, 4, 8, 16)
POOL_HALO = 16
N_Q_HEADS = 8
N_KV_HEADS = 2
HEAD_DIM = 64
Q_WIDTH = N_Q_HEADS * HEAD_DIM
KV_WIDTH = N_KV_HEADS * HEAD_DIM
WINDOW = 128
ATT_BLOCK = 128
D_IN = POOL_WIDTH + Q_WIDTH + 2 * KV_WIDTH + 2 * D_MODEL
QKV_START = POOL_WIDTH
GATE_START = POOL_WIDTH + Q_WIDTH + 2 * KV_WIDTH
N_EXPERTS = 64
EXPERT_FF = 256
TOP_K = 8
N_EXPERT_GROUPS = 8
EXPERTS_PER_GROUP = N_EXPERTS // N_EXPERT_GROUPS
TOPK_GROUPS = 4
ROUTED_SCALE = 2.5
ALPHA = (2.0 * DEPTH) ** 0.25
LN_EPS = 1e-5
ALIBI_SLOPES = tuple(float(2.0 ** (-8.0 * h / N_Q_HEADS)) for h in range(1, N_Q_HEADS + 1))

LANES = 128
SUBLANES = 8
TILE_SUBLANES = D_MODEL // LANES
MIXER_TILE = 256
ROUTER_TILE = 512
EXPERT_BLOCK = 512
EXPERT_CHUNK = 256
TOKEN_TILE = 128
VMEM_LIMIT = 48 * 1024 * 1024

BF16 = jnp.bfloat16
F32 = jnp.float32

assert TILE_SUBLANES == SUBLANES and TOP_K == SUBLANES and TOKEN_TILE == LANES


def _dot(a, b):
    return jnp.dot(a, b, preferred_element_type=F32)


def _dot_nt(a, b):
    return lax.dot_general(a, b, (((1,), (1,)), ((), ())), preferred_element_type=F32)


def _layer_norm(x, g, b):
    mu = jnp.mean(x, axis=-1, keepdims=True)
    xc = x - mu
    var = jnp.mean(xc * xc, axis=-1, keepdims=True)
    return xc * lax.rsqrt(var + LN_EPS) * g + b


def _sigmoid(x):
    return 1.0 / (1.0 + jnp.exp(-x))


def _load_token_tiles(ref, rows, row0=0):
    return jnp.concatenate(
        [ref[pl.ds(row0 * TILE_SUBLANES + j, rows, stride=TILE_SUBLANES), :] for j in range(TILE_SUBLANES)], axis=1)


def _store_token_tiles(ref, value, row0=0):
    for j in range(TILE_SUBLANES):
        ref[pl.ds(row0 * TILE_SUBLANES + j, value.shape[0], stride=TILE_SUBLANES), :] = (
            value[:, j * LANES:(j + 1) * LANES])


def _token_tile(ref, row):
    return ref.at[pl.ds(pl.multiple_of(row * TILE_SUBLANES, TILE_SUBLANES), TILE_SUBLANES), :]


def _mixer_kernel(sinks_ref, x_ref, ln0g_ref, ln0b_ref, w_in_ref, b_in_ref, w_pool_ref, pscale_ref,
                  w_brp_ref, w_bra_ref, w_out_ref, ln1g_ref, ln1b_ref, o_ref, ubuf, kvbuf,
                  *, layer, pre_ln, tq):
    s = pl.program_id(1)

    @pl.when(s == 0)
    def _():
        ubuf[0:POOL_HALO, :] = jnp.zeros((POOL_HALO, POOL_WIDTH), F32)
        kvbuf[0:ATT_BLOCK, :] = jnp.zeros((ATT_BLOCK, 8 * LANES), BF16)

    x = x_ref[...]
    if pre_ln:
        x = _layer_norm(x, ln0g_ref[...], ln0b_ref[...])
    xb = x.astype(BF16)

    u = _dot(xb, w_in_ref[:, 0:POOL_WIDTH]) + b_in_ref[:, 0:POOL_WIDTH]
    ubuf[POOL_HALO:POOL_HALO + tq, :] = u
    pos = (s * tq + lax.broadcasted_iota(jnp.int32, (tq, 1), 0)).astype(F32)
    mixed_parts = []
    for g, w in enumerate(POOL_WINDOWS):
        sl = slice(g * POOL_GROUP_CH, (g + 1) * POOL_GROUP_CH)
        cur = ubuf[POOL_HALO:POOL_HALO + tq, sl]
        acc = cur
        for j in range(1, w):
            acc = acc + ubuf[POOL_HALO - j:POOL_HALO - j + tq, sl]
        inv_cnt = 1.0 / jnp.minimum(pos + 1.0, float(w))
        d = (acc * inv_cnt - cur).astype(BF16)
        mixed_parts.append(_dot(d, w_pool_ref[g]) * pscale_ref[:, sl])
    mixed = jnp.concatenate(mixed_parts, axis=1).astype(BF16)
    y_pool = _dot(mixed, w_brp_ref[...])
    ubuf[0:POOL_HALO, :] = ubuf[tq:tq + POOL_HALO, :]

    qkv = _dot(xb, w_in_ref[:, QKV_START:GATE_START]) + b_in_ref[:, QKV_START:GATE_START]
    q = qkv[:, 0:Q_WIDTH].astype(BF16)
    k = qkv[:, Q_WIDTH:Q_WIDTH + KV_WIDTH]
    v = qkv[:, Q_WIDTH + KV_WIDTH:Q_WIDTH + 2 * KV_WIDTH]
    lo = lax.broadcasted_iota(jnp.int32, (tq, LANES), 1) < HEAD_DIM
    k_sw = pltpu.roll(k, HEAD_DIM, axis=1)
    v_sw = pltpu.roll(v, HEAD_DIM, axis=1)
    zero = jnp.zeros((tq, LANES), F32)
    slabs = (
        jnp.where(lo, k, zero), jnp.where(lo, zero, k_sw),
        jnp.where(lo, k_sw, zero), jnp.where(lo, zero, k),
        jnp.where(lo, v, zero), jnp.where(lo, zero, v_sw),
        jnp.where(lo, v_sw, zero), jnp.where(lo, zero, v),
    )
    for i, slab in enumerate(slabs):
        kvbuf[ATT_BLOCK:ATT_BLOCK + tq, i * LANES:(i + 1) * LANES] = slab.astype(BF16)

    qi = lax.broadcasted_iota(jnp.int32, (ATT_BLOCK, 2 * ATT_BLOCK), 0)
    kj = lax.broadcasted_iota(jnp.int32, (ATT_BLOCK, 2 * ATT_BLOCK), 1)
    dist = ATT_BLOCK + qi - kj
    band_ok = (dist >= 0) & (dist < WINDOW)
    distf = dist.astype(F32)
    o_blocks = []
    for qb in range(tq // ATT_BLOCK):
        r0 = qb * ATT_BLOCK
        first_key_pos = s * tq + r0 - ATT_BLOCK
        valid = band_ok & (kj + first_key_pos >= 0)
        o_pairs = []
        for hk in range(N_KV_HEADS):
            k_slabs = [kvbuf[r0:r0 + 2 * ATT_BLOCK, (2 * hk + i) * LANES:(2 * hk + i + 1) * LANES] for i in range(2)]
            v_slabs = [kvbuf[r0:r0 + 2 * ATT_BLOCK, (4 + 2 * hk + i) * LANES:(4 + 2 * hk + i + 1) * LANES] for i in range(2)]
            for pj in range(2):
                pair = hk * 2 + pj
                qp = q[r0:r0 + ATT_BLOCK, pair * LANES:(pair + 1) * LANES]
                o_pair = None
                for half in range(2):
                    h = pair * 2 + half
                    sc = _dot_nt(qp, k_slabs[half]) * (HEAD_DIM ** -0.5) - ALIBI_SLOPES[h] * distf
                    sc = jnp.where(valid, sc, -jnp.inf)
                    sink = sinks_ref[layer, h]
                    m = jnp.maximum(jnp.max(sc, axis=1, keepdims=True), sink)
                    p = jnp.exp(sc - m)
                    den = jnp.sum(p, axis=1, keepdims=True) + jnp.exp(sink - m)
                    pn = (p * (1.0 / den)).astype(BF16)
                    contrib = _dot(pn, v_slabs[half])
                    o_pair = contrib if o_pair is None else o_pair + contrib
                o_pairs.append(o_pair)
        o_blocks.append(jnp.concatenate(o_pairs, axis=1))
    o = jnp.concatenate(o_blocks, axis=0).astype(BF16)
    y_attn = _dot(o, w_bra_ref[...])
    kvbuf[0:ATT_BLOCK, :] = kvbuf[tq:tq + ATT_BLOCK, :]

    gates = _dot(xb, w_in_ref[:, GATE_START:D_IN]) + b_in_ref[:, GATE_START:D_IN]
    merged = _sigmoid(gates[:, 0:D_MODEL]) * y_pool + _sigmoid(gates[:, D_MODEL:2 * D_MODEL]) * y_attn
    mix = _dot(merged.astype(BF16), w_out_ref[...])
    _store_token_tiles(o_ref, _layer_norm(ALPHA * x + mix, ln1g_ref[...], ln1b_ref[...]))


def _mixer(x, sinks, ln0_g, ln0_b, w_in, b_in, w_pool, pool_scale, w_br_pool, w_br_attn, w_out, ln1_g, ln1_b,
           *, layer, pre_ln):
    B, S, D = x.shape
    tq = MIXER_TILE
    per_layer3 = lambda *shape: pl.BlockSpec((None,) + shape, lambda b, s: (layer,) + (0,) * len(shape))
    return pl.pallas_call(
        functools.partial(_mixer_kernel, layer=layer, pre_ln=pre_ln, tq=tq),
        grid=(B, S // tq),
        in_specs=[
            pl.BlockSpec(memory_space=pltpu.SMEM),
            pl.BlockSpec((None, tq, D), lambda b, s: (b, s, 0)),
            pl.BlockSpec((1, D), lambda b, s: (0, 0)),
            pl.BlockSpec((1, D), lambda b, s: (0, 0)),
            per_layer3(D, D_IN),
            per_layer3(1, D_IN),
            per_layer3(POOL_GROUPS, POOL_GROUP_CH, POOL_GROUP_CH),
            per_layer3(1, POOL_WIDTH),
            per_layer3(POOL_WIDTH, D),
            per_layer3(Q_WIDTH, D),
            per_layer3(D, D),
            per_layer3(1, D),
            per_layer3(1, D),
        ],
        out_specs=pl.BlockSpec((tq * TILE_SUBLANES, LANES), lambda b, s: (b * (S // tq) + s, 0)),
        out_shape=jax.ShapeDtypeStruct((B * S * TILE_SUBLANES, LANES), F32),
        scratch_shapes=[
            pltpu.VMEM((POOL_HALO + tq, POOL_WIDTH), F32),
            pltpu.VMEM((ATT_BLOCK + tq, 8 * LANES), BF16),
        ],
        compiler_params=pltpu.CompilerParams(
            dimension_semantics=("arbitrary", "arbitrary"), vmem_limit_bytes=VMEM_LIMIT),
        name=f"mixer_l{layer}",
    )(sinks, x, ln0_g, ln0_b, w_in, b_in, w_pool, pool_scale, w_br_pool, w_br_attn, w_out, ln1_g, ln1_b)


def _first_index_of_max(vals, iota, n):
    m = jnp.max(vals, axis=0, keepdims=True)
    idx = jnp.min(jnp.where(vals == m, iota, n), axis=0, keepdims=True)
    return m, idx


def _router_kernel(x_ref, wr_ref, bias_ref, ek_ref, rk_ref, gk_ref, cnt_ref, carry, *, tr):
    @pl.when(pl.program_id(0) == 0)
    def _():
        carry[...] = jnp.zeros_like(carry)

    logits = _dot_nt(wr_ref[...], _load_token_tiles(x_ref, tr).astype(BF16))
    scores = _sigmoid(logits)
    biased = scores + bias_ref[...]
    neg_inf = -jnp.inf

    io8 = lax.broadcasted_iota(jnp.int32, (EXPERTS_PER_GROUP, tr), 0)
    group_rows = []
    for g in range(N_EXPERT_GROUPS):
        blk = biased[g * EXPERTS_PER_GROUP:(g + 1) * EXPERTS_PER_GROUP]
        m1, i1 = _first_index_of_max(blk, io8, EXPERTS_PER_GROUP)
        m2 = jnp.max(jnp.where(io8 == i1, neg_inf, blk), axis=0, keepdims=True)
        group_rows.append(m1 + m2)
    gscore = jnp.concatenate(group_rows, axis=0)
    iog = lax.broadcasted_iota(jnp.int32, (N_EXPERT_GROUPS, tr), 0)
    keep = jnp.zeros((N_EXPERT_GROUPS, tr), F32)
    for _ in range(TOPK_GROUPS):
        _, gi = _first_index_of_max(gscore, iog, N_EXPERT_GROUPS)
        hit = iog == gi
        keep = jnp.where(hit, 1.0, keep)
        gscore = jnp.where(hit, neg_inf, gscore)
    masked = jnp.concatenate(
        [jnp.where(keep[g:g + 1] > 0.0, biased[g * EXPERTS_PER_GROUP:(g + 1) * EXPERTS_PER_GROUP], neg_inf)
         for g in range(N_EXPERT_GROUPS)], axis=0)

    ioe = lax.broadcasted_iota(jnp.int32, (N_EXPERTS, tr), 0)
    sel = jnp.zeros((N_EXPERTS, tr), F32)
    e_rows, s_rows, hits = [], [], []
    for _ in range(TOP_K):
        _, ei = _first_index_of_max(masked, ioe, N_EXPERTS)
        hit = ioe == ei
        sel = jnp.where(hit, 1.0, sel)
        masked = jnp.where(hit, neg_inf, masked)
        e_rows.append(ei)
        hits.append(hit)
        s_rows.append(jnp.sum(jnp.where(hit, scores, 0.0), axis=0, keepdims=True))
    sel_scores = jnp.concatenate(s_rows, axis=0)
    gk_ref[...] = sel_scores / jnp.sum(sel_scores, axis=0, keepdims=True) * ROUTED_SCALE
    ek_ref[...] = jnp.concatenate(e_rows, axis=0)

    before = (lax.broadcasted_iota(jnp.int32, (tr, tr), 0) < lax.broadcasted_iota(jnp.int32, (tr, tr), 1))
    prefix = _dot(sel.astype(BF16), jnp.where(before, 1.0, 0.0).astype(BF16))
    rank_full = prefix + carry[...]
    rk_ref[...] = jnp.concatenate(
        [jnp.sum(jnp.where(hit, rank_full, 0.0), axis=0, keepdims=True) for hit in hits], axis=0).astype(jnp.int32)
    total = carry[...] + jnp.sum(sel, axis=1, keepdims=True)
    carry[...] = total
    cnt_ref[...] = jnp.broadcast_to(total, (N_EXPERTS, LANES))


def _router(x1, w_router_t, bias_col, *, layer):
    T = x1.shape[0] // TILE_SUBLANES
    tr = ROUTER_TILE
    row_spec = pl.BlockSpec((TOP_K, tr), lambda i: (0, i))
    return pl.pallas_call(
        functools.partial(_router_kernel, tr=tr),
        grid=(T // tr,),
        in_specs=[
            pl.BlockSpec((tr * TILE_SUBLANES, LANES), lambda i: (i, 0)),
            pl.BlockSpec((None, N_EXPERTS, D_MODEL), lambda i: (layer, 0, 0)),
            pl.BlockSpec((None, N_EXPERTS, 1), lambda i: (layer, 0, 0)),
        ],
        out_specs=[row_spec, row_spec, row_spec, pl.BlockSpec((N_EXPERTS, LANES), lambda i: (0, 0))],
        out_shape=[
            jax.ShapeDtypeStruct((TOP_K, T), jnp.int32),
            jax.ShapeDtypeStruct((TOP_K, T), jnp.int32),
            jax.ShapeDtypeStruct((TOP_K, T), F32),
            jax.ShapeDtypeStruct((N_EXPERTS, LANES), F32),
        ],
        scratch_shapes=[pltpu.VMEM((N_EXPERTS, 1), F32)],
        compiler_params=pltpu.CompilerParams(dimension_semantics=("arbitrary",), vmem_limit_bytes=VMEM_LIMIT),
        name=f"router_l{layer}",
    )(x1, w_router_t, bias_col)


def _for_each_assignment(fn):
    for k in range(TOP_K):
        for j in range(TOKEN_TILE):
            fn(k, j)


def _dispatch_kernel(fill_ref, nu_ref, pos_hbm, x_ref, xs_hbm, idx_smem, zbuf, sem_idx, sem_out, sem_fill,
                     *, n_tiles, n_blocks):
    i = pl.program_id(0)
    slot = lax.rem(i, 2)

    def idx_copy(tile, sl):
        return pltpu.make_async_copy(pos_hbm.at[tile], idx_smem.at[sl], sem_idx.at[sl])

    def fill_copy(row0, rows):
        return pltpu.make_async_copy(zbuf.at[pl.ds(0, rows)], xs_hbm.at[pl.ds(row0, rows)], sem_fill)

    def pad_fill(e, wait):
        row0 = fill_ref[e]
        n_pad = (0 - row0) & (EXPERT_BLOCK - 1)
        piece = EXPERT_BLOCK // 2
        while piece >= 1:
            has = (n_pad & piece) != 0

            @pl.when(has)
            def _(row0=row0, piece=piece):
                fill_copy(0 if wait else row0, piece).wait() if wait else fill_copy(row0, piece).start()
            row0 = row0 + jnp.where(has, piece, 0)
            piece //= 2

    @pl.when(i == 0)
    def _():
        idx_copy(0, 0).start()
        zbuf[...] = jnp.zeros_like(zbuf)
        lax.fori_loop(0, N_EXPERTS, lambda e, c: (pad_fill(e, False), c)[1], 0)
        lax.fori_loop(nu_ref[0], n_blocks, lambda b, c: (fill_copy(b * EXPERT_BLOCK, EXPERT_BLOCK).start(), c)[1], 0)
        lax.fori_loop(0, N_EXPERTS, lambda e, c: (pad_fill(e, True), c)[1], 0)
        lax.fori_loop(nu_ref[0], n_blocks, lambda b, c: (fill_copy(0, EXPERT_BLOCK).wait(), c)[1], 0)

    idx_copy(i, slot).wait()

    @pl.when(i + 1 < n_tiles)
    def _():
        idx_copy(i + 1, 1 - slot).start()

    def send(k, j):
        pltpu.make_async_copy(_token_tile(x_ref, j), xs_hbm.at[idx_smem[slot, k, j]], sem_out).start(priority=j % 2)
    _for_each_assignment(send)

    for _ in range(TOP_K):
        pltpu.make_async_copy(x_ref, x_ref, sem_out).wait()


def _dispatch(fill_start, n_used, pos_tiles, x1, n_sorted_rows, *, layer):
    T = x1.shape[0] // TILE_SUBLANES
    n_tiles = T // TOKEN_TILE
    return pl.pallas_call(
        functools.partial(_dispatch_kernel, n_tiles=n_tiles, n_blocks=n_sorted_rows // EXPERT_BLOCK),
        grid_spec=pltpu.PrefetchScalarGridSpec(
            num_scalar_prefetch=2,
            grid=(n_tiles,),
            in_specs=[
                pl.BlockSpec(memory_space=pl.ANY),
                pl.BlockSpec((TOKEN_TILE * TILE_SUBLANES, LANES), lambda i, fill, nu: (i, 0)),
            ],
            out_specs=pl.BlockSpec(memory_space=pl.ANY),
            scratch_shapes=[
                pltpu.SMEM((2, TOP_K, TOKEN_TILE), jnp.int32),
                pltpu.VMEM((EXPERT_BLOCK, TILE_SUBLANES, LANES), F32),
                pltpu.SemaphoreType.DMA((2,)),
                pltpu.SemaphoreType.DMA(()),
                pltpu.SemaphoreType.DMA(()),
            ],
        ),
        out_shape=jax.ShapeDtypeStruct((n_sorted_rows, TILE_SUBLANES, LANES), F32),
        compiler_params=pltpu.CompilerParams(dimension_semantics=("arbitrary",), vmem_limit_bytes=VMEM_LIMIT),
        name=f"dispatch_l{layer}",
    )(fill_start, n_used, pos_tiles, x1)


def _expert_kernel(be_ref, nu_ref, xs_ref, wg_ref, wu_ref, wd_ref, y_ref, wgu_bf, wd_bf):
    i = pl.program_id(0)

    @pl.when(i < nu_ref[0])
    def _():
        new_expert = (i == 0) | (be_ref[i] != be_ref[jnp.maximum(i - 1, 0)])

        @pl.when(new_expert)
        def _():
            wgu_bf[:, 0:EXPERT_FF] = wg_ref[...].astype(BF16)
            wgu_bf[:, EXPERT_FF:2 * EXPERT_FF] = wu_ref[...].astype(BF16)
            wd_bf[...] = wd_ref[...].astype(BF16)

        for r0 in range(0, EXPERT_BLOCK, EXPERT_CHUNK):
            h = _dot(_load_token_tiles(xs_ref, EXPERT_CHUNK, r0).astype(BF16), wgu_bf[...])
            g = h[:, 0:EXPERT_FF]
            a = (g * _sigmoid(g) * h[:, EXPERT_FF:2 * EXPERT_FF]).astype(BF16)
            _store_token_tiles(y_ref, _dot(a, wd_bf[...]), r0)

    @pl.when(i >= nu_ref[0])
    def _():
        y_ref[...] = jnp.zeros_like(y_ref)


def _experts(blk_expert, n_used, xs, w_gate, w_up, w_down, *, layer):
    nb = blk_expert.shape[0]
    blk = EXPERT_BLOCK
    used = lambda i, nu: jnp.minimum(i, nu[0] - 1)
    w_spec = lambda *shape: pl.BlockSpec(
        (None,) + shape, lambda i, be, nu: (layer * N_EXPERTS + be[used(i, nu)], 0, 0))
    in_row_spec = pl.BlockSpec((blk * TILE_SUBLANES, LANES), lambda i, be, nu: (used(i, nu), 0))
    return pl.pallas_call(
        _expert_kernel,
        grid_spec=pltpu.PrefetchScalarGridSpec(
            num_scalar_prefetch=2,
            grid=(nb,),
            in_specs=[in_row_spec, w_spec(D_MODEL, EXPERT_FF), w_spec(D_MODEL, EXPERT_FF), w_spec(EXPERT_FF, D_MODEL)],
            out_specs=pl.BlockSpec((blk * TILE_SUBLANES, LANES), lambda i, be, nu: (i, 0)),
            scratch_shapes=[
                pltpu.VMEM((D_MODEL, 2 * EXPERT_FF), BF16),
                pltpu.VMEM((EXPERT_FF, D_MODEL), BF16),
            ],
        ),
        out_shape=jax.ShapeDtypeStruct((nb * blk * TILE_SUBLANES, LANES), F32),
        compiler_params=pltpu.CompilerParams(dimension_semantics=("arbitrary",), vmem_limit_bytes=VMEM_LIMIT),
        name=f"experts_l{layer}",
    )(blk_expert, n_used, xs, w_gate, w_up, w_down)


def _combine_kernel(pos_hbm, y_hbm, x_ref, gate_ref, wsg_ref, wsu_ref, wsd_ref, ln2g_ref, ln2b_ref, o_ref,
                    idx_smem, ybuf, sem_idx, sem_y, *, n_tiles):
    i = pl.program_id(0)
    slot = lax.rem(i, 2)
    nslot = 1 - slot
    rows_per_slot = TOP_K * TOKEN_TILE

    def idx_copy(tile, sl):
        return pltpu.make_async_copy(pos_hbm.at[tile], idx_smem.at[sl], sem_idx.at[sl])

    def start_gather(sl):
        def fetch(k, j):
            pltpu.make_async_copy(y_hbm.at[idx_smem[sl, k, j]],
                                  _token_tile(ybuf, sl * rows_per_slot + k * TOKEN_TILE + j), sem_y.at[sl]
                                  ).start(priority=j % 2)
        _for_each_assignment(fetch)

    @pl.when(i == 0)
    def _():
        idx_copy(0, 0).start()
        idx_copy(0, 0).wait()
        start_gather(0)
        if n_tiles > 1:
            idx_copy(1, 1).start()

    @pl.when(i + 1 < n_tiles)
    def _():
        idx_copy(i + 1, nslot).wait()
        start_gather(nslot)

    @pl.when(i + 2 < n_tiles)
    def _():
        idx_copy(i + 2, slot).start()

    x = _load_token_tiles(x_ref, TOKEN_TILE)
    xb = x.astype(BF16)
    g = _dot(xb, wsg_ref[...])
    a = (g * _sigmoid(g) * _dot(xb, wsu_ref[...])).astype(BF16)
    shared = _dot(a, wsd_ref[...])

    base = slot * rows_per_slot
    slot_rows = ybuf.at[pl.ds(pl.multiple_of(base * TILE_SUBLANES, TILE_SUBLANES), rows_per_slot * TILE_SUBLANES), :]
    pltpu.make_async_copy(slot_rows, slot_rows, sem_y.at[slot]).wait()
    gate = gate_ref[...]
    routed = None
    for k in range(TOP_K):
        term = gate[:, k:k + 1] * _load_token_tiles(ybuf, TOKEN_TILE, base + k * TOKEN_TILE)
        routed = term if routed is None else routed + term
    o_ref[...] = _layer_norm(ALPHA * x + (routed + shared), ln2g_ref[...], ln2b_ref[...])


def _combine(pos_tiles, y_sorted, x1, gate_t, w_sg, w_su, w_sd, ln2_g, ln2_b, *, layer):
    T = x1.shape[0] // TILE_SUBLANES
    tc = TOKEN_TILE
    n_tiles = T // tc
    per_layer = lambda *shape: pl.BlockSpec((None,) + shape, lambda i: (layer,) + (0,) * len(shape))
    return pl.pallas_call(
        functools.partial(_combine_kernel, n_tiles=n_tiles),
        grid=(n_tiles,),
        in_specs=[
            pl.BlockSpec(memory_space=pl.ANY),
            pl.BlockSpec(memory_space=pl.ANY),
            pl.BlockSpec((tc * TILE_SUBLANES, LANES), lambda i: (i, 0)),
            pl.BlockSpec((tc, TOP_K), lambda i: (i, 0)),
            per_layer(D_MODEL, EXPERT_FF),
            per_layer(D_MODEL, EXPERT_FF),
            per_layer(EXPERT_FF, D_MODEL),
            per_layer(1, D_MODEL),
            per_layer(1, D_MODEL),
        ],
        out_specs=pl.BlockSpec((tc, D_MODEL), lambda i: (i, 0)),
        out_shape=jax.ShapeDtypeStruct((T, D_MODEL), F32),
        scratch_shapes=[
            pltpu.SMEM((2, TOP_K, TOKEN_TILE), jnp.int32),
            pltpu.VMEM((2 * TOP_K * tc * TILE_SUBLANES, LANES), F32),
            pltpu.SemaphoreType.DMA((2,)),
            pltpu.SemaphoreType.DMA((2,)),
        ],
        compiler_params=pltpu.CompilerParams(dimension_semantics=("arbitrary",), vmem_limit_bytes=VMEM_LIMIT),
        name=f"combine_l{layer}",
    )(pos_tiles, y_sorted, x1, gate_t, w_sg, w_su, w_sd, ln2_g, ln2_b)


def _dispatch_plan(ek, rk, counts_f):
    T = ek.shape[1]
    blk = EXPERT_BLOCK
    nb = (T * TOP_K) // blk + N_EXPERTS
    experts = jnp.arange(N_EXPERTS, dtype=jnp.int32)
    counts = counts_f[:, 0].astype(jnp.int32)
    pcounts = (counts + blk - 1) // blk * blk
    pends = jnp.sum(jnp.where(experts[None, :] <= experts[:, None], pcounts[None, :], 0), axis=1)
    pstarts = pends - pcounts
    pos = jnp.sum(jnp.where(ek[None] == experts[:, None, None], pstarts[:, None, None], 0), axis=0) + rk
    n_used = (pends[-1] // blk).reshape(1)
    block_row0 = jnp.arange(nb, dtype=jnp.int32) * blk
    blk_expert = jnp.minimum(jnp.sum((pends[None, :] <= block_row0[:, None]).astype(jnp.int32), axis=1), N_EXPERTS - 1)
    fill_start = pstarts + counts
    pos_tiles = pos.reshape(TOP_K, T // TOKEN_TILE, TOKEN_TILE).transpose(1, 0, 2)
    return blk_expert, n_used, fill_start, pos_tiles, nb * blk


def kernel(x, ln0_g, ln0_b, w_in, b_in, w_pool, pool_scale, attn_sinks, w_br_pool, w_br_attn, w_out, ln1_g, ln1_b,
           w_router, router_bias, w_exp_gate, w_exp_up, w_exp_down, w_sh_gate, w_sh_up, w_sh_down, ln2_g, ln2_b):
    B, S, D = x.shape
    depth = w_in.shape[0]
    row = lambda a: a.reshape(a.shape[0], 1, a.shape[1])
    w_in_b, w_pool_b = w_in.astype(BF16), w_pool.astype(BF16)
    w_brp_b, w_bra_b, w_out_b = w_br_pool.astype(BF16), w_br_attn.astype(BF16), w_out.astype(BF16)
    w_router_t = jnp.swapaxes(w_router, 1, 2).astype(BF16)
    bias_col = router_bias.reshape(depth, N_EXPERTS, 1)
    w_sg_b, w_su_b, w_sd_b = w_sh_gate.astype(BF16), w_sh_up.astype(BF16), w_sh_down.astype(BF16)
    w_eg = w_exp_gate.reshape(depth * N_EXPERTS, D, EXPERT_FF)
    w_eu = w_exp_up.reshape(depth * N_EXPERTS, D, EXPERT_FF)
    w_ed = w_exp_down.reshape(depth * N_EXPERTS, EXPERT_FF, D)
    ln0_g2, ln0_b2 = ln0_g.reshape(1, D), ln0_b.reshape(1, D)

    for l in range(depth):
        x1 = _mixer(x, attn_sinks, ln0_g2, ln0_b2, w_in_b, row(b_in), w_pool_b, row(pool_scale), w_brp_b, w_bra_b,
                    w_out_b, row(ln1_g), row(ln1_b), layer=l, pre_ln=(l == 0))
        ek, rk, gk, counts_f = _router(x1, w_router_t, bias_col, layer=l)
        blk_expert, n_used, fill_start, pos_tiles, n_sorted = _dispatch_plan(ek, rk, counts_f)
        xs = _dispatch(fill_start, n_used, pos_tiles, x1, n_sorted + EXPERT_BLOCK, layer=l)
        y_sorted = _experts(blk_expert, n_used, xs.reshape(-1, LANES), w_eg, w_eu, w_ed, layer=l)
        x = _combine(pos_tiles, y_sorted.reshape(-1, TILE_SUBLANES, LANES), x1, gk.T, w_sg_b, w_su_b, w_sd_b,
                     row(ln2_g), row(ln2_b), layer=l).reshape(B, S, D)
    return x
```

```python
import functools

import jax
import jax.numpy as jnp
from jax import lax
from jax.experimental import pallas as pl
from jax.experimental.pallas import tpu as pltpu

D_MODEL = 1024
DEPTH = 4
POOL_GROUPS = 4
POOL_GROUP_CH = 128
POOL_WIDTH = POOL_GROUPS * POOL_GROUP_CH
POOL_WINDOWS = (2, 4, 8, 16)
POOL_HALO = 16
N_Q_HEADS = 8
N_KV_HEADS = 2
HEAD_DIM = 64
Q_WIDTH = N_Q_HEADS * HEAD_DIM
KV_WIDTH = N_KV_HEADS * HEAD_DIM
WINDOW = 128
ATT_BLOCK = 128
D_IN = POOL_WIDTH + Q_WIDTH + 2 * KV_WIDTH + 2 * D_MODEL
QKV_START = POOL_WIDTH
GATE_START = POOL_WIDTH + Q_WIDTH + 2 * KV_WIDTH
N_EXPERTS = 64
EXPERT_FF = 256
TOP_K = 8
N_EXPERT_GROUPS = 8
EXPERTS_PER_GROUP = N_EXPERTS // N_EXPERT_GROUPS
TOPK_GROUPS = 4
ROUTED_SCALE = 2.5
ALPHA = (2.0 * DEPTH) ** 0.25
LN_EPS = 1e-5
SEGMENT_HEADS = (0, 2, 1, 3, 4, 6, 5, 7)
ALIBI_SLOPES = tuple(float(2.0 ** (-8.0 * h / N_Q_HEADS)) for h in range(1, N_Q_HEADS + 1))

LANES = 128
SUBLANES = 8
TILE_SUBLANES = D_MODEL // LANES
MIXER_TILE = 256
ROUTER_TILE = 1024
EXPERT_BLOCK = 512
EXPERT_CHUNK = 512
TOKEN_TILE = 128
VMEM_LIMIT = 48 * 1024 * 1024

BF16 = jnp.bfloat16
F32 = jnp.float32

assert TILE_SUBLANES == SUBLANES and TOP_K == SUBLANES and TOKEN_TILE == LANES


def _dot(a, b):
    return jnp.dot(a, b, preferred_element_type=F32)


def _dot_nt(a, b):
    return lax.dot_general(a, b, (((1,), (1,)), ((), ())), preferred_element_type=F32)


def _layer_norm(x, g, b):
    mu = jnp.mean(x, axis=-1, keepdims=True)
    xc = x - mu
    var = jnp.mean(xc * xc, axis=-1, keepdims=True)
    return xc * lax.rsqrt(var + LN_EPS) * g + b


def _sigmoid(x):
    return 1.0 / (1.0 + jnp.exp(-x))


def _load_token_tiles(ref, rows, row0=0):
    return jnp.concatenate(
        [ref[pl.ds(row0 * TILE_SUBLANES + j, rows, stride=TILE_SUBLANES), :] for j in range(TILE_SUBLANES)], axis=1)


def _store_token_tiles(ref, value, row0=0):
    for j in range(TILE_SUBLANES):
        ref[pl.ds(row0 * TILE_SUBLANES + j, value.shape[0], stride=TILE_SUBLANES), :] = (
            value[:, j * LANES:(j + 1) * LANES])


def _token_tile(ref, row):
    return ref.at[pl.ds(pl.multiple_of(row * TILE_SUBLANES, TILE_SUBLANES), TILE_SUBLANES), :]


def _mixer_kernel(sinks_ref, x_ref, ln0g_ref, ln0b_ref, w_in_ref, b_in_ref, w_pool_ref, pscale_ref,
                  w_brp_ref, w_bra_ref, w_out_ref, ln1g_ref, ln1b_ref, o_ref, ubuf, kvbuf, bias_tab,
                  *, layer, pre_ln, tq):
    s = pl.program_id(1)

    @pl.when(s == 0)
    def _():
        ubuf[0:POOL_HALO, :] = jnp.zeros((POOL_HALO, POOL_WIDTH), F32)
        kvbuf[0:ATT_BLOCK, :] = jnp.zeros((ATT_BLOCK, 8 * LANES), BF16)

    x = x_ref[...]
    if pre_ln:
        x = _layer_norm(x, ln0g_ref[...], ln0b_ref[...])
    xb = x.astype(BF16)

    u = _dot(xb, w_in_ref[:, 0:POOL_WIDTH]) + b_in_ref[:, 0:POOL_WIDTH]
    ubuf[POOL_HALO:POOL_HALO + tq, :] = u
    pos = (s * tq + lax.broadcasted_iota(jnp.int32, (tq, 1), 0)).astype(F32)
    mixed_parts = []
    for g, w in enumerate(POOL_WINDOWS):
        sl = slice(g * POOL_GROUP_CH, (g + 1) * POOL_GROUP_CH)
        cur = ubuf[POOL_HALO:POOL_HALO + tq, sl]
        acc = cur
        for j in range(1, w):
            acc = acc + ubuf[POOL_HALO - j:POOL_HALO - j + tq, sl]
        inv_cnt = 1.0 / jnp.minimum(pos + 1.0, float(w))
        d = (acc * inv_cnt - cur).astype(BF16)
        mixed_parts.append(_dot(d, w_pool_ref[g]) * pscale_ref[:, sl])
    mixed = jnp.concatenate(mixed_parts, axis=1).astype(BF16)
    y_pool = _dot(mixed, w_brp_ref[...])
    ubuf[0:POOL_HALO, :] = ubuf[tq:tq + POOL_HALO, :]

    qkv = _dot(xb, w_in_ref[:, QKV_START:GATE_START]) + b_in_ref[:, QKV_START:GATE_START]
    q = (qkv[:, 0:Q_WIDTH] * (HEAD_DIM ** -0.5)).astype(BF16)
    k = qkv[:, Q_WIDTH:Q_WIDTH + KV_WIDTH]
    v = qkv[:, Q_WIDTH + KV_WIDTH:Q_WIDTH + 2 * KV_WIDTH]
    lo = lax.broadcasted_iota(jnp.int32, (tq, LANES), 1) < HEAD_DIM
    k_sw = pltpu.roll(k, HEAD_DIM, axis=1)
    v_sw = pltpu.roll(v, HEAD_DIM, axis=1)
    zero = jnp.zeros((tq, LANES), F32)
    slabs = (
        jnp.where(lo, k, zero), jnp.where(lo, zero, k_sw),
        jnp.where(lo, k_sw, zero), jnp.where(lo, zero, k),
        jnp.where(lo, v, zero), jnp.where(lo, zero, v_sw),
        jnp.where(lo, v_sw, zero), jnp.where(lo, zero, v),
    )
    for i, slab in enumerate(slabs):
        kvbuf[ATT_BLOCK:ATT_BLOCK + tq, i * LANES:(i + 1) * LANES] = slab.astype(BF16)

    @pl.when((pl.program_id(0) == 0) & (s == 0))
    def _():
        qi = lax.broadcasted_iota(jnp.int32, (ATT_BLOCK, 2 * ATT_BLOCK), 0)
        kj = lax.broadcasted_iota(jnp.int32, (ATT_BLOCK, 2 * ATT_BLOCK), 1)
        dist = ATT_BLOCK + qi - kj
        band_ok = (dist >= 0) & (dist < WINDOW)
        distf = dist.astype(F32)
        for i, h in enumerate(SEGMENT_HEADS):
            bias_tab[i * ATT_BLOCK:(i + 1) * ATT_BLOCK, :] = jnp.where(band_ok, -ALIBI_SLOPES[h] * distf, -jnp.inf)

    sink_col = jnp.concatenate([jnp.full((ATT_BLOCK, 1), sinks_ref[layer, h], F32) for h in SEGMENT_HEADS], axis=0)
    key_col = lax.broadcasted_iota(jnp.int32, (1, 2 * ATT_BLOCK), 1)
    o_blocks = []
    for qb in range(tq // ATT_BLOCK):
        r0 = qb * ATT_BLOCK
        first_key_pos = s * tq + r0 - ATT_BLOCK
        score_parts = []
        for hk in range(N_KV_HEADS):
            q_pairs = jnp.concatenate(
                [q[r0:r0 + ATT_BLOCK, (2 * hk + pj) * LANES:(2 * hk + pj + 1) * LANES] for pj in range(2)], axis=0)
            for half in range(2):
                k_slab = kvbuf[r0:r0 + 2 * ATT_BLOCK, (2 * hk + half) * LANES:(2 * hk + half + 1) * LANES]
                score_parts.append(_dot_nt(q_pairs, k_slab))
        sc = jnp.concatenate(score_parts, axis=0) + bias_tab[...]
        sc = jnp.where(key_col + first_key_pos < 0, -jnp.inf, sc)
        m = jnp.maximum(jnp.max(sc, axis=1, keepdims=True), sink_col)
        p = jnp.exp(sc - m).astype(BF16)
        den = _dot(p, jnp.ones((2 * ATT_BLOCK, LANES), BF16)) + jnp.exp(sink_col - m)
        inv_den = 1.0 / den
        o_pairs = []
        for hk in range(N_KV_HEADS):
            pv = None
            for half in range(2):
                rows = (2 * hk + half) * 2 * ATT_BLOCK
                v_slab = kvbuf[r0:r0 + 2 * ATT_BLOCK, (4 + 2 * hk + half) * LANES:(4 + 2 * hk + half + 1) * LANES]
                contrib = _dot(p[rows:rows + 2 * ATT_BLOCK], v_slab) * inv_den[rows:rows + 2 * ATT_BLOCK]
                pv = contrib if pv is None else pv + contrib
            o_pairs += [pv[0:ATT_BLOCK], pv[ATT_BLOCK:2 * ATT_BLOCK]]
        o_blocks.append(jnp.concatenate(o_pairs, axis=1))
    o = jnp.concatenate(o_blocks, axis=0).astype(BF16)
    y_attn = _dot(o, w_bra_ref[...])
    kvbuf[0:ATT_BLOCK, :] = kvbuf[tq:tq + ATT_BLOCK, :]

    gates = _dot(xb, w_in_ref[:, GATE_START:D_IN]) + b_in_ref[:, GATE_START:D_IN]
    merged = _sigmoid(gates[:, 0:D_MODEL]) * y_pool + _sigmoid(gates[:, D_MODEL:2 * D_MODEL]) * y_attn
    mix = _dot(merged.astype(BF16), w_out_ref[...])
    _store_token_tiles(o_ref, _layer_norm(ALPHA * x + mix, ln1g_ref[...], ln1b_ref[...]))


def _mixer(x, sinks, ln0_g, ln0_b, w_in, b_in, w_pool, pool_scale, w_br_pool, w_br_attn, w_out, ln1_g, ln1_b,
           *, layer, pre_ln):
    B, S, D = x.shape
    tq = MIXER_TILE
    per_layer3 = lambda *shape: pl.BlockSpec((None,) + shape, lambda b, s: (layer,) + (0,) * len(shape))
    return pl.pallas_call(
        functools.partial(_mixer_kernel, layer=layer, pre_ln=pre_ln, tq=tq),
        grid=(B, S // tq),
        in_specs=[
            pl.BlockSpec(memory_space=pltpu.SMEM),
            pl.BlockSpec((None, tq, D), lambda b, s: (b, s, 0)),
            pl.BlockSpec((1, D), lambda b, s: (0, 0)),
            pl.BlockSpec((1, D), lambda b, s: (0, 0)),
            per_layer3(D, D_IN),
            per_layer3(1, D_IN),
            per_layer3(POOL_GROUPS, POOL_GROUP_CH, POOL_GROUP_CH),
            per_layer3(1, POOL_WIDTH),
            per_layer3(POOL_WIDTH, D),
            per_layer3(Q_WIDTH, D),
            per_layer3(D, D),
            per_layer3(1, D),
            per_layer3(1, D),
        ],
        out_specs=pl.BlockSpec((tq * TILE_SUBLANES, LANES), lambda b, s: (b * (S // tq) + s, 0)),
        out_shape=jax.ShapeDtypeStruct((B * S * TILE_SUBLANES, LANES), F32),
        scratch_shapes=[
            pltpu.VMEM((POOL_HALO + tq, POOL_WIDTH), F32),
            pltpu.VMEM((ATT_BLOCK + tq, 8 * LANES), BF16),
            pltpu.VMEM((N_Q_HEADS * ATT_BLOCK, 2 * ATT_BLOCK), F32),
        ],
        compiler_params=pltpu.CompilerParams(
            dimension_semantics=("arbitrary", "arbitrary"), vmem_limit_bytes=VMEM_LIMIT),
        name=f"mixer_l{layer}",
    )(sinks, x, ln0_g, ln0_b, w_in, b_in, w_pool, pool_scale, w_br_pool, w_br_attn, w_out, ln1_g, ln1_b)


def _first_index_of_max(vals, iota, n):
    m = jnp.max(vals, axis=0, keepdims=True)
    idx = jnp.min(jnp.where(vals == m, iota, n), axis=0, keepdims=True)
    return m, idx


def _route(xb, wr_ref, bias_ref, ek_ref, rk_ref, gk_ref, cnt_ref, carry):
    tr = xb.shape[0]
    logits = _dot_nt(wr_ref[...], xb)
    scores = _sigmoid(logits)
    biased = scores + bias_ref[...]
    neg_inf = -jnp.inf

    io8 = lax.broadcasted_iota(jnp.int32, (EXPERTS_PER_GROUP, tr), 0)
    group_rows = []
    for g in range(N_EXPERT_GROUPS):
        blk = biased[g * EXPERTS_PER_GROUP:(g + 1) * EXPERTS_PER_GROUP]
        m1, i1 = _first_index_of_max(blk, io8, EXPERTS_PER_GROUP)
        m2 = jnp.max(jnp.where(io8 == i1, neg_inf, blk), axis=0, keepdims=True)
        group_rows.append(m1 + m2)
    gscore = jnp.concatenate(group_rows, axis=0)
    iog = lax.broadcasted_iota(jnp.int32, (N_EXPERT_GROUPS, tr), 0)
    keep = jnp.zeros((N_EXPERT_GROUPS, tr), F32)
    for _ in range(TOPK_GROUPS):
        _, gi = _first_index_of_max(gscore, iog, N_EXPERT_GROUPS)
        hit = iog == gi
        keep = jnp.where(hit, 1.0, keep)
        gscore = jnp.where(hit, neg_inf, gscore)
    masked = jnp.concatenate(
        [jnp.where(keep[g:g + 1] > 0.0, biased[g * EXPERTS_PER_GROUP:(g + 1) * EXPERTS_PER_GROUP], neg_inf)
         for g in range(N_EXPERT_GROUPS)], axis=0)

    ioe = lax.broadcasted_iota(jnp.int32, (N_EXPERTS, tr), 0)
    sel = jnp.zeros((N_EXPERTS, tr), F32)
    e_rows, s_rows, hits = [], [], []
    for _ in range(TOP_K):
        _, ei = _first_index_of_max(masked, ioe, N_EXPERTS)
        hit = ioe == ei
        sel = jnp.where(hit, 1.0, sel)
        masked = jnp.where(hit, neg_inf, masked)
        e_rows.append(ei)
        hits.append(hit)
        s_rows.append(jnp.sum(jnp.where(hit, scores, 0.0), axis=0, keepdims=True))
    sel_scores = jnp.concatenate(s_rows, axis=0)
    gk_ref[...] = sel_scores / jnp.sum(sel_scores, axis=0, keepdims=True) * ROUTED_SCALE
    ek_ref[...] = jnp.concatenate(e_rows, axis=0)

    before = (lax.broadcasted_iota(jnp.int32, (tr, tr), 0) < lax.broadcasted_iota(jnp.int32, (tr, tr), 1))
    prefix = _dot(sel.astype(BF16), jnp.where(before, 1.0, 0.0).astype(BF16))
    rank_full = prefix + carry[...]
    rk_ref[...] = jnp.concatenate(
        [jnp.sum(jnp.where(hit, rank_full, 0.0), axis=0, keepdims=True) for hit in hits], axis=0).astype(jnp.int32)
    total = carry[...] + jnp.sum(sel, axis=1, keepdims=True)
    carry[...] = total
    cnt_ref[...] = jnp.broadcast_to(total, (N_EXPERTS, LANES))


def _router_kernel(x_ref, wr_ref, bias_ref, ek_ref, rk_ref, gk_ref, cnt_ref, carry, *, tr):
    @pl.when(pl.program_id(0) == 0)
    def _():
        carry[...] = jnp.zeros_like(carry)

    _route(_load_token_tiles(x_ref, tr).astype(BF16), wr_ref, bias_ref, ek_ref, rk_ref, gk_ref, cnt_ref, carry)


def _router(x1, w_router_t, bias_col, *, layer):
    T = x1.shape[0] // TILE_SUBLANES
    tr = ROUTER_TILE
    row_spec = pl.BlockSpec((TOP_K, tr), lambda i: (0, i))
    return pl.pallas_call(
        functools.partial(_router_kernel, tr=tr),
        grid=(T // tr,),
        in_specs=[
            pl.BlockSpec((tr * TILE_SUBLANES, LANES), lambda i: (i, 0)),
            pl.BlockSpec((None, N_EXPERTS, D_MODEL), lambda i: (layer, 0, 0)),
            pl.BlockSpec((None, N_EXPERTS, 1), lambda i: (layer, 0, 0)),
        ],
        out_specs=[row_spec, row_spec, row_spec, pl.BlockSpec((N_EXPERTS, LANES), lambda i: (0, 0))],
        out_shape=[
            jax.ShapeDtypeStruct((TOP_K, T), jnp.int32),
            jax.ShapeDtypeStruct((TOP_K, T), jnp.int32),
            jax.ShapeDtypeStruct((TOP_K, T), F32),
            jax.ShapeDtypeStruct((N_EXPERTS, LANES), F32),
        ],
        scratch_shapes=[pltpu.VMEM((N_EXPERTS, 1), F32)],
        compiler_params=pltpu.CompilerParams(dimension_semantics=("arbitrary",), vmem_limit_bytes=VMEM_LIMIT),
        name=f"router_l{layer}",
    )(x1, w_router_t, bias_col)


def _for_each_assignment(fn):
    for k in range(TOP_K):
        for j in range(TOKEN_TILE):
            fn(k, j)


def _dispatch_kernel(fill_ref, nu_ref, pos_hbm, x_ref, xs_hbm, idx_smem, zbuf, sem_idx, sem_out, sem_fill,
                     *, n_tiles, n_blocks):
    i = pl.program_id(0)
    slot = lax.rem(i, 2)

    def idx_copy(tile, sl):
        return pltpu.make_async_copy(pos_hbm.at[tile], idx_smem.at[sl], sem_idx.at[sl])

    def fill_copy(row0, rows):
        return pltpu.make_async_copy(zbuf.at[pl.ds(0, rows)], xs_hbm.at[pl.ds(row0, rows)], sem_fill)

    def pad_fill(e, wait):
        row0 = fill_ref[e]
        n_pad = (0 - row0) & (EXPERT_BLOCK - 1)
        piece = EXPERT_BLOCK // 2
        while piece >= 1:
            has = (n_pad & piece) != 0

            @pl.when(has)
            def _(row0=row0, piece=piece):
                fill_copy(0 if wait else row0, piece).wait() if wait else fill_copy(row0, piece).start()
            row0 = row0 + jnp.where(has, piece, 0)
            piece //= 2

    @pl.when(i == 0)
    def _():
        idx_copy(0, 0).start()
        zbuf[...] = jnp.zeros_like(zbuf)
        lax.fori_loop(0, N_EXPERTS, lambda e, c: (pad_fill(e, False), c)[1], 0)
        lax.fori_loop(nu_ref[0], n_blocks, lambda b, c: (fill_copy(b * EXPERT_BLOCK, EXPERT_BLOCK).start(), c)[1], 0)
        lax.fori_loop(0, N_EXPERTS, lambda e, c: (pad_fill(e, True), c)[1], 0)
        lax.fori_loop(nu_ref[0], n_blocks, lambda b, c: (fill_copy(0, EXPERT_BLOCK).wait(), c)[1], 0)

    idx_copy(i, slot).wait()

    @pl.when(i + 1 < n_tiles)
    def _():
        idx_copy(i + 1, 1 - slot).start()

    def send(k, j):
        pltpu.make_async_copy(_token_tile(x_ref, j), xs_hbm.at[idx_smem[slot, k, j]], sem_out).start(priority=j % 2)
    _for_each_assignment(send)

    for _ in range(TOP_K):
        pltpu.make_async_copy(x_ref, x_ref, sem_out).wait()


def _dispatch(fill_start, n_used, pos_tiles, x1, n_sorted_rows, *, layer):
    T = x1.shape[0] // TILE_SUBLANES
    n_tiles = T // TOKEN_TILE
    return pl.pallas_call(
        functools.partial(_dispatch_kernel, n_tiles=n_tiles, n_blocks=n_sorted_rows // EXPERT_BLOCK),
        grid_spec=pltpu.PrefetchScalarGridSpec(
            num_scalar_prefetch=2,
            grid=(n_tiles,),
            in_specs=[
                pl.BlockSpec(memory_space=pl.ANY),
                pl.BlockSpec((TOKEN_TILE * TILE_SUBLANES, LANES), lambda i, fill, nu: (i, 0)),
            ],
            out_specs=pl.BlockSpec(memory_space=pl.ANY),
            scratch_shapes=[
                pltpu.SMEM((2, TOP_K, TOKEN_TILE), jnp.int32),
                pltpu.VMEM((EXPERT_BLOCK, TILE_SUBLANES, LANES), F32),
                pltpu.SemaphoreType.DMA((2,)),
                pltpu.SemaphoreType.DMA(()),
                pltpu.SemaphoreType.DMA(()),
            ],
        ),
        out_shape=jax.ShapeDtypeStruct((n_sorted_rows, TILE_SUBLANES, LANES), F32),
        compiler_params=pltpu.CompilerParams(dimension_semantics=("arbitrary",), vmem_limit_bytes=VMEM_LIMIT),
        name=f"dispatch_l{layer}",
    )(fill_start, n_used, pos_tiles, x1)


def _expert_kernel(be_ref, nu_ref, xs_ref, wg_ref, wu_ref, wd_ref, y_ref, wgu_bf, wd_bf):
    i = pl.program_id(0)

    @pl.when(i < nu_ref[0])
    def _():
        new_expert = (i == 0) | (be_ref[i] != be_ref[jnp.maximum(i - 1, 0)])

        @pl.when(new_expert)
        def _():
            wgu_bf[:, 0:EXPERT_FF] = wg_ref[...].astype(BF16)
            wgu_bf[:, EXPERT_FF:2 * EXPERT_FF] = wu_ref[...].astype(BF16)
            wd_bf[...] = wd_ref[...].astype(BF16)

        for r0 in range(0, EXPERT_BLOCK, EXPERT_CHUNK):
            h = _dot(_load_token_tiles(xs_ref, EXPERT_CHUNK, r0).astype(BF16), wgu_bf[...])
            g = h[:, 0:EXPERT_FF]
            a = (g * _sigmoid(g) * h[:, EXPERT_FF:2 * EXPERT_FF]).astype(BF16)
            _store_token_tiles(y_ref, _dot(a, wd_bf[...]), r0)

    @pl.when(i >= nu_ref[0])
    def _():
        y_ref[...] = jnp.zeros_like(y_ref)


def _experts(blk_expert, n_used, xs, w_gate, w_up, w_down, *, layer):
    nb = blk_expert.shape[0]
    blk = EXPERT_BLOCK
    used = lambda i, nu: jnp.minimum(i, nu[0] - 1)
    w_spec = lambda *shape: pl.BlockSpec(
        (None,) + shape, lambda i, be, nu: (layer * N_EXPERTS + be[used(i, nu)], 0, 0))
    in_row_spec = pl.BlockSpec((blk * TILE_SUBLANES, LANES), lambda i, be, nu: (used(i, nu), 0))
    return pl.pallas_call(
        _expert_kernel,
        grid_spec=pltpu.PrefetchScalarGridSpec(
            num_scalar_prefetch=2,
            grid=(nb,),
            in_specs=[in_row_spec, w_spec(D_MODEL, EXPERT_FF), w_spec(D_MODEL, EXPERT_FF), w_spec(EXPERT_FF, D_MODEL)],
            out_specs=pl.BlockSpec((blk * TILE_SUBLANES, LANES), lambda i, be, nu: (i, 0)),
            scratch_shapes=[
                pltpu.VMEM((D_MODEL, 2 * EXPERT_FF), BF16),
                pltpu.VMEM((EXPERT_FF, D_MODEL), BF16),
            ],
        ),
        out_shape=jax.ShapeDtypeStruct((nb * blk * TILE_SUBLANES, LANES), F32),
        compiler_params=pltpu.CompilerParams(dimension_semantics=("arbitrary",), vmem_limit_bytes=VMEM_LIMIT),
        name=f"experts_l{layer}",
    )(blk_expert, n_used, xs, w_gate, w_up, w_down)


def _combine_kernel(pos_hbm, y_hbm, x_ref, gate_ref, wsg_ref, wsu_ref, wsd_ref, ln2g_ref, ln2b_ref, o_ref,
                    idx_smem, ybuf, sem_idx, sem_y, *, n_tiles):
    i = pl.program_id(0)
    slot = lax.rem(i, 2)
    nslot = 1 - slot
    rows_per_slot = TOP_K * TOKEN_TILE

    def idx_copy(tile, sl):
        return pltpu.make_async_copy(pos_hbm.at[tile], idx_smem.at[sl], sem_idx.at[sl])

    def start_gather(sl):
        def fetch(k, j):
            pltpu.make_async_copy(y_hbm.at[idx_smem[sl, k, j]],
                                  _token_tile(ybuf, sl * rows_per_slot + k * TOKEN_TILE + j), sem_y.at[sl]
                                  ).start(priority=j % 2)
        _for_each_assignment(fetch)

    @pl.when(i == 0)
    def _():
        idx_copy(0, 0).start()
        idx_copy(0, 0).wait()
        start_gather(0)
        if n_tiles > 1:
            idx_copy(1, 1).start()

    @pl.when(i + 1 < n_tiles)
    def _():
        idx_copy(i + 1, nslot).wait()
        start_gather(nslot)

    @pl.when(i + 2 < n_tiles)
    def _():
        idx_copy(i + 2, slot).start()

    x = _load_token_tiles(x_ref, TOKEN_TILE)
    xb = x.astype(BF16)
    g = _dot(xb, wsg_ref[...])
    a = (g * _sigmoid(g) * _dot(xb, wsu_ref[...])).astype(BF16)
    shared = _dot(a, wsd_ref[...])

    base = slot * rows_per_slot
    slot_rows = ybuf.at[pl.ds(pl.multiple_of(base * TILE_SUBLANES, TILE_SUBLANES), rows_per_slot * TILE_SUBLANES), :]
    pltpu.make_async_copy(slot_rows, slot_rows, sem_y.at[slot]).wait()
    gate = gate_ref[...]
    routed = None
    for k in range(TOP_K):
        term = gate[:, k:k + 1] * _load_token_tiles(ybuf, TOKEN_TILE, base + k * TOKEN_TILE)
        routed = term if routed is None else routed + term
    o_ref[...] = _layer_norm(ALPHA * x + (routed + shared), ln2g_ref[...], ln2b_ref[...])


def _combine(pos_tiles, y_sorted, x1, gate_t, w_sg, w_su, w_sd, ln2_g, ln2_b, *, layer):
    T = x1.shape[0] // TILE_SUBLANES
    tc = TOKEN_TILE
    n_tiles = T // tc
    per_layer = lambda *shape: pl.BlockSpec((None,) + shape, lambda i: (layer,) + (0,) * len(shape))
    return pl.pallas_call(
        functools.partial(_combine_kernel, n_tiles=n_tiles),
        grid=(n_tiles,),
        in_specs=[
            pl.BlockSpec(memory_space=pl.ANY),
            pl.BlockSpec(memory_space=pl.ANY),
            pl.BlockSpec((tc * TILE_SUBLANES, LANES), lambda i: (i, 0)),
            pl.BlockSpec((tc, TOP_K), lambda i: (i, 0)),
            per_layer(D_MODEL, EXPERT_FF),
            per_layer(D_MODEL, EXPERT_FF),
            per_layer(EXPERT_FF, D_MODEL),
            per_layer(1, D_MODEL),
            per_layer(1, D_MODEL),
        ],
        out_specs=pl.BlockSpec((tc, D_MODEL), lambda i: (i, 0)),
        out_shape=jax.ShapeDtypeStruct((T, D_MODEL), F32),
        scratch_shapes=[
            pltpu.SMEM((2, TOP_K, TOKEN_TILE), jnp.int32),
            pltpu.VMEM((2 * TOP_K * tc * TILE_SUBLANES, LANES), F32),
            pltpu.SemaphoreType.DMA((2,)),
            pltpu.SemaphoreType.DMA((2,)),
        ],
        compiler_params=pltpu.CompilerParams(dimension_semantics=("arbitrary",), vmem_limit_bytes=VMEM_LIMIT),
        name=f"combine_l{layer}",
    )(pos_tiles, y_sorted, x1, gate_t, w_sg, w_su, w_sd, ln2_g, ln2_b)


def _dispatch_plan(ek, rk, counts_f):
    T = ek.shape[1]
    blk = EXPERT_BLOCK
    nb = (T * TOP_K) // blk + N_EXPERTS
    experts = jnp.arange(N_EXPERTS, dtype=jnp.int32)
    counts = counts_f[:, 0].astype(jnp.int32)
    pcounts = (counts + blk - 1) // blk * blk
    pends = jnp.sum(jnp.where(experts[None, :] <= experts[:, None], pcounts[None, :], 0), axis=1)
    pstarts = pends - pcounts
    pos = jnp.sum(jnp.where(ek[None] == experts[:, None, None], pstarts[:, None, None], 0), axis=0) + rk
    n_used = (pends[-1] // blk).reshape(1)
    block_row0 = jnp.arange(nb, dtype=jnp.int32) * blk
    blk_expert = jnp.minimum(jnp.sum((pends[None, :] <= block_row0[:, None]).astype(jnp.int32), axis=1), N_EXPERTS - 1)
    fill_start = pstarts + counts
    pos_tiles = pos.reshape(TOP_K, T // TOKEN_TILE, TOKEN_TILE).transpose(1, 0, 2)
    return blk_expert, n_used, fill_start, pos_tiles, nb * blk


def kernel(x, ln0_g, ln0_b, w_in, b_in, w_pool, pool_scale, attn_sinks, w_br_pool, w_br_attn, w_out, ln1_g, ln1_b,
           w_router, router_bias, w_exp_gate, w_exp_up, w_exp_down, w_sh_gate, w_sh_up, w_sh_down, ln2_g, ln2_b):
    B, S, D = x.shape
    depth = w_in.shape[0]
    row = lambda a: a.reshape(a.shape[0], 1, a.shape[1])
    w_in_b, w_pool_b = w_in.astype(BF16), w_pool.astype(BF16)
    w_brp_b, w_bra_b, w_out_b = w_br_pool.astype(BF16), w_br_attn.astype(BF16), w_out.astype(BF16)
    w_router_t = jnp.swapaxes(w_router, 1, 2).astype(BF16)
    bias_col = router_bias.reshape(depth, N_EXPERTS, 1)
    w_sg_b, w_su_b, w_sd_b = w_sh_gate.astype(BF16), w_sh_up.astype(BF16), w_sh_down.astype(BF16)
    w_eg = w_exp_gate.reshape(depth * N_EXPERTS, D, EXPERT_FF)
    w_eu = w_exp_up.reshape(depth * N_EXPERTS, D, EXPERT_FF)
    w_ed = w_exp_down.reshape(depth * N_EXPERTS, EXPERT_FF, D)
    ln0_g2, ln0_b2 = ln0_g.reshape(1, D), ln0_b.reshape(1, D)

    for l in range(depth):
        x1 = _mixer(x, attn_sinks, ln0_g2, ln0_b2, w_in_b, row(b_in), w_pool_b, row(pool_scale), w_brp_b, w_bra_b,
                    w_out_b, row(ln1_g), row(ln1_b), layer=l, pre_ln=(l == 0))
        ek, rk, gk, counts_f = _router(x1, w_router_t, bias_col, layer=l)
        blk_expert, n_used, fill_start, pos_tiles, n_sorted = _dispatch_plan(ek, rk, counts_f)
        xs = _dispatch(fill_start, n_used, pos_tiles, x1, n_sorted + EXPERT_BLOCK, layer=l)
        y_sorted = _experts(blk_expert, n_used, xs.reshape(-1, LANES), w_eg, w_eu, w_ed, layer=l)
        x = _combine(pos_tiles, y_sorted.reshape(-1, TILE_SUBLANES, LANES), x1, gk.T, w_sg_b, w_su_b, w_sd_b,
                     row(ln2_g), row(ln2_b), layer=l).reshape(B, S, D)
    return x
```

```python
import functools

import jax
import jax.numpy as jnp
from jax import lax
from jax.experimental import pallas as pl
from jax.experimental.pallas import tpu as pltpu

D_MODEL = 1024
DEPTH = 4
POOL_GROUPS = 4
POOL_GROUP_CH = 128
POOL_WIDTH = POOL_GROUPS * POOL_GROUP_CH
POOL_WINDOWS = (2, 4, 8, 16)
POOL_HALO = 16
N_Q_HEADS = 8
N_KV_HEADS = 2
HEAD_DIM = 64
Q_WIDTH = N_Q_HEADS * HEAD_DIM
KV_WIDTH = N_KV_HEADS * HEAD_DIM
WINDOW = 128
ATT_BLOCK = 128
D_IN = POOL_WIDTH + Q_WIDTH + 2 * KV_WIDTH + 2 * D_MODEL
QKV_START = POOL_WIDTH
GATE_START = POOL_WIDTH + Q_WIDTH + 2 * KV_WIDTH
N_EXPERTS = 64
EXPERT_FF = 256
TOP_K = 8
N_EXPERT_GROUPS = 8
EXPERTS_PER_GROUP = N_EXPERTS // N_EXPERT_GROUPS
TOPK_GROUPS = 4
ROUTED_SCALE = 2.5
ALPHA = (2.0 * DEPTH) ** 0.25
LN_EPS = 1e-5
SEGMENT_HEADS = (0, 2, 1, 3, 4, 6, 5, 7)
ALIBI_SLOPES = tuple(float(2.0 ** (-8.0 * h / N_Q_HEADS)) for h in range(1, N_Q_HEADS + 1))

LANES = 128
SUBLANES = 8
TILE_SUBLANES = D_MODEL // LANES
PACK_SUBLANES = TILE_SUBLANES // 2
MIXER_TILE = 256
ROUTER_TILE = 1024
EXPERT_BLOCK = 512
EXPERT_CHUNK = 512
TOKEN_TILE = 128
VMEM_LIMIT = 48 * 1024 * 1024

BF16 = jnp.bfloat16
F32 = jnp.float32

assert TILE_SUBLANES == SUBLANES and TOP_K == SUBLANES and TOKEN_TILE == LANES


def _dot(a, b):
    return jnp.dot(a, b, preferred_element_type=F32)


def _dot_nt(a, b):
    return lax.dot_general(a, b, (((1,), (1,)), ((), ())), preferred_element_type=F32)


def _layer_norm(x, g, b):
    mu = jnp.mean(x, axis=-1, keepdims=True)
    xc = x - mu
    var = jnp.mean(xc * xc, axis=-1, keepdims=True)
    return xc * lax.rsqrt(var + LN_EPS) * g + b


def _sigmoid(x):
    return 1.0 / (1.0 + jnp.exp(-x))


def _load_token_tiles(ref, rows, row0=0):
    return jnp.concatenate(
        [ref[pl.ds(row0 * TILE_SUBLANES + j, rows, stride=TILE_SUBLANES), :] for j in range(TILE_SUBLANES)], axis=1)


def _store_token_tiles(ref, value, row0=0):
    for j in range(TILE_SUBLANES):
        ref[pl.ds(row0 * TILE_SUBLANES + j, value.shape[0], stride=TILE_SUBLANES), :] = (
            value[:, j * LANES:(j + 1) * LANES])


def _token_tile(ref, row):
    return ref.at[pl.ds(pl.multiple_of(row * TILE_SUBLANES, TILE_SUBLANES), TILE_SUBLANES), :]


def _bf16_bits(x):
    return lax.bitcast_convert_type(x.astype(BF16).astype(F32), jnp.uint32)


def _store_packed_tokens(ref, value, row0=0):
    n = value.shape[0]
    for c in range(PACK_SUBLANES):
        lo = _bf16_bits(value[:, c * LANES:(c + 1) * LANES])
        hi = _bf16_bits(value[:, (c + PACK_SUBLANES) * LANES:(c + PACK_SUBLANES + 1) * LANES])
        ref[pl.ds(row0 * PACK_SUBLANES + c, n, stride=PACK_SUBLANES), :] = (
            lax.shift_right_logical(lo, jnp.uint32(16)) | hi)


def _load_packed_tokens(ref, rows, row0=0):
    los, his = [], []
    for c in range(PACK_SUBLANES):
        w = ref[pl.ds(row0 * PACK_SUBLANES + c, rows, stride=PACK_SUBLANES), :]
        los.append(lax.bitcast_convert_type(lax.shift_left(w, jnp.uint32(16)), F32))
        his.append(lax.bitcast_convert_type(w & jnp.uint32(0xFFFF0000), F32))
    return jnp.concatenate(los + his, axis=1)


def _packed_token(ref, row):
    return ref.at[pl.ds(pl.multiple_of(row * PACK_SUBLANES, PACK_SUBLANES), PACK_SUBLANES), :]


def _mixer_kernel(sinks_ref, x_ref, ln0g_ref, ln0b_ref, w_in_ref, b_in_ref, w_pool_ref, pscale_ref,
                  w_brp_ref, w_bra_ref, w_out_ref, ln1g_ref, ln1b_ref, o_ref, op_ref, ubuf, kvbuf, bias_tab,
                  *, layer, pre_ln, tq):
    s = pl.program_id(1)

    @pl.when(s == 0)
    def _():
        ubuf[0:POOL_HALO, :] = jnp.zeros((POOL_HALO, POOL_WIDTH), F32)
        kvbuf[0:ATT_BLOCK, :] = jnp.zeros((ATT_BLOCK, 8 * LANES), BF16)

    x = x_ref[...]
    if pre_ln:
        x = _layer_norm(x, ln0g_ref[...], ln0b_ref[...])
    xb = x.astype(BF16)

    u = _dot(xb, w_in_ref[:, 0:POOL_WIDTH]) + b_in_ref[:, 0:POOL_WIDTH]
    ubuf[POOL_HALO:POOL_HALO + tq, :] = u
    pos = (s * tq + lax.broadcasted_iota(jnp.int32, (tq, 1), 0)).astype(F32)
    mixed_parts = []
    for g, w in enumerate(POOL_WINDOWS):
        sl = slice(g * POOL_GROUP_CH, (g + 1) * POOL_GROUP_CH)
        cur = ubuf[POOL_HALO:POOL_HALO + tq, sl]
        acc = cur
        for j in range(1, w):
            acc = acc + ubuf[POOL_HALO - j:POOL_HALO - j + tq, sl]
        inv_cnt = 1.0 / jnp.minimum(pos + 1.0, float(w))
        d = (acc * inv_cnt - cur).astype(BF16)
        mixed_parts.append(_dot(d, w_pool_ref[g]) * pscale_ref[:, sl])
    mixed = jnp.concatenate(mixed_parts, axis=1).astype(BF16)
    y_pool = _dot(mixed, w_brp_ref[...])
    ubuf[0:POOL_HALO, :] = ubuf[tq:tq + POOL_HALO, :]

    qkv = _dot(xb, w_in_ref[:, QKV_START:GATE_START]) + b_in_ref[:, QKV_START:GATE_START]
    q = (qkv[:, 0:Q_WIDTH] * (HEAD_DIM ** -0.5)).astype(BF16)
    k = qkv[:, Q_WIDTH:Q_WIDTH + KV_WIDTH]
    v = qkv[:, Q_WIDTH + KV_WIDTH:Q_WIDTH + 2 * KV_WIDTH]
    lo = lax.broadcasted_iota(jnp.int32, (tq, LANES), 1) < HEAD_DIM
    k_sw = pltpu.roll(k, HEAD_DIM, axis=1)
    v_sw = pltpu.roll(v, HEAD_DIM, axis=1)
    zero = jnp.zeros((tq, LANES), F32)
    slabs = (
        jnp.where(lo, k, zero), jnp.where(lo, zero, k_sw),
        jnp.where(lo, k_sw, zero), jnp.where(lo, zero, k),
        jnp.where(lo, v, zero), jnp.where(lo, zero, v_sw),
        jnp.where(lo, v_sw, zero), jnp.where(lo, zero, v),
    )
    for i, slab in enumerate(slabs):
        kvbuf[ATT_BLOCK:ATT_BLOCK + tq, i * LANES:(i + 1) * LANES] = slab.astype(BF16)

    @pl.when((pl.program_id(0) == 0) & (s == 0))
    def _():
        qi = lax.broadcasted_iota(jnp.int32, (ATT_BLOCK, 2 * ATT_BLOCK), 0)
        kj = lax.broadcasted_iota(jnp.int32, (ATT_BLOCK, 2 * ATT_BLOCK), 1)
        dist = ATT_BLOCK + qi - kj
        band_ok = (dist >= 0) & (dist < WINDOW)
        distf = dist.astype(F32)
        for i, h in enumerate(SEGMENT_HEADS):
            bias_tab[i * ATT_BLOCK:(i + 1) * ATT_BLOCK, :] = jnp.where(band_ok, -ALIBI_SLOPES[h] * distf, -jnp.inf)

    sink_col = jnp.concatenate([jnp.full((ATT_BLOCK, 1), sinks_ref[layer, h], F32) for h in SEGMENT_HEADS], axis=0)
    key_col = lax.broadcasted_iota(jnp.int32, (1, 2 * ATT_BLOCK), 1)
    o_blocks = []
    for qb in range(tq // ATT_BLOCK):
        r0 = qb * ATT_BLOCK
        first_key_pos = s * tq + r0 - ATT_BLOCK
        score_parts = []
        for hk in range(N_KV_HEADS):
            q_pairs = jnp.concatenate(
                [q[r0:r0 + ATT_BLOCK, (2 * hk + pj) * LANES:(2 * hk + pj + 1) * LANES] for pj in range(2)], axis=0)
            for half in range(2):
                k_slab = kvbuf[r0:r0 + 2 * ATT_BLOCK, (2 * hk + half) * LANES:(2 * hk + half + 1) * LANES]
                score_parts.append(_dot_nt(q_pairs, k_slab))
        sc = jnp.concatenate(score_parts, axis=0) + bias_tab[...]
        sc = jnp.where(key_col + first_key_pos < 0, -jnp.inf, sc)
        m = jnp.maximum(jnp.max(sc, axis=1, keepdims=True), sink_col)
        p = jnp.exp(sc - m).astype(BF16)
        den = _dot(p, jnp.ones((2 * ATT_BLOCK, LANES), BF16)) + jnp.exp(sink_col - m)
        inv_den = 1.0 / den
        o_pairs = []
        for hk in range(N_KV_HEADS):
            pv = None
            for half in range(2):
                rows = (2 * hk + half) * 2 * ATT_BLOCK
                v_slab = kvbuf[r0:r0 + 2 * ATT_BLOCK, (4 + 2 * hk + half) * LANES:(4 + 2 * hk + half + 1) * LANES]
                contrib = _dot(p[rows:rows + 2 * ATT_BLOCK], v_slab) * inv_den[rows:rows + 2 * ATT_BLOCK]
                pv = contrib if pv is None else pv + contrib
            o_pairs += [pv[0:ATT_BLOCK], pv[ATT_BLOCK:2 * ATT_BLOCK]]
        o_blocks.append(jnp.concatenate(o_pairs, axis=1))
    o = jnp.concatenate(o_blocks, axis=0).astype(BF16)
    y_attn = _dot(o, w_bra_ref[...])
    kvbuf[0:ATT_BLOCK, :] = kvbuf[tq:tq + ATT_BLOCK, :]

    gates = _dot(xb, w_in_ref[:, GATE_START:D_IN]) + b_in_ref[:, GATE_START:D_IN]
    merged = _sigmoid(gates[:, 0:D_MODEL]) * y_pool + _sigmoid(gates[:, D_MODEL:2 * D_MODEL]) * y_attn
    mix = _dot(merged.astype(BF16), w_out_ref[...])
    x1 = _layer_norm(ALPHA * x + mix, ln1g_ref[...], ln1b_ref[...])
    _store_token_tiles(o_ref, x1)
    _store_packed_tokens(op_ref, x1)


def _mixer(x, sinks, ln0_g, ln0_b, w_in, b_in, w_pool, pool_scale, w_br_pool, w_br_attn, w_out, ln1_g, ln1_b,
           *, layer, pre_ln):
    B, S, D = x.shape
    tq = MIXER_TILE
    per_layer3 = lambda *shape: pl.BlockSpec((None,) + shape, lambda b, s: (layer,) + (0,) * len(shape))
    return pl.pallas_call(
        functools.partial(_mixer_kernel, layer=layer, pre_ln=pre_ln, tq=tq),
        grid=(B, S // tq),
        in_specs=[
            pl.BlockSpec(memory_space=pltpu.SMEM),
            pl.BlockSpec((None, tq, D), lambda b, s: (b, s, 0)),
            pl.BlockSpec((1, D), lambda b, s: (0, 0)),
            pl.BlockSpec((1, D), lambda b, s: (0, 0)),
            per_layer3(D, D_IN),
            per_layer3(1, D_IN),
            per_layer3(POOL_GROUPS, POOL_GROUP_CH, POOL_GROUP_CH),
            per_layer3(1, POOL_WIDTH),
            per_layer3(POOL_WIDTH, D),
            per_layer3(Q_WIDTH, D),
            per_layer3(D, D),
            per_layer3(1, D),
            per_layer3(1, D),
        ],
        out_specs=[
            pl.BlockSpec((tq * TILE_SUBLANES, LANES), lambda b, s: (b * (S // tq) + s, 0)),
            pl.BlockSpec((tq * PACK_SUBLANES, LANES), lambda b, s: (b * (S // tq) + s, 0)),
        ],
        out_shape=[
            jax.ShapeDtypeStruct((B * S * TILE_SUBLANES, LANES), F32),
            jax.ShapeDtypeStruct((B * S * PACK_SUBLANES, LANES), jnp.uint32),
        ],
        scratch_shapes=[
            pltpu.VMEM((POOL_HALO + tq, POOL_WIDTH), F32),
            pltpu.VMEM((ATT_BLOCK + tq, 8 * LANES), BF16),
            pltpu.VMEM((N_Q_HEADS * ATT_BLOCK, 2 * ATT_BLOCK), F32),
        ],
        compiler_params=pltpu.CompilerParams(
            dimension_semantics=("arbitrary", "arbitrary"), vmem_limit_bytes=VMEM_LIMIT),
        name=f"mixer_l{layer}",
    )(sinks, x, ln0_g, ln0_b, w_in, b_in, w_pool, pool_scale, w_br_pool, w_br_attn, w_out, ln1_g, ln1_b)


def _first_index_of_max(vals, iota, n):
    m = jnp.max(vals, axis=0, keepdims=True)
    idx = jnp.min(jnp.where(vals == m, iota, n), axis=0, keepdims=True)
    return m, idx


def _route(xb, wr_ref, bias_ref, ek_ref, rk_ref, gk_ref, cnt_ref, carry):
    tr = xb.shape[0]
    logits = _dot_nt(wr_ref[...], xb)
    scores = _sigmoid(logits)
    biased = scores + bias_ref[...]
    neg_inf = -jnp.inf

    io8 = lax.broadcasted_iota(jnp.int32, (EXPERTS_PER_GROUP, tr), 0)
    group_rows = []
    for g in range(N_EXPERT_GROUPS):
        blk = biased[g * EXPERTS_PER_GROUP:(g + 1) * EXPERTS_PER_GROUP]
        m1, i1 = _first_index_of_max(blk, io8, EXPERTS_PER_GROUP)
        m2 = jnp.max(jnp.where(io8 == i1, neg_inf, blk), axis=0, keepdims=True)
        group_rows.append(m1 + m2)
    gscore = jnp.concatenate(group_rows, axis=0)
    iog = lax.broadcasted_iota(jnp.int32, (N_EXPERT_GROUPS, tr), 0)
    keep = jnp.zeros((N_EXPERT_GROUPS, tr), F32)
    for _ in range(TOPK_GROUPS):
        _, gi = _first_index_of_max(gscore, iog, N_EXPERT_GROUPS)
        hit = iog == gi
        keep = jnp.where(hit, 1.0, keep)
        gscore = jnp.where(hit, neg_inf, gscore)
    masked = jnp.concatenate(
        [jnp.where(keep[g:g + 1] > 0.0, biased[g * EXPERTS_PER_GROUP:(g + 1) * EXPERTS_PER_GROUP], neg_inf)
         for g in range(N_EXPERT_GROUPS)], axis=0)

    ioe = lax.broadcasted_iota(jnp.int32, (N_EXPERTS, tr), 0)
    sel = jnp.zeros((N_EXPERTS, tr), F32)
    e_rows, s_rows, hits = [], [], []
    for _ in range(TOP_K):
        _, ei = _first_index_of_max(masked, ioe, N_EXPERTS)
        hit = ioe == ei
        sel = jnp.where(hit, 1.0, sel)
        masked = jnp.where(hit, neg_inf, masked)
        e_rows.append(ei)
        hits.append(hit)
        s_rows.append(jnp.sum(jnp.where(hit, scores, 0.0), axis=0, keepdims=True))
    sel_scores = jnp.concatenate(s_rows, axis=0)
    gk_ref[...] = sel_scores / jnp.sum(sel_scores, axis=0, keepdims=True) * ROUTED_SCALE
    ek_ref[...] = jnp.concatenate(e_rows, axis=0)

    before = (lax.broadcasted_iota(jnp.int32, (tr, tr), 0) < lax.broadcasted_iota(jnp.int32, (tr, tr), 1))
    prefix = _dot(sel.astype(BF16), jnp.where(before, 1.0, 0.0).astype(BF16))
    rank_full = prefix + carry[...]
    rk_ref[...] = jnp.concatenate(
        [jnp.sum(jnp.where(hit, rank_full, 0.0), axis=0, keepdims=True) for hit in hits], axis=0).astype(jnp.int32)
    total = carry[...] + jnp.sum(sel, axis=1, keepdims=True)
    carry[...] = total
    cnt_ref[...] = jnp.broadcast_to(total, (N_EXPERTS, LANES))


def _router_kernel(x_ref, wr_ref, bias_ref, ek_ref, rk_ref, gk_ref, cnt_ref, carry, *, tr):
    @pl.when(pl.program_id(0) == 0)
    def _():
        carry[...] = jnp.zeros_like(carry)

    _route(_load_token_tiles(x_ref, tr).astype(BF16), wr_ref, bias_ref, ek_ref, rk_ref, gk_ref, cnt_ref, carry)


def _router(x1, w_router_t, bias_col, *, layer):
    T = x1.shape[0] // TILE_SUBLANES
    tr = ROUTER_TILE
    row_spec = pl.BlockSpec((TOP_K, tr), lambda i: (0, i))
    return pl.pallas_call(
        functools.partial(_router_kernel, tr=tr),
        grid=(T // tr,),
        in_specs=[
            pl.BlockSpec((tr * TILE_SUBLANES, LANES), lambda i: (i, 0)),
            pl.BlockSpec((None, N_EXPERTS, D_MODEL), lambda i: (layer, 0, 0)),
            pl.BlockSpec((None, N_EXPERTS, 1), lambda i: (layer, 0, 0)),
        ],
        out_specs=[row_spec, row_spec, row_spec, pl.BlockSpec((N_EXPERTS, LANES), lambda i: (0, 0))],
        out_shape=[
            jax.ShapeDtypeStruct((TOP_K, T), jnp.int32),
            jax.ShapeDtypeStruct((TOP_K, T), jnp.int32),
            jax.ShapeDtypeStruct((TOP_K, T), F32),
            jax.ShapeDtypeStruct((N_EXPERTS, LANES), F32),
        ],
        scratch_shapes=[pltpu.VMEM((N_EXPERTS, 1), F32)],
        compiler_params=pltpu.CompilerParams(dimension_semantics=("arbitrary",), vmem_limit_bytes=VMEM_LIMIT),
        name=f"router_l{layer}",
    )(x1, w_router_t, bias_col)


def _for_each_assignment(fn):
    for k in range(TOP_K):
        for j in range(TOKEN_TILE):
            fn(k, j)


def _dispatch_kernel(fill_ref, nu_ref, pos_hbm, x_ref, xs_hbm, idx_smem, zbuf, sem_idx, sem_out, sem_fill,
                     *, n_tiles, n_blocks):
    i = pl.program_id(0)
    slot = lax.rem(i, 2)

    def idx_copy(tile, sl):
        return pltpu.make_async_copy(pos_hbm.at[tile], idx_smem.at[sl], sem_idx.at[sl])

    def fill_copy(row0, rows):
        n = rows * PACK_SUBLANES
        return pltpu.make_async_copy(
            zbuf.at[pl.ds(0, n), :], xs_hbm.at[pl.ds(pl.multiple_of(row0 * PACK_SUBLANES, PACK_SUBLANES), n), :],
            sem_fill)

    def pad_fill(e, wait):
        row0 = fill_ref[e]
        n_pad = (0 - row0) & (EXPERT_BLOCK - 1)
        piece = EXPERT_BLOCK // 2
        while piece >= 1:
            has = (n_pad & piece) != 0

            @pl.when(has)
            def _(row0=row0, piece=piece):
                fill_copy(0 if wait else row0, piece).wait() if wait else fill_copy(row0, piece).start()
            row0 = row0 + jnp.where(has, piece, 0)
            piece //= 2

    @pl.when(i == 0)
    def _():
        idx_copy(0, 0).start()
        zbuf[...] = jnp.zeros_like(zbuf)
        lax.fori_loop(0, N_EXPERTS, lambda e, c: (pad_fill(e, False), c)[1], 0)
        lax.fori_loop(nu_ref[0], n_blocks, lambda b, c: (fill_copy(b * EXPERT_BLOCK, EXPERT_BLOCK).start(), c)[1], 0)
        lax.fori_loop(0, N_EXPERTS, lambda e, c: (pad_fill(e, True), c)[1], 0)
        lax.fori_loop(nu_ref[0], n_blocks, lambda b, c: (fill_copy(0, EXPERT_BLOCK).wait(), c)[1], 0)

    idx_copy(i, slot).wait()

    @pl.when(i + 1 < n_tiles)
    def _():
        idx_copy(i + 1, 1 - slot).start()

    def send(k, j):
        pltpu.make_async_copy(_packed_token(x_ref, j), _packed_token(xs_hbm, idx_smem[slot, k, j]), sem_out
                              ).start(priority=j % 2)
    _for_each_assignment(send)

    for _ in range(TOP_K):
        pltpu.make_async_copy(x_ref, x_ref, sem_out).wait()


def _dispatch(fill_start, n_used, pos_tiles, x1p, n_sorted_rows, *, layer):
    T = x1p.shape[0] // PACK_SUBLANES
    n_tiles = T // TOKEN_TILE
    return pl.pallas_call(
        functools.partial(_dispatch_kernel, n_tiles=n_tiles, n_blocks=n_sorted_rows // EXPERT_BLOCK),
        grid_spec=pltpu.PrefetchScalarGridSpec(
            num_scalar_prefetch=2,
            grid=(n_tiles,),
            in_specs=[
                pl.BlockSpec(memory_space=pl.ANY),
                pl.BlockSpec((TOKEN_TILE * PACK_SUBLANES, LANES), lambda i, fill, nu: (i, 0)),
            ],
            out_specs=pl.BlockSpec(memory_space=pl.ANY),
            scratch_shapes=[
                pltpu.SMEM((2, TOP_K, TOKEN_TILE), jnp.int32),
                pltpu.VMEM((EXPERT_BLOCK * PACK_SUBLANES, LANES), jnp.uint32),
                pltpu.SemaphoreType.DMA((2,)),
                pltpu.SemaphoreType.DMA(()),
                pltpu.SemaphoreType.DMA(()),
            ],
        ),
        out_shape=jax.ShapeDtypeStruct((n_sorted_rows * PACK_SUBLANES, LANES), jnp.uint32),
        compiler_params=pltpu.CompilerParams(dimension_semantics=("arbitrary",), vmem_limit_bytes=VMEM_LIMIT),
        name=f"dispatch_l{layer}",
    )(fill_start, n_used, pos_tiles, x1p)


def _expert_kernel(be_ref, nu_ref, xs_ref, wg_ref, wu_ref, wd_ref, y_ref, wgu_bf, wd_bf):
    i = pl.program_id(0)

    @pl.when(i < nu_ref[0])
    def _():
        new_expert = (i == 0) | (be_ref[i] != be_ref[jnp.maximum(i - 1, 0)])

        @pl.when(new_expert)
        def _():
            wgu_bf[:, 0:EXPERT_FF] = wg_ref[...].astype(BF16)
            wgu_bf[:, EXPERT_FF:2 * EXPERT_FF] = wu_ref[...].astype(BF16)
            wd_bf[...] = wd_ref[...].astype(BF16)

        for r0 in range(0, EXPERT_BLOCK, EXPERT_CHUNK):
            h = _dot(_load_packed_tokens(xs_ref, EXPERT_CHUNK, r0).astype(BF16), wgu_bf[...])
            g = h[:, 0:EXPERT_FF]
            a = (g * _sigmoid(g) * h[:, EXPERT_FF:2 * EXPERT_FF]).astype(BF16)
            _store_packed_tokens(y_ref, _dot(a, wd_bf[...]), r0)

    @pl.when(i >= nu_ref[0])
    def _():
        y_ref[...] = jnp.zeros_like(y_ref)


def _experts(blk_expert, n_used, xs, w_gate, w_up, w_down, *, layer):
    nb = blk_expert.shape[0]
    blk = EXPERT_BLOCK
    used = lambda i, nu: jnp.minimum(i, nu[0] - 1)
    w_spec = lambda *shape: pl.BlockSpec(
        (None,) + shape, lambda i, be, nu: (layer * N_EXPERTS + be[used(i, nu)], 0, 0))
    in_row_spec = pl.BlockSpec((blk * PACK_SUBLANES, LANES), lambda i, be, nu: (used(i, nu), 0))
    return pl.pallas_call(
        _expert_kernel,
        grid_spec=pltpu.PrefetchScalarGridSpec(
            num_scalar_prefetch=2,
            grid=(nb,),
            in_specs=[in_row_spec, w_spec(D_MODEL, EXPERT_FF), w_spec(D_MODEL, EXPERT_FF), w_spec(EXPERT_FF, D_MODEL)],
            out_specs=pl.BlockSpec((blk * PACK_SUBLANES, LANES), lambda i, be, nu: (i, 0)),
            scratch_shapes=[
                pltpu.VMEM((D_MODEL, 2 * EXPERT_FF), BF16),
                pltpu.VMEM((EXPERT_FF, D_MODEL), BF16),
            ],
        ),
        out_shape=jax.ShapeDtypeStruct((nb * blk * PACK_SUBLANES, LANES), jnp.uint32),
        compiler_params=pltpu.CompilerParams(dimension_semantics=("arbitrary",), vmem_limit_bytes=VMEM_LIMIT),
        name=f"experts_l{layer}",
    )(blk_expert, n_used, xs, w_gate, w_up, w_down)


def _combine_kernel(pos_hbm, y_hbm, x_ref, gate_ref, wsg_ref, wsu_ref, wsd_ref, ln2g_ref, ln2b_ref, o_ref,
                    idx_smem, ybuf, sem_idx, sem_y, *, n_tiles):
    i = pl.program_id(0)
    slot = lax.rem(i, 2)
    nslot = 1 - slot
    rows_per_slot = TOP_K * TOKEN_TILE

    def idx_copy(tile, sl):
        return pltpu.make_async_copy(pos_hbm.at[tile], idx_smem.at[sl], sem_idx.at[sl])

    def start_gather(sl):
        def fetch(k, j):
            pltpu.make_async_copy(_packed_token(y_hbm, idx_smem[sl, k, j]),
                                  _packed_token(ybuf, sl * rows_per_slot + k * TOKEN_TILE + j), sem_y.at[sl]
                                  ).start(priority=j % 2)
        _for_each_assignment(fetch)

    @pl.when(i == 0)
    def _():
        idx_copy(0, 0).start()
        idx_copy(0, 0).wait()
        start_gather(0)
        if n_tiles > 1:
            idx_copy(1, 1).start()

    @pl.when(i + 1 < n_tiles)
    def _():
        idx_copy(i + 1, nslot).wait()
        start_gather(nslot)

    @pl.when(i + 2 < n_tiles)
    def _():
        idx_copy(i + 2, slot).start()

    x = _load_token_tiles(x_ref, TOKEN_TILE)
    xb = x.astype(BF16)
    g = _dot(xb, wsg_ref[...])
    a = (g * _sigmoid(g) * _dot(xb, wsu_ref[...])).astype(BF16)
    shared = _dot(a, wsd_ref[...])

    base = slot * rows_per_slot
    slot_rows = ybuf.at[pl.ds(pl.multiple_of(base * PACK_SUBLANES, PACK_SUBLANES), rows_per_slot * PACK_SUBLANES), :]
    pltpu.make_async_copy(slot_rows, slot_rows, sem_y.at[slot]).wait()
    gate = gate_ref[...]
    routed = None
    for k in range(TOP_K):
        term = gate[:, k:k + 1] * _load_packed_tokens(ybuf, TOKEN_TILE, base + k * TOKEN_TILE)
        routed = term if routed is None else routed + term
    o_ref[...] = _layer_norm(ALPHA * x + (routed + shared), ln2g_ref[...], ln2b_ref[...])


def _combine(pos_tiles, y_sorted, x1, gate_t, w_sg, w_su, w_sd, ln2_g, ln2_b, *, layer):
    T = x1.shape[0] // TILE_SUBLANES
    tc = TOKEN_TILE
    n_tiles = T // tc
    per_layer = lambda *shape: pl.BlockSpec((None,) + shape, lambda i: (layer,) + (0,) * len(shape))
    return pl.pallas_call(
        functools.partial(_combine_kernel, n_tiles=n_tiles),
        grid=(n_tiles,),
        in_specs=[
            pl.BlockSpec(memory_space=pl.ANY),
            pl.BlockSpec(memory_space=pl.ANY),
            pl.BlockSpec((tc * TILE_SUBLANES, LANES), lambda i: (i, 0)),
            pl.BlockSpec((tc, TOP_K), lambda i: (i, 0)),
            per_layer(D_MODEL, EXPERT_FF),
            per_layer(D_MODEL, EXPERT_FF),
            per_layer(EXPERT_FF, D_MODEL),
            per_layer(1, D_MODEL),
            per_layer(1, D_MODEL),
        ],
        out_specs=pl.BlockSpec((tc, D_MODEL), lambda i: (i, 0)),
        out_shape=jax.ShapeDtypeStruct((T, D_MODEL), F32),
        scratch_shapes=[
            pltpu.SMEM((2, TOP_K, TOKEN_TILE), jnp.int32),
            pltpu.VMEM((2 * TOP_K * tc * PACK_SUBLANES, LANES), jnp.uint32),
            pltpu.SemaphoreType.DMA((2,)),
            pltpu.SemaphoreType.DMA((2,)),
        ],
        compiler_params=pltpu.CompilerParams(dimension_semantics=("arbitrary",), vmem_limit_bytes=VMEM_LIMIT),
        name=f"combine_l{layer}",
    )(pos_tiles, y_sorted, x1, gate_t, w_sg, w_su, w_sd, ln2_g, ln2_b)


def _dispatch_plan(ek, rk, counts_f):
    T = ek.shape[1]
    blk = EXPERT_BLOCK
    nb = (T * TOP_K) // blk + N_EXPERTS
    experts = jnp.arange(N_EXPERTS, dtype=jnp.int32)
    counts = counts_f[:, 0].astype(jnp.int32)
    pcounts = (counts + blk - 1) // blk * blk
    pends = jnp.sum(jnp.where(experts[None, :] <= experts[:, None], pcounts[None, :], 0), axis=1)
    pstarts = pends - pcounts
    pos = jnp.sum(jnp.where(ek[None] == experts[:, None, None], pstarts[:, None, None], 0), axis=0) + rk
    n_used = (pends[-1] // blk).reshape(1)
    block_row0 = jnp.arange(nb, dtype=jnp.int32) * blk
    blk_expert = jnp.minimum(jnp.sum((pends[None, :] <= block_row0[:, None]).astype(jnp.int32), axis=1), N_EXPERTS - 1)
    fill_start = pstarts + counts
    pos_tiles = pos.reshape(TOP_K, T // TOKEN_TILE, TOKEN_TILE).transpose(1, 0, 2)
    return blk_expert, n_used, fill_start, pos_tiles, nb * blk


def kernel(x, ln0_g, ln0_b, w_in, b_in, w_pool, pool_scale, attn_sinks, w_br_pool, w_br_attn, w_out, ln1_g, ln1_b,
           w_router, router_bias, w_exp_gate, w_exp_up, w_exp_down, w_sh_gate, w_sh_up, w_sh_down, ln2_g, ln2_b):
    B, S, D = x.shape
    depth = w_in.shape[0]
    row = lambda a: a.reshape(a.shape[0], 1, a.shape[1])
    w_in_b, w_pool_b = w_in.astype(BF16), w_pool.astype(BF16)
    w_brp_b, w_bra_b, w_out_b = w_br_pool.astype(BF16), w_br_attn.astype(BF16), w_out.astype(BF16)
    w_router_t = jnp.swapaxes(w_router, 1, 2).astype(BF16)
    bias_col = router_bias.reshape(depth, N_EXPERTS, 1)
    w_sg_b, w_su_b, w_sd_b = w_sh_gate.astype(BF16), w_sh_up.astype(BF16), w_sh_down.astype(BF16)
    w_eg = w_exp_gate.reshape(depth * N_EXPERTS, D, EXPERT_FF)
    w_eu = w_exp_up.reshape(depth * N_EXPERTS, D, EXPERT_FF)
    w_ed = w_exp_down.reshape(depth * N_EXPERTS, EXPERT_FF, D)
    ln0_g2, ln0_b2 = ln0_g.reshape(1, D), ln0_b.reshape(1, D)

    for l in range(depth):
        x1, x1p = _mixer(x, attn_sinks, ln0_g2, ln0_b2, w_in_b, row(b_in), w_pool_b, row(pool_scale), w_brp_b,
                         w_bra_b, w_out_b, row(ln1_g), row(ln1_b), layer=l, pre_ln=(l == 0))
        ek, rk, gk, counts_f = _router(x1, w_router_t, bias_col, layer=l)
        blk_expert, n_used, fill_start, pos_tiles, n_sorted = _dispatch_plan(ek, rk, counts_f)
        xs = _dispatch(fill_start, n_used, pos_tiles, x1p, n_sorted + EXPERT_BLOCK, layer=l)
        y_sorted = _experts(blk_expert, n_used, xs, w_eg, w_eu, w_ed, layer=l)
        x = _combine(pos_tiles, y_sorted, x1, gk.T, w_sg_b, w_su_b, w_sd_b, row(ln2_g), row(ln2_b),
                     layer=l).reshape(B, S, D)
    return x
```

```python
import functools

import jax
import jax.numpy as jnp
from jax import lax
from jax.experimental import pallas as pl
from jax.experimental.pallas import tpu as pltpu

D_MODEL = 1024
DEPTH = 4
POOL_GROUPS = 4
POOL_GROUP_CH = 128
POOL_WIDTH = POOL_GROUPS * POOL_GROUP_CH
POOL_WINDOWS = (2, 4, 8, 16)
POOL_HALO = 16
N_Q_HEADS = 8
N_KV_HEADS = 2
HEAD_DIM = 64
Q_WIDTH = N_Q_HEADS * HEAD_DIM
KV_WIDTH = N_KV_HEADS * HEAD_DIM
WINDOW = 128
ATT_BLOCK = 128
D_IN = POOL_WIDTH + Q_WIDTH + 2 * KV_WIDTH + 2 * D_MODEL
QKV_START = POOL_WIDTH
GATE_START = POOL_WIDTH + Q_WIDTH + 2 * KV_WIDTH
N_EXPERTS = 64
EXPERT_FF = 256
TOP_K = 8
N_EXPERT_GROUPS = 8
EXPERTS_PER_GROUP = N_EXPERTS // N_EXPERT_GROUPS
TOPK_GROUPS = 4
ROUTED_SCALE = 2.5
ALPHA = (2.0 * DEPTH) ** 0.25
LN_EPS = 1e-5
SEGMENT_HEADS = (0, 2, 1, 3, 4, 6, 5, 7)
ALIBI_SLOPES = tuple(float(2.0 ** (-8.0 * h / N_Q_HEADS)) for h in range(1, N_Q_HEADS + 1))

LANES = 128
SUBLANES = 8
TILE_SUBLANES = D_MODEL // LANES
PACK_SUBLANES = TILE_SUBLANES // 2
MIXER_TILE = 256
ROUTER_TILE = 1024
EXPERT_BLOCK = 512
EXPERT_CHUNK = 512
TOKEN_TILE = 128
VMEM_LIMIT = 48 * 1024 * 1024
FUSED_VMEM_LIMIT = 56 * 1024 * 1024

BF16 = jnp.bfloat16
F32 = jnp.float32

assert TILE_SUBLANES == SUBLANES and TOP_K == SUBLANES and TOKEN_TILE == LANES


def _dot(a, b):
    return jnp.dot(a, b, preferred_element_type=F32)


def _dot_nt(a, b):
    return lax.dot_general(a, b, (((1,), (1,)), ((), ())), preferred_element_type=F32)


def _layer_norm(x, g, b):
    mu = jnp.mean(x, axis=-1, keepdims=True)
    xc = x - mu
    var = jnp.mean(xc * xc, axis=-1, keepdims=True)
    return xc * lax.rsqrt(var + LN_EPS) * g + b


def _sigmoid(x):
    return 1.0 / (1.0 + jnp.exp(-x))


def _load_token_tiles(ref, rows, row0=0):
    return jnp.concatenate(
        [ref[pl.ds(row0 * TILE_SUBLANES + j, rows, stride=TILE_SUBLANES), :] for j in range(TILE_SUBLANES)], axis=1)


def _store_token_tiles(ref, value, row0=0):
    for j in range(TILE_SUBLANES):
        ref[pl.ds(row0 * TILE_SUBLANES + j, value.shape[0], stride=TILE_SUBLANES), :] = (
            value[:, j * LANES:(j + 1) * LANES])


def _token_tile(ref, row):
    return ref.at[pl.ds(pl.multiple_of(row * TILE_SUBLANES, TILE_SUBLANES), TILE_SUBLANES), :]


def _bf16_bits(x):
    return lax.bitcast_convert_type(x.astype(BF16).astype(F32), jnp.uint32)


def _store_packed_tokens(ref, value, row0=0):
    n = value.shape[0]
    for c in range(PACK_SUBLANES):
        lo = _bf16_bits(value[:, c * LANES:(c + 1) * LANES])
        hi = _bf16_bits(value[:, (c + PACK_SUBLANES) * LANES:(c + PACK_SUBLANES + 1) * LANES])
        ref[pl.ds(row0 * PACK_SUBLANES + c, n, stride=PACK_SUBLANES), :] = (
            lax.shift_right_logical(lo, jnp.uint32(16)) | hi)


def _load_packed_tokens(ref, rows, row0=0):
    los, his = [], []
    for c in range(PACK_SUBLANES):
        w = ref[pl.ds(row0 * PACK_SUBLANES + c, rows, stride=PACK_SUBLANES), :]
        los.append(lax.bitcast_convert_type(lax.shift_left(w, jnp.uint32(16)), F32))
        his.append(lax.bitcast_convert_type(w & jnp.uint32(0xFFFF0000), F32))
    return jnp.concatenate(los + his, axis=1)


def _packed_token(ref, row):
    return ref.at[pl.ds(pl.multiple_of(row * PACK_SUBLANES, PACK_SUBLANES), PACK_SUBLANES), :]


def _mixer_body(x, s, first, sinks_ref, w_in_ref, b_in_ref, w_pool_ref, pscale_ref, w_brp_ref, w_bra_ref, w_out_ref,
                ln1g_ref, ln1b_ref, o_ref, op_ref, ubuf, kvbuf, bias_tab, *, layer, tq):
    @pl.when(s == 0)
    def _():
        ubuf[0:POOL_HALO, :] = jnp.zeros((POOL_HALO, POOL_WIDTH), F32)
        kvbuf[0:ATT_BLOCK, :] = jnp.zeros((ATT_BLOCK, 8 * LANES), BF16)

    xb = x.astype(BF16)

    u = _dot(xb, w_in_ref[:, 0:POOL_WIDTH]) + b_in_ref[:, 0:POOL_WIDTH]
    ubuf[POOL_HALO:POOL_HALO + tq, :] = u
    pos = (s * tq + lax.broadcasted_iota(jnp.int32, (tq, 1), 0)).astype(F32)
    mixed_parts = []
    for g, w in enumerate(POOL_WINDOWS):
        sl = slice(g * POOL_GROUP_CH, (g + 1) * POOL_GROUP_CH)
        cur = ubuf[POOL_HALO:POOL_HALO + tq, sl]
        acc = cur
        for j in range(1, w):
            acc = acc + ubuf[POOL_HALO - j:POOL_HALO - j + tq, sl]
        inv_cnt = 1.0 / jnp.minimum(pos + 1.0, float(w))
        d = (acc * inv_cnt - cur).astype(BF16)
        mixed_parts.append(_dot(d, w_pool_ref[g]) * pscale_ref[:, sl])
    mixed = jnp.concatenate(mixed_parts, axis=1).astype(BF16)
    y_pool = _dot(mixed, w_brp_ref[...])
    ubuf[0:POOL_HALO, :] = ubuf[tq:tq + POOL_HALO, :]

    qkv = _dot(xb, w_in_ref[:, QKV_START:GATE_START]) + b_in_ref[:, QKV_START:GATE_START]
    q = (qkv[:, 0:Q_WIDTH] * (HEAD_DIM ** -0.5)).astype(BF16)
    k = qkv[:, Q_WIDTH:Q_WIDTH + KV_WIDTH]
    v = qkv[:, Q_WIDTH + KV_WIDTH:Q_WIDTH + 2 * KV_WIDTH]
    lo = lax.broadcasted_iota(jnp.int32, (tq, LANES), 1) < HEAD_DIM
    k_sw = pltpu.roll(k, HEAD_DIM, axis=1)
    v_sw = pltpu.roll(v, HEAD_DIM, axis=1)
    zero = jnp.zeros((tq, LANES), F32)
    slabs = (
        jnp.where(lo, k, zero), jnp.where(lo, zero, k_sw),
        jnp.where(lo, k_sw, zero), jnp.where(lo, zero, k),
        jnp.where(lo, v, zero), jnp.where(lo, zero, v_sw),
        jnp.where(lo, v_sw, zero), jnp.where(lo, zero, v),
    )
    for i, slab in enumerate(slabs):
        kvbuf[ATT_BLOCK:ATT_BLOCK + tq, i * LANES:(i + 1) * LANES] = slab.astype(BF16)

    @pl.when(first)
    def _():
        qi = lax.broadcasted_iota(jnp.int32, (ATT_BLOCK, 2 * ATT_BLOCK), 0)
        kj = lax.broadcasted_iota(jnp.int32, (ATT_BLOCK, 2 * ATT_BLOCK), 1)
        dist = ATT_BLOCK + qi - kj
        band_ok = (dist >= 0) & (dist < WINDOW)
        distf = dist.astype(F32)
        for i, h in enumerate(SEGMENT_HEADS):
            bias_tab[i * ATT_BLOCK:(i + 1) * ATT_BLOCK, :] = jnp.where(band_ok, -ALIBI_SLOPES[h] * distf, -jnp.inf)

    sink_col = jnp.concatenate([jnp.full((ATT_BLOCK, 1), sinks_ref[layer, h], F32) for h in SEGMENT_HEADS], axis=0)
    key_col = lax.broadcasted_iota(jnp.int32, (1, 2 * ATT_BLOCK), 1)
    o_blocks = []
    for qb in range(tq // ATT_BLOCK):
        r0 = qb * ATT_BLOCK
        first_key_pos = s * tq + r0 - ATT_BLOCK
        score_parts = []
        for hk in range(N_KV_HEADS):
            q_pairs = jnp.concatenate(
                [q[r0:r0 + ATT_BLOCK, (2 * hk + pj) * LANES:(2 * hk + pj + 1) * LANES] for pj in range(2)], axis=0)
            for half in range(2):
                k_slab = kvbuf[r0:r0 + 2 * ATT_BLOCK, (2 * hk + half) * LANES:(2 * hk + half + 1) * LANES]
                score_parts.append(_dot_nt(q_pairs, k_slab))
        sc = jnp.concatenate(score_parts, axis=0) + bias_tab[...]
        sc = jnp.where(key_col + first_key_pos < 0, -jnp.inf, sc)
        m = jnp.maximum(jnp.max(sc, axis=1, keepdims=True), sink_col)
        p = jnp.exp(sc - m).astype(BF16)
        den = _dot(p, jnp.ones((2 * ATT_BLOCK, LANES), BF16)) + jnp.exp(sink_col - m)
        inv_den = 1.0 / den
        o_pairs = []
        for hk in range(N_KV_HEADS):
            pv = None
            for half in range(2):
                rows = (2 * hk + half) * 2 * ATT_BLOCK
                v_slab = kvbuf[r0:r0 + 2 * ATT_BLOCK, (4 + 2 * hk + half) * LANES:(4 + 2 * hk + half + 1) * LANES]
                contrib = _dot(p[rows:rows + 2 * ATT_BLOCK], v_slab) * inv_den[rows:rows + 2 * ATT_BLOCK]
                pv = contrib if pv is None else pv + contrib
            o_pairs += [pv[0:ATT_BLOCK], pv[ATT_BLOCK:2 * ATT_BLOCK]]
        o_blocks.append(jnp.concatenate(o_pairs, axis=1))
    o = jnp.concatenate(o_blocks, axis=0).astype(BF16)
    y_attn = _dot(o, w_bra_ref[...])
    kvbuf[0:ATT_BLOCK, :] = kvbuf[tq:tq + ATT_BLOCK, :]

    gates = _dot(xb, w_in_ref[:, GATE_START:D_IN]) + b_in_ref[:, GATE_START:D_IN]
    merged = _sigmoid(gates[:, 0:D_MODEL]) * y_pool + _sigmoid(gates[:, D_MODEL:2 * D_MODEL]) * y_attn
    mix = _dot(merged.astype(BF16), w_out_ref[...])
    x1 = _layer_norm(ALPHA * x + mix, ln1g_ref[...], ln1b_ref[...])
    _store_token_tiles(o_ref, x1)
    _store_packed_tokens(op_ref, x1)


def _first_mixer_kernel(sinks_ref, x_ref, ln0g_ref, ln0b_ref, *refs, layer, tq, n_s):
    t = pl.program_id(0)
    x = _layer_norm(x_ref[...], ln0g_ref[...], ln0b_ref[...])
    _mixer_body(x, lax.rem(t, n_s), t == 0, sinks_ref, *refs, layer=layer, tq=tq)


def _combine_mixer_kernel(sinks_ref, pos_hbm, y_hbm, xprev_ref, gate_ref, wsg_ref, wsu_ref, wsd_ref, ln2g_ref, ln2b_ref,
                          w_in_ref, b_in_ref, w_pool_ref, pscale_ref, w_brp_ref, w_bra_ref, w_out_ref, ln1g_ref,
                          ln1b_ref, o_ref, op_ref, ubuf, kvbuf, bias_tab, idx_smem, ybuf, sem_idx, sem_y,
                          *, layer, tq, n_s, n_tiles):
    h = pl.program_id(0)
    gslot = lax.rem(h, 2)
    rows_per_slot = TOP_K * tq
    tiles_per_step = tq // TOKEN_TILE

    def idx_copy(tile, sl):
        return pltpu.make_async_copy(pos_hbm.at[tile], idx_smem.at[sl], sem_idx.at[sl])

    @pl.when(h == 0)
    def _():
        idx_copy(0, 0).start()

    @pl.when(h < n_tiles)
    def _():
        idx_copy(h, gslot).wait()
        for ti in range(tiles_per_step):
            def fetch(k, j, ti=ti):
                pltpu.make_async_copy(
                    _packed_token(y_hbm, idx_smem[gslot, ti * TOP_K + k, j]),
                    _packed_token(ybuf, gslot * rows_per_slot + k * tq + ti * TOKEN_TILE + j), sem_y.at[gslot]
                ).start(priority=j % 2)
            _for_each_assignment(fetch)

    @pl.when(h + 1 < n_tiles)
    def _():
        idx_copy(h + 1, 1 - gslot).start()

    @pl.when(h > 0)
    def _():
        t = h - 1
        base = (1 - gslot) * rows_per_slot
        slot_rows = ybuf.at[pl.ds(pl.multiple_of(base * PACK_SUBLANES, PACK_SUBLANES), rows_per_slot * PACK_SUBLANES), :]
        pltpu.make_async_copy(slot_rows, slot_rows, sem_y.at[1 - gslot]).wait()
        x1 = _load_token_tiles(xprev_ref, tq)
        x1b = x1.astype(BF16)
        g = _dot(x1b, wsg_ref[...])
        a = (g * _sigmoid(g) * _dot(x1b, wsu_ref[...])).astype(BF16)
        shared = _dot(a, wsd_ref[...])
        gate = gate_ref[...]
        routed = None
        for k in range(TOP_K):
            term = gate[:, k:k + 1] * _load_packed_tokens(ybuf, tq, base + k * tq)
            routed = term if routed is None else routed + term
        x = _layer_norm(ALPHA * x1 + (routed + shared), ln2g_ref[...], ln2b_ref[...])
        _mixer_body(x, lax.rem(t, n_s), t == 0, sinks_ref, w_in_ref, b_in_ref, w_pool_ref, pscale_ref, w_brp_ref,
                    w_bra_ref, w_out_ref, ln1g_ref, ln1b_ref, o_ref, op_ref, ubuf, kvbuf, bias_tab, layer=layer, tq=tq)


def _mixer_specs(layer):
    per_layer = lambda *shape: pl.BlockSpec((None,) + shape, lambda h: (layer,) + (0,) * len(shape))
    return [
        per_layer(D_MODEL, D_IN), per_layer(1, D_IN), per_layer(POOL_GROUPS, POOL_GROUP_CH, POOL_GROUP_CH),
        per_layer(1, POOL_WIDTH), per_layer(POOL_WIDTH, D_MODEL), per_layer(Q_WIDTH, D_MODEL),
        per_layer(D_MODEL, D_MODEL), per_layer(1, D_MODEL), per_layer(1, D_MODEL),
    ]


def _mixer_outputs(T, tq, tile_of_step):
    out_specs = [
        pl.BlockSpec((tq * TILE_SUBLANES, LANES), lambda h: (tile_of_step(h), 0)),
        pl.BlockSpec((tq * PACK_SUBLANES, LANES), lambda h: (tile_of_step(h), 0)),
    ]
    out_shape = [
        jax.ShapeDtypeStruct((T * TILE_SUBLANES, LANES), F32),
        jax.ShapeDtypeStruct((T * PACK_SUBLANES, LANES), jnp.uint32),
    ]
    scratch = [
        pltpu.VMEM((POOL_HALO + tq, POOL_WIDTH), F32),
        pltpu.VMEM((ATT_BLOCK + tq, 8 * LANES), BF16),
        pltpu.VMEM((N_Q_HEADS * ATT_BLOCK, 2 * ATT_BLOCK), F32),
    ]
    return out_specs, out_shape, scratch


def _first_mixer(x, sinks, ln0_g, ln0_b, mixer_weights):
    B, S, D = x.shape
    tq = MIXER_TILE
    n_s = S // tq
    out_specs, out_shape, scratch = _mixer_outputs(B * S, tq, lambda h: h)
    return pl.pallas_call(
        functools.partial(_first_mixer_kernel, layer=0, tq=tq, n_s=n_s),
        grid=(B * n_s,),
        in_specs=[
            pl.BlockSpec(memory_space=pltpu.SMEM),
            pl.BlockSpec((None, tq, D), lambda h: (h // n_s, h % n_s, 0)),
            pl.BlockSpec((1, D), lambda h: (0, 0)),
            pl.BlockSpec((1, D), lambda h: (0, 0)),
        ] + _mixer_specs(0),
        out_specs=out_specs,
        out_shape=out_shape,
        scratch_shapes=scratch,
        compiler_params=pltpu.CompilerParams(dimension_semantics=("arbitrary",), vmem_limit_bytes=VMEM_LIMIT),
        name="mixer_l0",
    )(sinks, x, ln0_g, ln0_b, *mixer_weights)


def _combine_mixer(pos_tiles, y_sorted, x1_prev, gate_t, shared_weights, sinks, mixer_weights, *, layer, n_s):
    tq = MIXER_TILE
    T = x1_prev.shape[0] // TILE_SUBLANES
    n_tiles = T // tq
    tiles_per_step = tq // TOKEN_TILE
    pos_steps = pos_tiles.reshape(n_tiles, tiles_per_step * TOP_K, TOKEN_TILE)
    tile_of_step = lambda h: jnp.maximum(h - 1, 0)
    prev = lambda *shape: pl.BlockSpec((None,) + shape, lambda h: (layer - 1,) + (0,) * len(shape))
    out_specs, out_shape, scratch = _mixer_outputs(T, tq, tile_of_step)
    return pl.pallas_call(
        functools.partial(_combine_mixer_kernel, layer=layer, tq=tq, n_s=n_s, n_tiles=n_tiles),
        grid=(n_tiles + 1,),
        in_specs=[
            pl.BlockSpec(memory_space=pltpu.SMEM),
            pl.BlockSpec(memory_space=pl.ANY),
            pl.BlockSpec(memory_space=pl.ANY),
            pl.BlockSpec((tq * TILE_SUBLANES, LANES), lambda h: (tile_of_step(h), 0)),
            pl.BlockSpec((tq, TOP_K), lambda h: (tile_of_step(h), 0)),
            prev(D_MODEL, EXPERT_FF), prev(D_MODEL, EXPERT_FF), prev(EXPERT_FF, D_MODEL), prev(1, D_MODEL), prev(1, D_MODEL),
        ] + _mixer_specs(layer),
        out_specs=out_specs,
        out_shape=out_shape,
        scratch_shapes=scratch + [
            pltpu.SMEM((2, tiles_per_step * TOP_K, TOKEN_TILE), jnp.int32),
            pltpu.VMEM((2 * TOP_K * tq * PACK_SUBLANES, LANES), jnp.uint32),
            pltpu.SemaphoreType.DMA((2,)),
            pltpu.SemaphoreType.DMA((2,)),
        ],
        compiler_params=pltpu.CompilerParams(dimension_semantics=("arbitrary",), vmem_limit_bytes=FUSED_VMEM_LIMIT),
        name=f"combine_mixer_l{layer}",
    )(sinks, pos_steps, y_sorted, x1_prev, gate_t, *shared_weights, *mixer_weights)


def _first_index_of_max(vals, iota, n):
    m = jnp.max(vals, axis=0, keepdims=True)
    idx = jnp.min(jnp.where(vals == m, iota, n), axis=0, keepdims=True)
    return m, idx


def _route(xb, wr_ref, bias_ref, ek_ref, rk_ref, gk_ref, cnt_ref, carry):
    tr = xb.shape[0]
    logits = _dot_nt(wr_ref[...], xb)
    scores = _sigmoid(logits)
    biased = scores + bias_ref[...]
    neg_inf = -jnp.inf

    io8 = lax.broadcasted_iota(jnp.int32, (EXPERTS_PER_GROUP, tr), 0)
    group_rows = []
    for g in range(N_EXPERT_GROUPS):
        blk = biased[g * EXPERTS_PER_GROUP:(g + 1) * EXPERTS_PER_GROUP]
        m1, i1 = _first_index_of_max(blk, io8, EXPERTS_PER_GROUP)
        m2 = jnp.max(jnp.where(io8 == i1, neg_inf, blk), axis=0, keepdims=True)
        group_rows.append(m1 + m2)
    gscore = jnp.concatenate(group_rows, axis=0)
    iog = lax.broadcasted_iota(jnp.int32, (N_EXPERT_GROUPS, tr), 0)
    keep = jnp.zeros((N_EXPERT_GROUPS, tr), F32)
    for _ in range(TOPK_GROUPS):
        _, gi = _first_index_of_max(gscore, iog, N_EXPERT_GROUPS)
        hit = iog == gi
        keep = jnp.where(hit, 1.0, keep)
        gscore = jnp.where(hit, neg_inf, gscore)
    masked = jnp.concatenate(
        [jnp.where(keep[g:g + 1] > 0.0, biased[g * EXPERTS_PER_GROUP:(g + 1) * EXPERTS_PER_GROUP], neg_inf)
         for g in range(N_EXPERT_GROUPS)], axis=0)

    ioe = lax.broadcasted_iota(jnp.int32, (N_EXPERTS, tr), 0)
    sel = jnp.zeros((N_EXPERTS, tr), F32)
    e_rows, s_rows, hits = [], [], []
    for _ in range(TOP_K):
        _, ei = _first_index_of_max(masked, ioe, N_EXPERTS)
        hit = ioe == ei
        sel = jnp.where(hit, 1.0, sel)
        masked = jnp.where(hit, neg_inf, masked)
        e_rows.append(ei)
        hits.append(hit)
        s_rows.append(jnp.sum(jnp.where(hit, scores, 0.0), axis=0, keepdims=True))
    sel_scores = jnp.concatenate(s_rows, axis=0)
    gk_ref[...] = sel_scores / jnp.sum(sel_scores, axis=0, keepdims=True) * ROUTED_SCALE
    ek_ref[...] = jnp.concatenate(e_rows, axis=0)

    before = (lax.broadcasted_iota(jnp.int32, (tr, tr), 0) < lax.broadcasted_iota(jnp.int32, (tr, tr), 1))
    prefix = _dot(sel.astype(BF16), jnp.where(before, 1.0, 0.0).astype(BF16))
    rank_full = prefix + carry[...]
    rk_ref[...] = jnp.concatenate(
        [jnp.sum(jnp.where(hit, rank_full, 0.0), axis=0, keepdims=True) for hit in hits], axis=0).astype(jnp.int32)
    total = carry[...] + jnp.sum(sel, axis=1, keepdims=True)
    carry[...] = total
    cnt_ref[...] = jnp.broadcast_to(total, (N_EXPERTS, LANES))


def _router_kernel(x_ref, wr_ref, bias_ref, ek_ref, rk_ref, gk_ref, cnt_ref, carry, *, tr):
    @pl.when(pl.program_id(0) == 0)
    def _():
        carry[...] = jnp.zeros_like(carry)

    _route(_load_token_tiles(x_ref, tr).astype(BF16), wr_ref, bias_ref, ek_ref, rk_ref, gk_ref, cnt_ref, carry)


def _router(x1, w_router_t, bias_col, *, layer):
    T = x1.shape[0] // TILE_SUBLANES
    tr = ROUTER_TILE
    row_spec = pl.BlockSpec((TOP_K, tr), lambda i: (0, i))
    return pl.pallas_call(
        functools.partial(_router_kernel, tr=tr),
        grid=(T // tr,),
        in_specs=[
            pl.BlockSpec((tr * TILE_SUBLANES, LANES), lambda i: (i, 0)),
            pl.BlockSpec((None, N_EXPERTS, D_MODEL), lambda i: (layer, 0, 0)),
            pl.BlockSpec((None, N_EXPERTS, 1), lambda i: (layer, 0, 0)),
        ],
        out_specs=[row_spec, row_spec, row_spec, pl.BlockSpec((N_EXPERTS, LANES), lambda i: (0, 0))],
        out_shape=[
            jax.ShapeDtypeStruct((TOP_K, T), jnp.int32),
            jax.ShapeDtypeStruct((TOP_K, T), jnp.int32),
            jax.ShapeDtypeStruct((TOP_K, T), F32),
            jax.ShapeDtypeStruct((N_EXPERTS, LANES), F32),
        ],
        scratch_shapes=[pltpu.VMEM((N_EXPERTS, 1), F32)],
        compiler_params=pltpu.CompilerParams(dimension_semantics=("arbitrary",), vmem_limit_bytes=VMEM_LIMIT),
        name=f"router_l{layer}",
    )(x1, w_router_t, bias_col)


def _for_each_assignment(fn):
    for k in range(TOP_K):
        for j in range(TOKEN_TILE):
            fn(k, j)


def _dispatch_kernel(fill_ref, nu_ref, pos_hbm, x_ref, xs_hbm, idx_smem, zbuf, sem_idx, sem_out, sem_fill,
                     *, n_tiles, n_blocks):
    i = pl.program_id(0)
    slot = lax.rem(i, 2)

    def idx_copy(tile, sl):
        return pltpu.make_async_copy(pos_hbm.at[tile], idx_smem.at[sl], sem_idx.at[sl])

    def fill_copy(row0, rows):
        n = rows * PACK_SUBLANES
        return pltpu.make_async_copy(
            zbuf.at[pl.ds(0, n), :], xs_hbm.at[pl.ds(pl.multiple_of(row0 * PACK_SUBLANES, PACK_SUBLANES), n), :],
            sem_fill)

    def pad_fill(e, wait):
        row0 = fill_ref[e]
        n_pad = (0 - row0) & (EXPERT_BLOCK - 1)
        piece = EXPERT_BLOCK // 2
        while piece >= 1:
            has = (n_pad & piece) != 0

            @pl.when(has)
            def _(row0=row0, piece=piece):
                fill_copy(0 if wait else row0, piece).wait() if wait else fill_copy(row0, piece).start()
            row0 = row0 + jnp.where(has, piece, 0)
            piece //= 2

    @pl.when(i == 0)
    def _():
        idx_copy(0, 0).start()
        zbuf[...] = jnp.zeros_like(zbuf)
        lax.fori_loop(0, N_EXPERTS, lambda e, c: (pad_fill(e, False), c)[1], 0)
        lax.fori_loop(nu_ref[0], n_blocks, lambda b, c: (fill_copy(b * EXPERT_BLOCK, EXPERT_BLOCK).start(), c)[1], 0)
        lax.fori_loop(0, N_EXPERTS, lambda e, c: (pad_fill(e, True), c)[1], 0)
        lax.fori_loop(nu_ref[0], n_blocks, lambda b, c: (fill_copy(0, EXPERT_BLOCK).wait(), c)[1], 0)

    idx_copy(i, slot).wait()

    @pl.when(i + 1 < n_tiles)
    def _():
        idx_copy(i + 1, 1 - slot).start()

    def send(k, j):
        pltpu.make_async_copy(_packed_token(x_ref, j), _packed_token(xs_hbm, idx_smem[slot, k, j]), sem_out
                              ).start(priority=j % 2)
    _for_each_assignment(send)

    for _ in range(TOP_K):
        pltpu.make_async_copy(x_ref, x_ref, sem_out).wait()


def _dispatch(fill_start, n_used, pos_tiles, x1p, n_sorted_rows, *, layer):
    T = x1p.shape[0] // PACK_SUBLANES
    n_tiles = T // TOKEN_TILE
    return pl.pallas_call(
        functools.partial(_dispatch_kernel, n_tiles=n_tiles, n_blocks=n_sorted_rows // EXPERT_BLOCK),
        grid_spec=pltpu.PrefetchScalarGridSpec(
            num_scalar_prefetch=2,
            grid=(n_tiles,),
            in_specs=[
                pl.BlockSpec(memory_space=pl.ANY),
                pl.BlockSpec((TOKEN_TILE * PACK_SUBLANES, LANES), lambda i, fill, nu: (i, 0)),
            ],
            out_specs=pl.BlockSpec(memory_space=pl.ANY),
            scratch_shapes=[
                pltpu.SMEM((2, TOP_K, TOKEN_TILE), jnp.int32),
                pltpu.VMEM((EXPERT_BLOCK * PACK_SUBLANES, LANES), jnp.uint32),
                pltpu.SemaphoreType.DMA((2,)),
                pltpu.SemaphoreType.DMA(()),
                pltpu.SemaphoreType.DMA(()),
            ],
        ),
        out_shape=jax.ShapeDtypeStruct((n_sorted_rows * PACK_SUBLANES, LANES), jnp.uint32),
        compiler_params=pltpu.CompilerParams(dimension_semantics=("arbitrary",), vmem_limit_bytes=VMEM_LIMIT),
        name=f"dispatch_l{layer}",
    )(fill_start, n_used, pos_tiles, x1p)


def _expert_kernel(be_ref, nu_ref, xs_ref, wg_ref, wu_ref, wd_ref, y_ref, wgu_bf, wd_bf):
    i = pl.program_id(0)

    @pl.when(i < nu_ref[0])
    def _():
        new_expert = (i == 0) | (be_ref[i] != be_ref[jnp.maximum(i - 1, 0)])

        @pl.when(new_expert)
        def _():
            wgu_bf[:, 0:EXPERT_FF] = wg_ref[...].astype(BF16)
            wgu_bf[:, EXPERT_FF:2 * EXPERT_FF] = wu_ref[...].astype(BF16)
            wd_bf[...] = wd_ref[...].astype(BF16)

        for r0 in range(0, EXPERT_BLOCK, EXPERT_CHUNK):
            h = _dot(_load_packed_tokens(xs_ref, EXPERT_CHUNK, r0).astype(BF16), wgu_bf[...])
            g = h[:, 0:EXPERT_FF]
            a = (g * _sigmoid(g) * h[:, EXPERT_FF:2 * EXPERT_FF]).astype(BF16)
            _store_packed_tokens(y_ref, _dot(a, wd_bf[...]), r0)

    @pl.when(i >= nu_ref[0])
    def _():
        y_ref[...] = jnp.zeros_like(y_ref)


def _experts(blk_expert, n_used, xs, w_gate, w_up, w_down, *, layer):
    nb = blk_expert.shape[0]
    blk = EXPERT_BLOCK
    used = lambda i, nu: jnp.minimum(i, nu[0] - 1)
    w_spec = lambda *shape: pl.BlockSpec(
        (None,) + shape, lambda i, be, nu: (layer * N_EXPERTS + be[used(i, nu)], 0, 0))
    in_row_spec = pl.BlockSpec((blk * PACK_SUBLANES, LANES), lambda i, be, nu: (used(i, nu), 0))
    return pl.pallas_call(
        _expert_kernel,
        grid_spec=pltpu.PrefetchScalarGridSpec(
            num_scalar_prefetch=2,
            grid=(nb,),
            in_specs=[in_row_spec, w_spec(D_MODEL, EXPERT_FF), w_spec(D_MODEL, EXPERT_FF), w_spec(EXPERT_FF, D_MODEL)],
            out_specs=pl.BlockSpec((blk * PACK_SUBLANES, LANES), lambda i, be, nu: (i, 0)),
            scratch_shapes=[
                pltpu.VMEM((D_MODEL, 2 * EXPERT_FF), BF16),
                pltpu.VMEM((EXPERT_FF, D_MODEL), BF16),
            ],
        ),
        out_shape=jax.ShapeDtypeStruct((nb * blk * PACK_SUBLANES, LANES), jnp.uint32),
        compiler_params=pltpu.CompilerParams(dimension_semantics=("arbitrary",), vmem_limit_bytes=VMEM_LIMIT),
        name=f"experts_l{layer}",
    )(blk_expert, n_used, xs, w_gate, w_up, w_down)


def _combine_kernel(pos_hbm, y_hbm, x_ref, gate_ref, wsg_ref, wsu_ref, wsd_ref, ln2g_ref, ln2b_ref, o_ref,
                    idx_smem, ybuf, sem_idx, sem_y, *, n_tiles):
    i = pl.program_id(0)
    slot = lax.rem(i, 2)
    nslot = 1 - slot
    rows_per_slot = TOP_K * TOKEN_TILE

    def idx_copy(tile, sl):
        return pltpu.make_async_copy(pos_hbm.at[tile], idx_smem.at[sl], sem_idx.at[sl])

    def start_gather(sl):
        def fetch(k, j):
            pltpu.make_async_copy(_packed_token(y_hbm, idx_smem[sl, k, j]),
                                  _packed_token(ybuf, sl * rows_per_slot + k * TOKEN_TILE + j), sem_y.at[sl]
                                  ).start(priority=j % 2)
        _for_each_assignment(fetch)

    @pl.when(i == 0)
    def _():
        idx_copy(0, 0).start()
        idx_copy(0, 0).wait()
        start_gather(0)
        if n_tiles > 1:
            idx_copy(1, 1).start()

    @pl.when(i + 1 < n_tiles)
    def _():
        idx_copy(i + 1, nslot).wait()
        start_gather(nslot)

    @pl.when(i + 2 < n_tiles)
    def _():
        idx_copy(i + 2, slot).start()

    x = _load_token_tiles(x_ref, TOKEN_TILE)
    xb = x.astype(BF16)
    g = _dot(xb, wsg_ref[...])
    a = (g * _sigmoid(g) * _dot(xb, wsu_ref[...])).astype(BF16)
    shared = _dot(a, wsd_ref[...])

    base = slot * rows_per_slot
    slot_rows = ybuf.at[pl.ds(pl.multiple_of(base * PACK_SUBLANES, PACK_SUBLANES), rows_per_slot * PACK_SUBLANES), :]
    pltpu.make_async_copy(slot_rows, slot_rows, sem_y.at[slot]).wait()
    gate = gate_ref[...]
    routed = None
    for k in range(TOP_K):
        term = gate[:, k:k + 1] * _load_packed_tokens(ybuf, TOKEN_TILE, base + k * TOKEN_TILE)
        routed = term if routed is None else routed + term
    o_ref[...] = _layer_norm(ALPHA * x + (routed + shared), ln2g_ref[...], ln2b_ref[...])


def _combine(pos_tiles, y_sorted, x1, gate_t, w_sg, w_su, w_sd, ln2_g, ln2_b, *, layer):
    T = x1.shape[0] // TILE_SUBLANES
    tc = TOKEN_TILE
    n_tiles = T // tc
    per_layer = lambda *shape: pl.BlockSpec((None,) + shape, lambda i: (layer,) + (0,) * len(shape))
    return pl.pallas_call(
        functools.partial(_combine_kernel, n_tiles=n_tiles),
        grid=(n_tiles,),
        in_specs=[
            pl.BlockSpec(memory_space=pl.ANY),
            pl.BlockSpec(memory_space=pl.ANY),
            pl.BlockSpec((tc * TILE_SUBLANES, LANES), lambda i: (i, 0)),
            pl.BlockSpec((tc, TOP_K), lambda i: (i, 0)),
            per_layer(D_MODEL, EXPERT_FF),
            per_layer(D_MODEL, EXPERT_FF),
            per_layer(EXPERT_FF, D_MODEL),
            per_layer(1, D_MODEL),
            per_layer(1, D_MODEL),
        ],
        out_specs=pl.BlockSpec((tc, D_MODEL), lambda i: (i, 0)),
        out_shape=jax.ShapeDtypeStruct((T, D_MODEL), F32),
        scratch_shapes=[
            pltpu.SMEM((2, TOP_K, TOKEN_TILE), jnp.int32),
            pltpu.VMEM((2 * TOP_K * tc * PACK_SUBLANES, LANES), jnp.uint32),
            pltpu.SemaphoreType.DMA((2,)),
            pltpu.SemaphoreType.DMA((2,)),
        ],
        compiler_params=pltpu.CompilerParams(dimension_semantics=("arbitrary",), vmem_limit_bytes=VMEM_LIMIT),
        name=f"combine_l{layer}",
    )(pos_tiles, y_sorted, x1, gate_t, w_sg, w_su, w_sd, ln2_g, ln2_b)


def _dispatch_plan(ek, rk, counts_f):
    T = ek.shape[1]
    blk = EXPERT_BLOCK
    nb = (T * TOP_K) // blk + N_EXPERTS
    experts = jnp.arange(N_EXPERTS, dtype=jnp.int32)
    counts = counts_f[:, 0].astype(jnp.int32)
    pcounts = (counts + blk - 1) // blk * blk
    pends = jnp.sum(jnp.where(experts[None, :] <= experts[:, None], pcounts[None, :], 0), axis=1)
    pstarts = pends - pcounts
    pos = jnp.sum(jnp.where(ek[None] == experts[:, None, None], pstarts[:, None, None], 0), axis=0) + rk
    n_used = (pends[-1] // blk).reshape(1)
    block_row0 = jnp.arange(nb, dtype=jnp.int32) * blk
    blk_expert = jnp.minimum(jnp.sum((pends[None, :] <= block_row0[:, None]).astype(jnp.int32), axis=1), N_EXPERTS - 1)
    fill_start = pstarts + counts
    pos_tiles = pos.reshape(TOP_K, T // TOKEN_TILE, TOKEN_TILE).transpose(1, 0, 2)
    return blk_expert, n_used, fill_start, pos_tiles, nb * blk


def kernel(x, ln0_g, ln0_b, w_in, b_in, w_pool, pool_scale, attn_sinks, w_br_pool, w_br_attn, w_out, ln1_g, ln1_b,
           w_router, router_bias, w_exp_gate, w_exp_up, w_exp_down, w_sh_gate, w_sh_up, w_sh_down, ln2_g, ln2_b):
    B, S, D = x.shape
    depth = w_in.shape[0]
    row = lambda a: a.reshape(a.shape[0], 1, a.shape[1])
    w_in_b, w_pool_b = w_in.astype(BF16), w_pool.astype(BF16)
    w_brp_b, w_bra_b, w_out_b = w_br_pool.astype(BF16), w_br_attn.astype(BF16), w_out.astype(BF16)
    w_router_t = jnp.swapaxes(w_router, 1, 2).astype(BF16)
    bias_col = router_bias.reshape(depth, N_EXPERTS, 1)
    w_sg_b, w_su_b, w_sd_b = w_sh_gate.astype(BF16), w_sh_up.astype(BF16), w_sh_down.astype(BF16)
    w_eg = w_exp_gate.reshape(depth * N_EXPERTS, D, EXPERT_FF)
    w_eu = w_exp_up.reshape(depth * N_EXPERTS, D, EXPERT_FF)
    w_ed = w_exp_down.reshape(depth * N_EXPERTS, EXPERT_FF, D)
    ln0_g2, ln0_b2 = ln0_g.reshape(1, D), ln0_b.reshape(1, D)

    mixer_weights = (w_in_b, row(b_in), w_pool_b, row(pool_scale), w_brp_b, w_bra_b, w_out_b, row(ln1_g), row(ln1_b))
    shared_weights = (w_sg_b, w_su_b, w_sd_b, row(ln2_g), row(ln2_b))
    n_s = S // MIXER_TILE
    x1, x1p = _first_mixer(x, attn_sinks, ln0_g2, ln0_b2, mixer_weights)
    for l in range(depth):
        ek, rk, gk, counts_f = _router(x1, w_router_t, bias_col, layer=l)
        blk_expert, n_used, fill_start, pos_tiles, n_sorted = _dispatch_plan(ek, rk, counts_f)
        xs = _dispatch(fill_start, n_used, pos_tiles, x1p, n_sorted + EXPERT_BLOCK, layer=l)
        y_sorted = _experts(blk_expert, n_used, xs, w_eg, w_eu, w_ed, layer=l)
        if l + 1 < depth:
            x1, x1p = _combine_mixer(pos_tiles, y_sorted, x1, gk.T, shared_weights, attn_sinks, mixer_weights,
                                     layer=l + 1, n_s=n_s)
    return _combine(pos_tiles, y_sorted, x1, gk.T, *shared_weights, layer=depth - 1).reshape(B, S, D)
```

```python
import functools

import jax
import jax.numpy as jnp
from jax import lax
from jax.experimental import pallas as pl
from jax.experimental.pallas import tpu as pltpu

D_MODEL = 1024
DEPTH = 4
POOL_GROUPS = 4
POOL_GROUP_CH = 128
POOL_WIDTH = POOL_GROUPS * POOL_GROUP_CH
POOL_WINDOWS = (2, 4, 8, 16)
POOL_HALO = 16
N_Q_HEADS = 8
N_KV_HEADS = 2
HEAD_DIM = 64
Q_WIDTH = N_Q_HEADS * HEAD_DIM
KV_WIDTH = N_KV_HEADS * HEAD_DIM
WINDOW = 128
ATT_BLOCK = 128
D_IN = POOL_WIDTH + Q_WIDTH + 2 * KV_WIDTH + 2 * D_MODEL
QKV_START = POOL_WIDTH
GATE_START = POOL_WIDTH + Q_WIDTH + 2 * KV_WIDTH
N_EXPERTS = 64
EXPERT_FF = 256
TOP_K = 8
N_EXPERT_GROUPS = 8
EXPERTS_PER_GROUP = N_EXPERTS // N_EXPERT_GROUPS
TOPK_GROUPS = 4
ROUTED_SCALE = 2.5
ALPHA = (2.0 * DEPTH) ** 0.25
LN_EPS = 1e-5
SEGMENT_HEADS = (0, 2, 1, 3, 4, 6, 5, 7)
ALIBI_SLOPES = tuple(float(2.0 ** (-8.0 * h / N_Q_HEADS)) for h in range(1, N_Q_HEADS + 1))

LANES = 128
SUBLANES = 8
TILE_SUBLANES = D_MODEL // LANES
PACK_SUBLANES = TILE_SUBLANES // 2
MIXER_TILE = 256
ROUTER_TILE = 1024
EXPERT_BLOCK = 512
EXPERT_CHUNK = 512
EXPERT_STEP_BLOCKS = 2
TOKEN_TILE = 128
VMEM_LIMIT = 48 * 1024 * 1024
FUSED_VMEM_LIMIT = 56 * 1024 * 1024

BF16 = jnp.bfloat16
F32 = jnp.float32

assert TILE_SUBLANES == SUBLANES and TOP_K == SUBLANES and TOKEN_TILE == LANES


def _dot(a, b):
    return jnp.dot(a, b, preferred_element_type=F32)


def _dot_nt(a, b):
    return lax.dot_general(a, b, (((1,), (1,)), ((), ())), preferred_element_type=F32)


def _layer_norm(x, g, b):
    mu = jnp.mean(x, axis=-1, keepdims=True)
    xc = x - mu
    var = jnp.mean(xc * xc, axis=-1, keepdims=True)
    return xc * lax.rsqrt(var + LN_EPS) * g + b


def _sigmoid(x):
    return 0.5 * jnp.tanh(0.5 * x) + 0.5


def _load_token_tiles(ref, rows, row0=0):
    return jnp.concatenate(
        [ref[pl.ds(row0 * TILE_SUBLANES + j, rows, stride=TILE_SUBLANES), :] for j in range(TILE_SUBLANES)], axis=1)


def _store_token_tiles(ref, value, row0=0):
    for j in range(TILE_SUBLANES):
        ref[pl.ds(row0 * TILE_SUBLANES + j, value.shape[0], stride=TILE_SUBLANES), :] = (
            value[:, j * LANES:(j + 1) * LANES])


def _token_tile(ref, row):
    return ref.at[pl.ds(pl.multiple_of(row * TILE_SUBLANES, TILE_SUBLANES), TILE_SUBLANES), :]


def _bf16_bits(x):
    return lax.bitcast_convert_type(x.astype(BF16).astype(F32), jnp.uint32)


def _store_packed_tokens(ref, value, row0=0):
    n = value.shape[0]
    for c in range(PACK_SUBLANES):
        lo = _bf16_bits(value[:, c * LANES:(c + 1) * LANES])
        hi = _bf16_bits(value[:, (c + PACK_SUBLANES) * LANES:(c + PACK_SUBLANES + 1) * LANES])
        ref[pl.ds(row0 * PACK_SUBLANES + c, n, stride=PACK_SUBLANES), :] = (
            lax.shift_right_logical(lo, jnp.uint32(16)) | hi)


def _load_packed_tokens(ref, rows, row0=0):
    los, his = [], []
    for c in range(PACK_SUBLANES):
        w = ref[pl.ds(row0 * PACK_SUBLANES + c, rows, stride=PACK_SUBLANES), :]
        los.append(lax.bitcast_convert_type(lax.shift_left(w, jnp.uint32(16)), F32))
        his.append(lax.bitcast_convert_type(w & jnp.uint32(0xFFFF0000), F32))
    return jnp.concatenate(los + his, axis=1)


def _packed_token(ref, row):
    return ref.at[pl.ds(pl.multiple_of(row * PACK_SUBLANES, PACK_SUBLANES), PACK_SUBLANES), :]


def _mixer_body(x, s, first, sinks_ref, w_in_ref, b_in_ref, w_pool_ref, pscale_ref, w_brp_ref, w_bra_ref, w_out_ref,
                ln1g_ref, ln1b_ref, o_ref, op_ref, ubuf, kvbuf, bias_tab, *, layer, tq):
    @pl.when(s == 0)
    def _():
        ubuf[0:POOL_HALO, :] = jnp.zeros((POOL_HALO, POOL_WIDTH), F32)
        kvbuf[0:ATT_BLOCK, :] = jnp.zeros((ATT_BLOCK, 8 * LANES), BF16)

    xb = x.astype(BF16)

    u = _dot(xb, w_in_ref[:, 0:POOL_WIDTH]) + b_in_ref[:, 0:POOL_WIDTH]
    ubuf[POOL_HALO:POOL_HALO + tq, :] = u
    pos = (s * tq + lax.broadcasted_iota(jnp.int32, (tq, 1), 0)).astype(F32)
    mixed_parts = []
    for g, w in enumerate(POOL_WINDOWS):
        sl = slice(g * POOL_GROUP_CH, (g + 1) * POOL_GROUP_CH)
        cur = ubuf[POOL_HALO:POOL_HALO + tq, sl]
        acc = cur
        for j in range(1, w):
            acc = acc + ubuf[POOL_HALO - j:POOL_HALO - j + tq, sl]
        inv_cnt = 1.0 / jnp.minimum(pos + 1.0, float(w))
        d = (acc * inv_cnt - cur).astype(BF16)
        mixed_parts.append(_dot(d, w_pool_ref[g]) * pscale_ref[:, sl])
    mixed = jnp.concatenate(mixed_parts, axis=1).astype(BF16)
    y_pool = _dot(mixed, w_brp_ref[...])
    ubuf[0:POOL_HALO, :] = ubuf[tq:tq + POOL_HALO, :]

    qkv = _dot(xb, w_in_ref[:, QKV_START:GATE_START]) + b_in_ref[:, QKV_START:GATE_START]
    q = (qkv[:, 0:Q_WIDTH] * (HEAD_DIM ** -0.5)).astype(BF16)
    k = qkv[:, Q_WIDTH:Q_WIDTH + KV_WIDTH]
    v = qkv[:, Q_WIDTH + KV_WIDTH:Q_WIDTH + 2 * KV_WIDTH]
    lo = lax.broadcasted_iota(jnp.int32, (tq, LANES), 1) < HEAD_DIM
    k_sw = pltpu.roll(k, HEAD_DIM, axis=1)
    v_sw = pltpu.roll(v, HEAD_DIM, axis=1)
    zero = jnp.zeros((tq, LANES), F32)
    slabs = (
        jnp.where(lo, k, zero), jnp.where(lo, zero, k_sw),
        jnp.where(lo, k_sw, zero), jnp.where(lo, zero, k),
        jnp.where(lo, v, zero), jnp.where(lo, zero, v_sw),
        jnp.where(lo, v_sw, zero), jnp.where(lo, zero, v),
    )
    for i, slab in enumerate(slabs):
        kvbuf[ATT_BLOCK:ATT_BLOCK + tq, i * LANES:(i + 1) * LANES] = slab.astype(BF16)

    @pl.when(first)
    def _():
        qi = lax.broadcasted_iota(jnp.int32, (ATT_BLOCK, 2 * ATT_BLOCK), 0)
        kj = lax.broadcasted_iota(jnp.int32, (ATT_BLOCK, 2 * ATT_BLOCK), 1)
        dist = ATT_BLOCK + qi - kj
        band_ok = (dist >= 0) & (dist < WINDOW)
        distf = dist.astype(F32)
        for i, h in enumerate(SEGMENT_HEADS):
            bias_tab[i * ATT_BLOCK:(i + 1) * ATT_BLOCK, :] = jnp.where(band_ok, -ALIBI_SLOPES[h] * distf, -jnp.inf)

    sink_col = jnp.concatenate([jnp.full((ATT_BLOCK, 1), sinks_ref[layer, h], F32) for h in SEGMENT_HEADS], axis=0)
    key_col = lax.broadcasted_iota(jnp.int32, (1, 2 * ATT_BLOCK), 1)
    o_blocks = []
    for qb in range(tq // ATT_BLOCK):
        r0 = qb * ATT_BLOCK
        first_key_pos = s * tq + r0 - ATT_BLOCK
        score_parts = []
        for hk in range(N_KV_HEADS):
            q_pairs = jnp.concatenate(
                [q[r0:r0 + ATT_BLOCK, (2 * hk + pj) * LANES:(2 * hk + pj + 1) * LANES] for pj in range(2)], axis=0)
            for half in range(2):
                k_slab = kvbuf[r0:r0 + 2 * ATT_BLOCK, (2 * hk + half) * LANES:(2 * hk + half + 1) * LANES]
                score_parts.append(_dot_nt(q_pairs, k_slab))
        sc = jnp.concatenate(score_parts, axis=0) + bias_tab[...]
        sc = jnp.where(key_col + first_key_pos < 0, -jnp.inf, sc)
        m = jnp.maximum(jnp.max(sc, axis=1, keepdims=True), sink_col)
        p = jnp.exp(sc - m).astype(BF16)
        den = _dot(p, jnp.ones((2 * ATT_BLOCK, LANES), BF16)) + jnp.exp(sink_col - m)
        inv_den = 1.0 / den
        o_pairs = []
        for hk in range(N_KV_HEADS):
            pv = None
            for half in range(2):
                rows = (2 * hk + half) * 2 * ATT_BLOCK
                v_slab = kvbuf[r0:r0 + 2 * ATT_BLOCK, (4 + 2 * hk + half) * LANES:(4 + 2 * hk + half + 1) * LANES]
                contrib = _dot(p[rows:rows + 2 * ATT_BLOCK], v_slab) * inv_den[rows:rows + 2 * ATT_BLOCK]
                pv = contrib if pv is None else pv + contrib
            o_pairs += [pv[0:ATT_BLOCK], pv[ATT_BLOCK:2 * ATT_BLOCK]]
        o_blocks.append(jnp.concatenate(o_pairs, axis=1))
    o = jnp.concatenate(o_blocks, axis=0).astype(BF16)
    y_attn = _dot(o, w_bra_ref[...])
    kvbuf[0:ATT_BLOCK, :] = kvbuf[tq:tq + ATT_BLOCK, :]

    gates = _dot(xb, w_in_ref[:, GATE_START:D_IN]) + b_in_ref[:, GATE_START:D_IN]
    merged = _sigmoid(gates[:, 0:D_MODEL]) * y_pool + _sigmoid(gates[:, D_MODEL:2 * D_MODEL]) * y_attn
    mix = _dot(merged.astype(BF16), w_out_ref[...])
    x1 = _layer_norm(ALPHA * x + mix, ln1g_ref[...], ln1b_ref[...])
    _store_token_tiles(o_ref, x1)
    _store_packed_tokens(op_ref, x1)


def _first_mixer_kernel(sinks_ref, x_ref, ln0g_ref, ln0b_ref, *refs, layer, tq, n_s):
    t = pl.program_id(0)
    x = _layer_norm(x_ref[...], ln0g_ref[...], ln0b_ref[...])
    _mixer_body(x, lax.rem(t, n_s), t == 0, sinks_ref, *refs, layer=layer, tq=tq)


def _combine_mixer_kernel(sinks_ref, pos_hbm, y_hbm, xprev_ref, gate_ref, wsg_ref, wsu_ref, wsd_ref, ln2g_ref, ln2b_ref,
                          w_in_ref, b_in_ref, w_pool_ref, pscale_ref, w_brp_ref, w_bra_ref, w_out_ref, ln1g_ref,
                          ln1b_ref, o_ref, op_ref, ubuf, kvbuf, bias_tab, idx_smem, ybuf, sem_idx, sem_y,
                          *, layer, tq, n_s, n_tiles):
    h = pl.program_id(0)
    gslot = lax.rem(h, 2)
    rows_per_slot = TOP_K * tq
    tiles_per_step = tq // TOKEN_TILE

    def idx_copy(tile, sl):
        return pltpu.make_async_copy(pos_hbm.at[tile], idx_smem.at[sl], sem_idx.at[sl])

    @pl.when(h == 0)
    def _():
        idx_copy(0, 0).start()

    @pl.when(h < n_tiles)
    def _():
        idx_copy(h, gslot).wait()
        for ti in range(tiles_per_step):
            def fetch(k, j, ti=ti):
                pltpu.make_async_copy(
                    _packed_token(y_hbm, idx_smem[gslot, ti * TOP_K + k, j]),
                    _packed_token(ybuf, gslot * rows_per_slot + k * tq + ti * TOKEN_TILE + j), sem_y.at[gslot]
                ).start(priority=j % 2)
            _for_each_assignment(fetch)

    @pl.when(h + 1 < n_tiles)
    def _():
        idx_copy(h + 1, 1 - gslot).start()

    @pl.when(h > 0)
    def _():
        t = h - 1
        base = (1 - gslot) * rows_per_slot
        slot_rows = ybuf.at[pl.ds(pl.multiple_of(base * PACK_SUBLANES, PACK_SUBLANES), rows_per_slot * PACK_SUBLANES), :]
        pltpu.make_async_copy(slot_rows, slot_rows, sem_y.at[1 - gslot]).wait()
        x1 = _load_token_tiles(xprev_ref, tq)
        x1b = x1.astype(BF16)
        g = _dot(x1b, wsg_ref[...])
        a = (g * _sigmoid(g) * _dot(x1b, wsu_ref[...])).astype(BF16)
        shared = _dot(a, wsd_ref[...])
        gate = gate_ref[...]
        routed = None
        for k in range(TOP_K):
            term = gate[:, k:k + 1] * _load_packed_tokens(ybuf, tq, base + k * tq)
            routed = term if routed is None else routed + term
        x = _layer_norm(ALPHA * x1 + (routed + shared), ln2g_ref[...], ln2b_ref[...])
        _mixer_body(x, lax.rem(t, n_s), t == 0, sinks_ref, w_in_ref, b_in_ref, w_pool_ref, pscale_ref, w_brp_ref,
                    w_bra_ref, w_out_ref, ln1g_ref, ln1b_ref, o_ref, op_ref, ubuf, kvbuf, bias_tab, layer=layer, tq=tq)


def _mixer_specs(layer):
    per_layer = lambda *shape: pl.BlockSpec((None,) + shape, lambda h: (layer,) + (0,) * len(shape))
    return [
        per_layer(D_MODEL, D_IN), per_layer(1, D_IN), per_layer(POOL_GROUPS, POOL_GROUP_CH, POOL_GROUP_CH),
        per_layer(1, POOL_WIDTH), per_layer(POOL_WIDTH, D_MODEL), per_layer(Q_WIDTH, D_MODEL),
        per_layer(D_MODEL, D_MODEL), per_layer(1, D_MODEL), per_layer(1, D_MODEL),
    ]


def _mixer_outputs(T, tq, tile_of_step):
    out_specs = [
        pl.BlockSpec((tq * TILE_SUBLANES, LANES), lambda h: (tile_of_step(h), 0)),
        pl.BlockSpec((tq * PACK_SUBLANES, LANES), lambda h: (tile_of_step(h), 0)),
    ]
    out_shape = [
        jax.ShapeDtypeStruct((T * TILE_SUBLANES, LANES), F32),
        jax.ShapeDtypeStruct((T * PACK_SUBLANES, LANES), jnp.uint32),
    ]
    scratch = [
        pltpu.VMEM((POOL_HALO + tq, POOL_WIDTH), F32),
        pltpu.VMEM((ATT_BLOCK + tq, 8 * LANES), BF16),
        pltpu.VMEM((N_Q_HEADS * ATT_BLOCK, 2 * ATT_BLOCK), F32),
    ]
    return out_specs, out_shape, scratch


def _first_mixer(x, sinks, ln0_g, ln0_b, mixer_weights):
    B, S, D = x.shape
    tq = MIXER_TILE
    n_s = S // tq
    out_specs, out_shape, scratch = _mixer_outputs(B * S, tq, lambda h: h)
    return pl.pallas_call(
        functools.partial(_first_mixer_kernel, layer=0, tq=tq, n_s=n_s),
        grid=(B * n_s,),
        in_specs=[
            pl.BlockSpec(memory_space=pltpu.SMEM),
            pl.BlockSpec((None, tq, D), lambda h: (h // n_s, h % n_s, 0)),
            pl.BlockSpec((1, D), lambda h: (0, 0)),
            pl.BlockSpec((1, D), lambda h: (0, 0)),
        ] + _mixer_specs(0),
        out_specs=out_specs,
        out_shape=out_shape,
        scratch_shapes=scratch,
        compiler_params=pltpu.CompilerParams(dimension_semantics=("arbitrary",), vmem_limit_bytes=VMEM_LIMIT),
        name="mixer_l0",
    )(sinks, x, ln0_g, ln0_b, *mixer_weights)


def _combine_mixer(pos_tiles, y_sorted, x1_prev, gate_t, shared_weights, sinks, mixer_weights, *, layer, n_s):
    tq = MIXER_TILE
    T = x1_prev.shape[0] // TILE_SUBLANES
    n_tiles = T // tq
    tiles_per_step = tq // TOKEN_TILE
    pos_steps = pos_tiles.reshape(n_tiles, tiles_per_step * TOP_K, TOKEN_TILE)
    tile_of_step = lambda h: jnp.maximum(h - 1, 0)
    prev = lambda *shape: pl.BlockSpec((None,) + shape, lambda h: (layer - 1,) + (0,) * len(shape))
    out_specs, out_shape, scratch = _mixer_outputs(T, tq, tile_of_step)
    return pl.pallas_call(
        functools.partial(_combine_mixer_kernel, layer=layer, tq=tq, n_s=n_s, n_tiles=n_tiles),
        grid=(n_tiles + 1,),
        in_specs=[
            pl.BlockSpec(memory_space=pltpu.SMEM),
            pl.BlockSpec(memory_space=pl.ANY),
            pl.BlockSpec(memory_space=pl.ANY),
            pl.BlockSpec((tq * TILE_SUBLANES, LANES), lambda h: (tile_of_step(h), 0)),
            pl.BlockSpec((tq, TOP_K), lambda h: (tile_of_step(h), 0)),
            prev(D_MODEL, EXPERT_FF), prev(D_MODEL, EXPERT_FF), prev(EXPERT_FF, D_MODEL), prev(1, D_MODEL), prev(1, D_MODEL),
        ] + _mixer_specs(layer),
        out_specs=out_specs,
        out_shape=out_shape,
        scratch_shapes=scratch + [
            pltpu.SMEM((2, tiles_per_step * TOP_K, TOKEN_TILE), jnp.int32),
            pltpu.VMEM((2 * TOP_K * tq * PACK_SUBLANES, LANES), jnp.uint32),
            pltpu.SemaphoreType.DMA((2,)),
            pltpu.SemaphoreType.DMA((2,)),
        ],
        compiler_params=pltpu.CompilerParams(dimension_semantics=("arbitrary",), vmem_limit_bytes=FUSED_VMEM_LIMIT),
        name=f"combine_mixer_l{layer}",
    )(sinks, pos_steps, y_sorted, x1_prev, gate_t, *shared_weights, *mixer_weights)


def _first_index_of_max(vals, iota, n):
    m = jnp.max(vals, axis=0, keepdims=True)
    idx = jnp.min(jnp.where(vals == m, iota, n), axis=0, keepdims=True)
    return m, idx


def _route(xb, wr_ref, bias_ref, ek_ref, rk_ref, gk_ref, cnt_ref, carry):
    tr = xb.shape[0]
    logits = _dot_nt(wr_ref[...], xb)
    scores = _sigmoid(logits)
    biased = scores + bias_ref[...]
    neg_inf = -jnp.inf

    io8 = lax.broadcasted_iota(jnp.int32, (EXPERTS_PER_GROUP, tr), 0)
    group_rows = []
    for g in range(N_EXPERT_GROUPS):
        blk = biased[g * EXPERTS_PER_GROUP:(g + 1) * EXPERTS_PER_GROUP]
        m1, i1 = _first_index_of_max(blk, io8, EXPERTS_PER_GROUP)
        m2 = jnp.max(jnp.where(io8 == i1, neg_inf, blk), axis=0, keepdims=True)
        group_rows.append(m1 + m2)
    gscore = jnp.concatenate(group_rows, axis=0)
    iog = lax.broadcasted_iota(jnp.int32, (N_EXPERT_GROUPS, tr), 0)
    keep = jnp.zeros((N_EXPERT_GROUPS, tr), F32)
    for _ in range(TOPK_GROUPS):
        _, gi = _first_index_of_max(gscore, iog, N_EXPERT_GROUPS)
        hit = iog == gi
        keep = jnp.where(hit, 1.0, keep)
        gscore = jnp.where(hit, neg_inf, gscore)
    masked = jnp.concatenate(
        [jnp.where(keep[g:g + 1] > 0.0, biased[g * EXPERTS_PER_GROUP:(g + 1) * EXPERTS_PER_GROUP], neg_inf)
         for g in range(N_EXPERT_GROUPS)], axis=0)

    ioe = lax.broadcasted_iota(jnp.int32, (N_EXPERTS, tr), 0)
    sel = jnp.zeros((N_EXPERTS, tr), F32)
    e_rows, s_rows, hits = [], [], []
    for _ in range(TOP_K):
        _, ei = _first_index_of_max(masked, ioe, N_EXPERTS)
        hit = ioe == ei
        sel = jnp.where(hit, 1.0, sel)
        masked = jnp.where(hit, neg_inf, masked)
        e_rows.append(ei)
        hits.append(hit)
        s_rows.append(jnp.sum(jnp.where(hit, scores, 0.0), axis=0, keepdims=True))
    sel_scores = jnp.concatenate(s_rows, axis=0)
    gk_ref[...] = sel_scores / jnp.sum(sel_scores, axis=0, keepdims=True) * ROUTED_SCALE
    ek_ref[...] = jnp.concatenate(e_rows, axis=0)

    before = (lax.broadcasted_iota(jnp.int32, (tr, tr), 0) < lax.broadcasted_iota(jnp.int32, (tr, tr), 1))
    prefix = _dot(sel.astype(BF16), jnp.where(before, 1.0, 0.0).astype(BF16))
    rank_full = prefix + carry[...]
    rk_ref[...] = jnp.concatenate(
        [jnp.sum(jnp.where(hit, rank_full, 0.0), axis=0, keepdims=True) for hit in hits], axis=0).astype(jnp.int32)
    total = carry[...] + jnp.sum(sel, axis=1, keepdims=True)
    carry[...] = total
    cnt_ref[...] = jnp.broadcast_to(total, (N_EXPERTS, LANES))


def _router_kernel(x_ref, wr_ref, bias_ref, ek_ref, rk_ref, gk_ref, cnt_ref, carry, *, tr):
    @pl.when(pl.program_id(0) == 0)
    def _():
        carry[...] = jnp.zeros_like(carry)

    _route(_load_token_tiles(x_ref, tr).astype(BF16), wr_ref, bias_ref, ek_ref, rk_ref, gk_ref, cnt_ref, carry)


def _router(x1, w_router_t, bias_col, *, layer):
    T = x1.shape[0] // TILE_SUBLANES
    tr = ROUTER_TILE
    row_spec = pl.BlockSpec((TOP_K, tr), lambda i: (0, i))
    return pl.pallas_call(
        functools.partial(_router_kernel, tr=tr),
        grid=(T // tr,),
        in_specs=[
            pl.BlockSpec((tr * TILE_SUBLANES, LANES), lambda i: (i, 0)),
            pl.BlockSpec((None, N_EXPERTS, D_MODEL), lambda i: (layer, 0, 0)),
            pl.BlockSpec((None, N_EXPERTS, 1), lambda i: (layer, 0, 0)),
        ],
        out_specs=[row_spec, row_spec, row_spec, pl.BlockSpec((N_EXPERTS, LANES), lambda i: (0, 0))],
        out_shape=[
            jax.ShapeDtypeStruct((TOP_K, T), jnp.int32),
            jax.ShapeDtypeStruct((TOP_K, T), jnp.int32),
            jax.ShapeDtypeStruct((TOP_K, T), F32),
            jax.ShapeDtypeStruct((N_EXPERTS, LANES), F32),
        ],
        scratch_shapes=[pltpu.VMEM((N_EXPERTS, 1), F32)],
        compiler_params=pltpu.CompilerParams(dimension_semantics=("arbitrary",), vmem_limit_bytes=VMEM_LIMIT),
        name=f"router_l{layer}",
    )(x1, w_router_t, bias_col)


def _for_each_assignment(fn):
    for k in range(TOP_K):
        for j in range(TOKEN_TILE):
            fn(k, j)


def _dispatch_kernel(fill_ref, nu_ref, pos_hbm, x_ref, xs_hbm, idx_smem, zbuf, sem_idx, sem_out, sem_fill,
                     *, n_tiles, n_blocks):
    i = pl.program_id(0)
    slot = lax.rem(i, 2)

    def idx_copy(tile, sl):
        return pltpu.make_async_copy(pos_hbm.at[tile], idx_smem.at[sl], sem_idx.at[sl])

    def fill_copy(row0, rows):
        n = rows * PACK_SUBLANES
        return pltpu.make_async_copy(
            zbuf.at[pl.ds(0, n), :], xs_hbm.at[pl.ds(pl.multiple_of(row0 * PACK_SUBLANES, PACK_SUBLANES), n), :],
            sem_fill)

    def pad_fill(e, wait):
        row0 = fill_ref[e]
        n_pad = (0 - row0) & (EXPERT_BLOCK - 1)
        piece = EXPERT_BLOCK // 2
        while piece >= 1:
            has = (n_pad & piece) != 0

            @pl.when(has)
            def _(row0=row0, piece=piece):
                fill_copy(0 if wait else row0, piece).wait() if wait else fill_copy(row0, piece).start()
            row0 = row0 + jnp.where(has, piece, 0)
            piece //= 2

    @pl.when(i == 0)
    def _():
        idx_copy(0, 0).start()
        zbuf[...] = jnp.zeros_like(zbuf)
        lax.fori_loop(0, N_EXPERTS, lambda e, c: (pad_fill(e, False), c)[1], 0)
        lax.fori_loop(nu_ref[0], n_blocks, lambda b, c: (fill_copy(b * EXPERT_BLOCK, EXPERT_BLOCK).start(), c)[1], 0)
        lax.fori_loop(0, N_EXPERTS, lambda e, c: (pad_fill(e, True), c)[1], 0)
        lax.fori_loop(nu_ref[0], n_blocks, lambda b, c: (fill_copy(0, EXPERT_BLOCK).wait(), c)[1], 0)

    idx_copy(i, slot).wait()

    @pl.when(i + 1 < n_tiles)
    def _():
        idx_copy(i + 1, 1 - slot).start()

    def send(k, j):
        pltpu.make_async_copy(_packed_token(x_ref, j), _packed_token(xs_hbm, idx_smem[slot, k, j]), sem_out
                              ).start(priority=j % 2)
    _for_each_assignment(send)

    for _ in range(TOP_K):
        pltpu.make_async_copy(x_ref, x_ref, sem_out).wait()


def _dispatch(fill_start, n_used, pos_tiles, x1p, n_sorted_rows, *, layer):
    T = x1p.shape[0] // PACK_SUBLANES
    n_tiles = T // TOKEN_TILE
    return pl.pallas_call(
        functools.partial(_dispatch_kernel, n_tiles=n_tiles, n_blocks=n_sorted_rows // EXPERT_BLOCK),
        grid_spec=pltpu.PrefetchScalarGridSpec(
            num_scalar_prefetch=2,
            grid=(n_tiles,),
            in_specs=[
                pl.BlockSpec(memory_space=pl.ANY),
                pl.BlockSpec((TOKEN_TILE * PACK_SUBLANES, LANES), lambda i, fill, nu: (i, 0)),
            ],
            out_specs=pl.BlockSpec(memory_space=pl.ANY),
            scratch_shapes=[
                pltpu.SMEM((2, TOP_K, TOKEN_TILE), jnp.int32),
                pltpu.VMEM((EXPERT_BLOCK * PACK_SUBLANES, LANES), jnp.uint32),
                pltpu.SemaphoreType.DMA((2,)),
                pltpu.SemaphoreType.DMA(()),
                pltpu.SemaphoreType.DMA(()),
            ],
        ),
        out_shape=jax.ShapeDtypeStruct((n_sorted_rows * PACK_SUBLANES, LANES), jnp.uint32),
        compiler_params=pltpu.CompilerParams(dimension_semantics=("arbitrary",), vmem_limit_bytes=VMEM_LIMIT),
        name=f"dispatch_l{layer}",
    )(fill_start, n_used, pos_tiles, x1p)


def _expert_kernel(be_ref, nu_ref, xs_ref, *refs):
    i = pl.program_id(0)
    n_slots = EXPERT_STEP_BLOCKS
    y_ref = refs[3 * n_slots]
    for sub in range(n_slots):
        wg_ref, wu_ref, wd_ref = refs[3 * sub:3 * sub + 3]
        wgu_bf, wd_bf = refs[3 * n_slots + 1 + 2 * sub:3 * n_slots + 3 + 2 * sub]
        blk = i * n_slots + sub
        row_base = sub * EXPERT_BLOCK

        @pl.when(blk < nu_ref[0])
        def _(blk=blk, row_base=row_base, wg_ref=wg_ref, wu_ref=wu_ref, wd_ref=wd_ref, wgu_bf=wgu_bf, wd_bf=wd_bf):
            new_expert = (i == 0) | (be_ref[blk] != be_ref[jnp.maximum(blk - n_slots, 0)])

            @pl.when(new_expert)
            def _():
                wgu_bf[:, 0:EXPERT_FF] = wg_ref[...].astype(BF16)
                wgu_bf[:, EXPERT_FF:2 * EXPERT_FF] = wu_ref[...].astype(BF16)
                wd_bf[...] = wd_ref[...].astype(BF16)

            for r0 in range(row_base, row_base + EXPERT_BLOCK, EXPERT_CHUNK):
                h = _dot(_load_packed_tokens(xs_ref, EXPERT_CHUNK, r0).astype(BF16), wgu_bf[...])
                g = h[:, 0:EXPERT_FF]
                a = (g * _sigmoid(g) * h[:, EXPERT_FF:2 * EXPERT_FF]).astype(BF16)
                _store_packed_tokens(y_ref, _dot(a, wd_bf[...]), r0)

        @pl.when(blk >= nu_ref[0])
        def _(row_base=row_base):
            y_ref[row_base * PACK_SUBLANES:(row_base + EXPERT_BLOCK) * PACK_SUBLANES, :] = jnp.zeros(
                (EXPERT_BLOCK * PACK_SUBLANES, LANES), jnp.uint32)


def _experts(blk_expert, n_used, xs, w_gate, w_up, w_down, *, layer):
    nb = blk_expert.shape[0]
    n_slots = EXPERT_STEP_BLOCKS
    assert nb % n_slots == 0
    step_rows = n_slots * EXPERT_BLOCK
    w_specs = []
    for sub in range(n_slots):
        index = lambda i, be, nu, sub=sub: (layer * N_EXPERTS + be[jnp.minimum(i * n_slots + sub, nu[0] - 1)], 0, 0)
        w_specs += [pl.BlockSpec((None, D_MODEL, EXPERT_FF), index), pl.BlockSpec((None, D_MODEL, EXPERT_FF), index),
                    pl.BlockSpec((None, EXPERT_FF, D_MODEL), index)]
    in_row_spec = pl.BlockSpec((step_rows * PACK_SUBLANES, LANES),
                               lambda i, be, nu: (jnp.minimum(i, (nu[0] - 1) // n_slots), 0))
    return pl.pallas_call(
        _expert_kernel,
        grid_spec=pltpu.PrefetchScalarGridSpec(
            num_scalar_prefetch=2,
            grid=(nb // n_slots,),
            in_specs=[in_row_spec] + w_specs,
            out_specs=pl.BlockSpec((step_rows * PACK_SUBLANES, LANES), lambda i, be, nu: (i, 0)),
            scratch_shapes=[
                pltpu.VMEM((D_MODEL, 2 * EXPERT_FF), BF16),
                pltpu.VMEM((EXPERT_FF, D_MODEL), BF16),
            ] * n_slots,
        ),
        out_shape=jax.ShapeDtypeStruct((nb * EXPERT_BLOCK * PACK_SUBLANES, LANES), jnp.uint32),
        compiler_params=pltpu.CompilerParams(dimension_semantics=("arbitrary",), vmem_limit_bytes=VMEM_LIMIT),
        name=f"experts_l{layer}",
    )(blk_expert, n_used, xs, *([w_gate, w_up, w_down] * n_slots))


def _combine_kernel(pos_hbm, y_hbm, x_ref, gate_ref, wsg_ref, wsu_ref, wsd_ref, ln2g_ref, ln2b_ref, o_ref,
                    idx_smem, ybuf, sem_idx, sem_y, *, n_tiles):
    i = pl.program_id(0)
    slot = lax.rem(i, 2)
    nslot = 1 - slot
    rows_per_slot = TOP_K * TOKEN_TILE

    def idx_copy(tile, sl):
        return pltpu.make_async_copy(pos_hbm.at[tile], idx_smem.at[sl], sem_idx.at[sl])

    def start_gather(sl):
        def fetch(k, j):
            pltpu.make_async_copy(_packed_token(y_hbm, idx_smem[sl, k, j]),
                                  _packed_token(ybuf, sl * rows_per_slot + k * TOKEN_TILE + j), sem_y.at[sl]
                                  ).start(priority=j % 2)
        _for_each_assignment(fetch)

    @pl.when(i == 0)
    def _():
        idx_copy(0, 0).start()
        idx_copy(0, 0).wait()
        start_gather(0)
        if n_tiles > 1:
            idx_copy(1, 1).start()

    @pl.when(i + 1 < n_tiles)
    def _():
        idx_copy(i + 1, nslot).wait()
        start_gather(nslot)

    @pl.when(i + 2 < n_tiles)
    def _():
        idx_copy(i + 2, slot).start()

    x = _load_token_tiles(x_ref, TOKEN_TILE)
    xb = x.astype(BF16)
    g = _dot(xb, wsg_ref[...])
    a = (g * _sigmoid(g) * _dot(xb, wsu_ref[...])).astype(BF16)
    shared = _dot(a, wsd_ref[...])

    base = slot * rows_per_slot
    slot_rows = ybuf.at[pl.ds(pl.multiple_of(base * PACK_SUBLANES, PACK_SUBLANES), rows_per_slot * PACK_SUBLANES), :]
    pltpu.make_async_copy(slot_rows, slot_rows, sem_y.at[slot]).wait()
    gate = gate_ref[...]
    routed = None
    for k in range(TOP_K):
        term = gate[:, k:k + 1] * _load_packed_tokens(ybuf, TOKEN_TILE, base + k * TOKEN_TILE)
        routed = term if routed is None else routed + term
    o_ref[...] = _layer_norm(ALPHA * x + (routed + shared), ln2g_ref[...], ln2b_ref[...])


def _combine(pos_tiles, y_sorted, x1, gate_t, w_sg, w_su, w_sd, ln2_g, ln2_b, *, layer):
    T = x1.shape[0] // TILE_SUBLANES
    tc = TOKEN_TILE
    n_tiles = T // tc
    per_layer = lambda *shape: pl.BlockSpec((None,) + shape, lambda i: (layer,) + (0,) * len(shape))
    return pl.pallas_call(
        functools.partial(_combine_kernel, n_tiles=n_tiles),
        grid=(n_tiles,),
        in_specs=[
            pl.BlockSpec(memory_space=pl.ANY),
            pl.BlockSpec(memory_space=pl.ANY),
            pl.BlockSpec((tc * TILE_SUBLANES, LANES), lambda i: (i, 0)),
            pl.BlockSpec((tc, TOP_K), lambda i: (i, 0)),
            per_layer(D_MODEL, EXPERT_FF),
            per_layer(D_MODEL, EXPERT_FF),
            per_layer(EXPERT_FF, D_MODEL),
            per_layer(1, D_MODEL),
            per_layer(1, D_MODEL),
        ],
        out_specs=pl.BlockSpec((tc, D_MODEL), lambda i: (i, 0)),
        out_shape=jax.ShapeDtypeStruct((T, D_MODEL), F32),
        scratch_shapes=[
            pltpu.SMEM((2, TOP_K, TOKEN_TILE), jnp.int32),
            pltpu.VMEM((2 * TOP_K * tc * PACK_SUBLANES, LANES), jnp.uint32),
            pltpu.SemaphoreType.DMA((2,)),
            pltpu.SemaphoreType.DMA((2,)),
        ],
        compiler_params=pltpu.CompilerParams(dimension_semantics=("arbitrary",), vmem_limit_bytes=VMEM_LIMIT),
        name=f"combine_l{layer}",
    )(pos_tiles, y_sorted, x1, gate_t, w_sg, w_su, w_sd, ln2_g, ln2_b)


def _dispatch_plan(ek, rk, counts_f):
    T = ek.shape[1]
    blk = EXPERT_BLOCK
    nb = (T * TOP_K) // blk + N_EXPERTS
    experts = jnp.arange(N_EXPERTS, dtype=jnp.int32)
    counts = counts_f[:, 0].astype(jnp.int32)
    pcounts = (counts + blk - 1) // blk * blk
    pends = jnp.sum(jnp.where(experts[None, :] <= experts[:, None], pcounts[None, :], 0), axis=1)
    pstarts = pends - pcounts
    pos = jnp.sum(jnp.where(ek[None] == experts[:, None, None], pstarts[:, None, None], 0), axis=0) + rk
    n_used = (pends[-1] // blk).reshape(1)
    block_row0 = jnp.arange(nb, dtype=jnp.int32) * blk
    blk_expert = jnp.minimum(jnp.sum((pends[None, :] <= block_row0[:, None]).astype(jnp.int32), axis=1), N_EXPERTS - 1)
    fill_start = pstarts + counts
    pos_tiles = pos.reshape(TOP_K, T // TOKEN_TILE, TOKEN_TILE).transpose(1, 0, 2)
    return blk_expert, n_used, fill_start, pos_tiles, nb * blk


def kernel(x, ln0_g, ln0_b, w_in, b_in, w_pool, pool_scale, attn_sinks, w_br_pool, w_br_attn, w_out, ln1_g, ln1_b,
           w_router, router_bias, w_exp_gate, w_exp_up, w_exp_down, w_sh_gate, w_sh_up, w_sh_down, ln2_g, ln2_b):
    B, S, D = x.shape
    depth = w_in.shape[0]
    row = lambda a: a.reshape(a.shape[0], 1, a.shape[1])
    w_in_b, w_pool_b = w_in.astype(BF16), w_pool.astype(BF16)
    w_brp_b, w_bra_b, w_out_b = w_br_pool.astype(BF16), w_br_attn.astype(BF16), w_out.astype(BF16)
    w_router_t = jnp.swapaxes(w_router, 1, 2).astype(BF16)
    bias_col = router_bias.reshape(depth, N_EXPERTS, 1)
    w_sg_b, w_su_b, w_sd_b = w_sh_gate.astype(BF16), w_sh_up.astype(BF16), w_sh_down.astype(BF16)
    w_eg = w_exp_gate.reshape(depth * N_EXPERTS, D, EXPERT_FF)
    w_eu = w_exp_up.reshape(depth * N_EXPERTS, D, EXPERT_FF)
    w_ed = w_exp_down.reshape(depth * N_EXPERTS, EXPERT_FF, D)
    ln0_g2, ln0_b2 = ln0_g.reshape(1, D), ln0_b.reshape(1, D)

    mixer_weights = (w_in_b, row(b_in), w_pool_b, row(pool_scale), w_brp_b, w_bra_b, w_out_b, row(ln1_g), row(ln1_b))
    shared_weights = (w_sg_b, w_su_b, w_sd_b, row(ln2_g), row(ln2_b))
    n_s = S // MIXER_TILE
    x1, x1p = _first_mixer(x, attn_sinks, ln0_g2, ln0_b2, mixer_weights)
    for l in range(depth):
        ek, rk, gk, counts_f = _router(x1, w_router_t, bias_col, layer=l)
        blk_expert, n_used, fill_start, pos_tiles, n_sorted = _dispatch_plan(ek, rk, counts_f)
        xs = _dispatch(fill_start, n_used, pos_tiles, x1p, n_sorted + EXPERT_BLOCK, layer=l)
        y_sorted = _experts(blk_expert, n_used, xs, w_eg, w_eu, w_ed, layer=l)
        if l + 1 < depth:
            x1, x1p = _combine_mixer(pos_tiles, y_sorted, x1, gk.T, shared_weights, attn_sinks, mixer_weights,
                                     layer=l + 1, n_s=n_s)
    return _combine(pos_tiles, y_sorted, x1, gk.T, *shared_weights, layer=depth - 1).reshape(B, S, D)
```

```python
import functools

import jax
import jax.numpy as jnp
from jax import lax
from jax.experimental import pallas as pl
from jax.experimental.pallas import tpu as pltpu

D_MODEL = 1024
DEPTH = 4
POOL_GROUPS = 4
POOL_GROUP_CH = 128
POOL_WIDTH = POOL_GROUPS * POOL_GROUP_CH
POOL_WINDOWS = (2, 4, 8, 16)
POOL_HALO = 16
N_Q_HEADS = 8
N_KV_HEADS = 2
HEAD_DIM = 64
Q_WIDTH = N_Q_HEADS * HEAD_DIM
KV_WIDTH = N_KV_HEADS * HEAD_DIM
WINDOW = 128
ATT_BLOCK = 128
D_IN = POOL_WIDTH + Q_WIDTH + 2 * KV_WIDTH + 2 * D_MODEL
QKV_START = POOL_WIDTH
GATE_START = POOL_WIDTH + Q_WIDTH + 2 * KV_WIDTH
N_EXPERTS = 64
EXPERT_FF = 256
TOP_K = 8
N_EXPERT_GROUPS = 8
EXPERTS_PER_GROUP = N_EXPERTS // N_EXPERT_GROUPS
TOPK_GROUPS = 4
ROUTED_SCALE = 2.5
ALPHA = (2.0 * DEPTH) ** 0.25
LN_EPS = 1e-5
SEGMENT_HEADS = (0, 2, 1, 3, 4, 6, 5, 7)
ALIBI_SLOPES = tuple(float(2.0 ** (-8.0 * h / N_Q_HEADS)) for h in range(1, N_Q_HEADS + 1))

LANES = 128
SUBLANES = 8
TILE_SUBLANES = D_MODEL // LANES
PACK_SUBLANES = TILE_SUBLANES // 2
MIXER_TILE = 256
ROUTER_TILE = 1024
EXPERT_BLOCK = 512
EXPERT_CHUNK = 512
EXPERT_STEP_BLOCKS = 2
TOKEN_TILE = 128
VMEM_LIMIT = 48 * 1024 * 1024
FUSED_VMEM_LIMIT = 56 * 1024 * 1024

BF16 = jnp.bfloat16
F32 = jnp.float32

assert TILE_SUBLANES == SUBLANES and TOP_K == SUBLANES and TOKEN_TILE == LANES


def _dot(a, b):
    return jnp.dot(a, b, preferred_element_type=F32)


def _dot_nt(a, b):
    return lax.dot_general(a, b, (((1,), (1,)), ((), ())), preferred_element_type=F32)


def _layer_norm(x, g, b):
    mu = jnp.mean(x, axis=-1, keepdims=True)
    xc = x - mu
    var = jnp.mean(xc * xc, axis=-1, keepdims=True)
    return xc * lax.rsqrt(var + LN_EPS) * g + b


def _sigmoid(x):
    return 0.5 * jnp.tanh(0.5 * x) + 0.5


def _load_token_tiles(ref, rows, row0=0):
    return jnp.concatenate(
        [ref[pl.ds(row0 * TILE_SUBLANES + j, rows, stride=TILE_SUBLANES), :] for j in range(TILE_SUBLANES)], axis=1)


def _store_token_tiles(ref, value, row0=0):
    for j in range(TILE_SUBLANES):
        ref[pl.ds(row0 * TILE_SUBLANES + j, value.shape[0], stride=TILE_SUBLANES), :] = (
            value[:, j * LANES:(j + 1) * LANES])


def _token_tile(ref, row):
    return ref.at[pl.ds(pl.multiple_of(row * TILE_SUBLANES, TILE_SUBLANES), TILE_SUBLANES), :]


def _bf16_bits(x):
    return lax.bitcast_convert_type(x.astype(BF16).astype(F32), jnp.uint32)


def _store_packed_tokens(ref, value, row0=0):
    n = value.shape[0]
    for c in range(PACK_SUBLANES):
        lo = _bf16_bits(value[:, c * LANES:(c + 1) * LANES])
        hi = _bf16_bits(value[:, (c + PACK_SUBLANES) * LANES:(c + PACK_SUBLANES + 1) * LANES])
        ref[pl.ds(row0 * PACK_SUBLANES + c, n, stride=PACK_SUBLANES), :] = (
            lax.shift_right_logical(lo, jnp.uint32(16)) | hi)


def _load_packed_tokens(ref, rows, row0=0):
    los, his = [], []
    for c in range(PACK_SUBLANES):
        w = ref[pl.ds(row0 * PACK_SUBLANES + c, rows, stride=PACK_SUBLANES), :]
        los.append(lax.bitcast_convert_type(lax.shift_left(w, jnp.uint32(16)), F32))
        his.append(lax.bitcast_convert_type(w & jnp.uint32(0xFFFF0000), F32))
    return jnp.concatenate(los + his, axis=1)


def _packed_token(ref, row):
    return ref.at[pl.ds(pl.multiple_of(row * PACK_SUBLANES, PACK_SUBLANES), PACK_SUBLANES), :]


def _mixer_body(x, s, first, sinks_ref, w_in_ref, b_in_ref, w_pool_ref, pscale_ref, w_brp_ref, w_bra_ref, w_out_ref,
                ln1g_ref, ln1b_ref, o_ref, op_ref, ubuf, kvbuf, bias_tab, *, layer, tq):
    @pl.when(s == 0)
    def _():
        ubuf[0:POOL_HALO, :] = jnp.zeros((POOL_HALO, POOL_WIDTH), F32)
        kvbuf[0:ATT_BLOCK, :] = jnp.zeros((ATT_BLOCK, 8 * LANES), BF16)

    xb = x.astype(BF16)

    u = _dot(xb, w_in_ref[:, 0:POOL_WIDTH]) + b_in_ref[:, 0:POOL_WIDTH]
    ubuf[POOL_HALO:POOL_HALO + tq, :] = u
    pos = (s * tq + lax.broadcasted_iota(jnp.int32, (tq, 1), 0)).astype(F32)
    mixed_parts = []
    for g, w in enumerate(POOL_WINDOWS):
        sl = slice(g * POOL_GROUP_CH, (g + 1) * POOL_GROUP_CH)
        cur = ubuf[POOL_HALO:POOL_HALO + tq, sl]
        acc = cur
        for j in range(1, w):
            acc = acc + ubuf[POOL_HALO - j:POOL_HALO - j + tq, sl]
        inv_cnt = 1.0 / jnp.minimum(pos + 1.0, float(w))
        d = (acc * inv_cnt - cur).astype(BF16)
        mixed_parts.append(_dot(d, w_pool_ref[g]) * pscale_ref[:, sl])
    mixed = jnp.concatenate(mixed_parts, axis=1).astype(BF16)
    y_pool = _dot(mixed, w_brp_ref[...])
    ubuf[0:POOL_HALO, :] = ubuf[tq:tq + POOL_HALO, :]

    qkv = _dot(xb, w_in_ref[:, QKV_START:GATE_START]) + b_in_ref[:, QKV_START:GATE_START]
    q = (qkv[:, 0:Q_WIDTH] * (HEAD_DIM ** -0.5)).astype(BF16)
    k = qkv[:, Q_WIDTH:Q_WIDTH + KV_WIDTH]
    v = qkv[:, Q_WIDTH + KV_WIDTH:Q_WIDTH + 2 * KV_WIDTH]
    lo = lax.broadcasted_iota(jnp.int32, (tq, LANES), 1) < HEAD_DIM
    k_sw = pltpu.roll(k, HEAD_DIM, axis=1)
    v_sw = pltpu.roll(v, HEAD_DIM, axis=1)
    zero = jnp.zeros((tq, LANES), F32)
    slabs = (
        jnp.where(lo, k, zero), jnp.where(lo, zero, k_sw),
        jnp.where(lo, k_sw, zero), jnp.where(lo, zero, k),
        jnp.where(lo, v, zero), jnp.where(lo, zero, v_sw),
        jnp.where(lo, v_sw, zero), jnp.where(lo, zero, v),
    )
    for i, slab in enumerate(slabs):
        kvbuf[ATT_BLOCK:ATT_BLOCK + tq, i * LANES:(i + 1) * LANES] = slab.astype(BF16)

    @pl.when(first)
    def _():
        qi = lax.broadcasted_iota(jnp.int32, (ATT_BLOCK, 2 * ATT_BLOCK), 0)
        kj = lax.broadcasted_iota(jnp.int32, (ATT_BLOCK, 2 * ATT_BLOCK), 1)
        dist = ATT_BLOCK + qi - kj
        band_ok = (dist >= 0) & (dist < WINDOW)
        distf = dist.astype(F32)
        for i, h in enumerate(SEGMENT_HEADS):
            bias_tab[i * ATT_BLOCK:(i + 1) * ATT_BLOCK, :] = jnp.where(band_ok, -ALIBI_SLOPES[h] * distf, -jnp.inf)

    sink_col = jnp.concatenate([jnp.full((ATT_BLOCK, 1), sinks_ref[layer, h], F32) for h in SEGMENT_HEADS], axis=0)
    key_col = lax.broadcasted_iota(jnp.int32, (1, 2 * ATT_BLOCK), 1)
    o_blocks = []
    for qb in range(tq // ATT_BLOCK):
        r0 = qb * ATT_BLOCK
        first_key_pos = s * tq + r0 - ATT_BLOCK
        score_parts = []
        for hk in range(N_KV_HEADS):
            q_pairs = jnp.concatenate(
                [q[r0:r0 + ATT_BLOCK, (2 * hk + pj) * LANES:(2 * hk + pj + 1) * LANES] for pj in range(2)], axis=0)
            for half in range(2):
                k_slab = kvbuf[r0:r0 + 2 * ATT_BLOCK, (2 * hk + half) * LANES:(2 * hk + half + 1) * LANES]
                score_parts.append(_dot_nt(q_pairs, k_slab))
        sc = jnp.concatenate(score_parts, axis=0) + bias_tab[...]
        sc = jnp.where(key_col + first_key_pos < 0, -jnp.inf, sc)
        m = jnp.maximum(jnp.max(sc, axis=1, keepdims=True), sink_col)
        p = jnp.exp(sc - m).astype(BF16)
        den = _dot(p, jnp.ones((2 * ATT_BLOCK, LANES), BF16)) + jnp.exp(sink_col - m)
        inv_den = 1.0 / den
        o_pairs = []
        for hk in range(N_KV_HEADS):
            pv = None
            for half in range(2):
                rows = (2 * hk + half) * 2 * ATT_BLOCK
                v_slab = kvbuf[r0:r0 + 2 * ATT_BLOCK, (4 + 2 * hk + half) * LANES:(4 + 2 * hk + half + 1) * LANES]
                contrib = _dot(p[rows:rows + 2 * ATT_BLOCK], v_slab) * inv_den[rows:rows + 2 * ATT_BLOCK]
                pv = contrib if pv is None else pv + contrib
            o_pairs += [pv[0:ATT_BLOCK], pv[ATT_BLOCK:2 * ATT_BLOCK]]
        o_blocks.append(jnp.concatenate(o_pairs, axis=1))
    o = jnp.concatenate(o_blocks, axis=0).astype(BF16)
    y_attn = _dot(o, w_bra_ref[...])
    kvbuf[0:ATT_BLOCK, :] = kvbuf[tq:tq + ATT_BLOCK, :]

    gates = _dot(xb, w_in_ref[:, GATE_START:D_IN]) + b_in_ref[:, GATE_START:D_IN]
    merged = _sigmoid(gates[:, 0:D_MODEL]) * y_pool + _sigmoid(gates[:, D_MODEL:2 * D_MODEL]) * y_attn
    mix = _dot(merged.astype(BF16), w_out_ref[...])
    x1 = _layer_norm(ALPHA * x + mix, ln1g_ref[...], ln1b_ref[...])
    _store_token_tiles(o_ref, x1)
    _store_packed_tokens(op_ref, x1)


def _first_mixer_kernel(sinks_ref, x_ref, ln0g_ref, ln0b_ref, *refs, layer, tq, n_s):
    t = pl.program_id(0)
    x = _layer_norm(x_ref[...], ln0g_ref[...], ln0b_ref[...])
    _mixer_body(x, lax.rem(t, n_s), t == 0, sinks_ref, *refs, layer=layer, tq=tq)


def _combine_mixer_kernel(sinks_ref, pos_hbm, y_hbm, xprev_ref, gate_ref, wsg_ref, wsu_ref, wsd_ref, ln2g_ref, ln2b_ref,
                          w_in_ref, b_in_ref, w_pool_ref, pscale_ref, w_brp_ref, w_bra_ref, w_out_ref, ln1g_ref,
                          ln1b_ref, o_ref, op_ref, ubuf, kvbuf, bias_tab, idx_smem, ybuf, sem_idx, sem_y,
                          *, layer, tq, n_s, n_tiles):
    h = pl.program_id(0)
    gslot = lax.rem(h, 2)
    rows_per_slot = TOP_K * tq
    tiles_per_step = tq // TOKEN_TILE

    def idx_copy(tile, sl):
        return pltpu.make_async_copy(pos_hbm.at[tile], idx_smem.at[sl], sem_idx.at[sl])

    @pl.when(h == 0)
    def _():
        idx_copy(0, 0).start()

    @pl.when(h < n_tiles)
    def _():
        idx_copy(h, gslot).wait()
        for ti in range(tiles_per_step):
            def fetch(k, j, ti=ti):
                pltpu.make_async_copy(
                    _packed_token(y_hbm, idx_smem[gslot, ti * TOP_K + k, j]),
                    _packed_token(ybuf, gslot * rows_per_slot + k * tq + ti * TOKEN_TILE + j), sem_y.at[gslot]
                ).start(priority=j % 2)
            _for_each_assignment(fetch)

    @pl.when(h + 1 < n_tiles)
    def _():
        idx_copy(h + 1, 1 - gslot).start()

    @pl.when(h > 0)
    def _():
        t = h - 1
        base = (1 - gslot) * rows_per_slot
        slot_rows = ybuf.at[pl.ds(pl.multiple_of(base * PACK_SUBLANES, PACK_SUBLANES), rows_per_slot * PACK_SUBLANES), :]
        pltpu.make_async_copy(slot_rows, slot_rows, sem_y.at[1 - gslot]).wait()
        x1 = _load_token_tiles(xprev_ref, tq)
        x1b = x1.astype(BF16)
        g = _dot(x1b, wsg_ref[...])
        a = (g * _sigmoid(g) * _dot(x1b, wsu_ref[...])).astype(BF16)
        shared = _dot(a, wsd_ref[...])
        gate = gate_ref[...]
        routed = None
        for k in range(TOP_K):
            term = gate[:, k:k + 1] * _load_packed_tokens(ybuf, tq, base + k * tq)
            routed = term if routed is None else routed + term
        x = _layer_norm(ALPHA * x1 + (routed + shared), ln2g_ref[...], ln2b_ref[...])
        _mixer_body(x, lax.rem(t, n_s), t == 0, sinks_ref, w_in_ref, b_in_ref, w_pool_ref, pscale_ref, w_brp_ref,
                    w_bra_ref, w_out_ref, ln1g_ref, ln1b_ref, o_ref, op_ref, ubuf, kvbuf, bias_tab, layer=layer, tq=tq)


def _mixer_specs(layer):
    per_layer = lambda *shape: pl.BlockSpec((None,) + shape, lambda h: (layer,) + (0,) * len(shape))
    return [
        per_layer(D_MODEL, D_IN), per_layer(1, D_IN), per_layer(POOL_GROUPS, POOL_GROUP_CH, POOL_GROUP_CH),
        per_layer(1, POOL_WIDTH), per_layer(POOL_WIDTH, D_MODEL), per_layer(Q_WIDTH, D_MODEL),
        per_layer(D_MODEL, D_MODEL), per_layer(1, D_MODEL), per_layer(1, D_MODEL),
    ]


def _mixer_outputs(T, tq, tile_of_step):
    out_specs = [
        pl.BlockSpec((tq * TILE_SUBLANES, LANES), lambda h: (tile_of_step(h), 0)),
        pl.BlockSpec((tq * PACK_SUBLANES, LANES), lambda h: (tile_of_step(h), 0)),
    ]
    out_shape = [
        jax.ShapeDtypeStruct((T * TILE_SUBLANES, LANES), F32),
        jax.ShapeDtypeStruct((T * PACK_SUBLANES, LANES), jnp.uint32),
    ]
    scratch = [
        pltpu.VMEM((POOL_HALO + tq, POOL_WIDTH), F32),
        pltpu.VMEM((ATT_BLOCK + tq, 8 * LANES), BF16),
        pltpu.VMEM((N_Q_HEADS * ATT_BLOCK, 2 * ATT_BLOCK), F32),
    ]
    return out_specs, out_shape, scratch


def _first_mixer(x, sinks, ln0_g, ln0_b, mixer_weights):
    B, S, D = x.shape
    tq = MIXER_TILE
    n_s = S // tq
    out_specs, out_shape, scratch = _mixer_outputs(B * S, tq, lambda h: h)
    return pl.pallas_call(
        functools.partial(_first_mixer_kernel, layer=0, tq=tq, n_s=n_s),
        grid=(B * n_s,),
        in_specs=[
            pl.BlockSpec(memory_space=pltpu.SMEM),
            pl.BlockSpec((None, tq, D), lambda h: (h // n_s, h % n_s, 0)),
            pl.BlockSpec((1, D), lambda h: (0, 0)),
            pl.BlockSpec((1, D), lambda h: (0, 0)),
        ] + _mixer_specs(0),
        out_specs=out_specs,
        out_shape=out_shape,
        scratch_shapes=scratch,
        compiler_params=pltpu.CompilerParams(dimension_semantics=("arbitrary",), vmem_limit_bytes=VMEM_LIMIT),
        name="mixer_l0",
    )(sinks, x, ln0_g, ln0_b, *mixer_weights)


def _combine_mixer(pos_tiles, y_sorted, x1_prev, gate_t, shared_weights, sinks, mixer_weights, *, layer, n_s):
    tq = MIXER_TILE
    T = x1_prev.shape[0] // TILE_SUBLANES
    n_tiles = T // tq
    tiles_per_step = tq // TOKEN_TILE
    pos_steps = pos_tiles.reshape(n_tiles, tiles_per_step * TOP_K, TOKEN_TILE)
    tile_of_step = lambda h: jnp.maximum(h - 1, 0)
    prev = lambda *shape: pl.BlockSpec((None,) + shape, lambda h: (layer - 1,) + (0,) * len(shape))
    out_specs, out_shape, scratch = _mixer_outputs(T, tq, tile_of_step)
    return pl.pallas_call(
        functools.partial(_combine_mixer_kernel, layer=layer, tq=tq, n_s=n_s, n_tiles=n_tiles),
        grid=(n_tiles + 1,),
        in_specs=[
            pl.BlockSpec(memory_space=pltpu.SMEM),
            pl.BlockSpec(memory_space=pl.ANY),
            pl.BlockSpec(memory_space=pl.ANY),
            pl.BlockSpec((tq * TILE_SUBLANES, LANES), lambda h: (tile_of_step(h), 0)),
            pl.BlockSpec((tq, TOP_K), lambda h: (tile_of_step(h), 0)),
            prev(D_MODEL, EXPERT_FF), prev(D_MODEL, EXPERT_FF), prev(EXPERT_FF, D_MODEL), prev(1, D_MODEL), prev(1, D_MODEL),
        ] + _mixer_specs(layer),
        out_specs=out_specs,
        out_shape=out_shape,
        scratch_shapes=scratch + [
            pltpu.SMEM((2, tiles_per_step * TOP_K, TOKEN_TILE), jnp.int32),
            pltpu.VMEM((2 * TOP_K * tq * PACK_SUBLANES, LANES), jnp.uint32),
            pltpu.SemaphoreType.DMA((2,)),
            pltpu.SemaphoreType.DMA((2,)),
        ],
        compiler_params=pltpu.CompilerParams(dimension_semantics=("arbitrary",), vmem_limit_bytes=FUSED_VMEM_LIMIT),
        name=f"combine_mixer_l{layer}",
    )(sinks, pos_steps, y_sorted, x1_prev, gate_t, *shared_weights, *mixer_weights)


def _first_index_of_max(vals, iota, n):
    m = jnp.max(vals, axis=0, keepdims=True)
    idx = jnp.min(jnp.where(vals == m, iota, n), axis=0, keepdims=True)
    return m, idx


def _route(xb, wr_ref, bias_ref, ek_ref, rk_ref, gk_ref, cnt_ref, carry):
    tr = xb.shape[0]
    logits = _dot_nt(wr_ref[...], xb)
    scores = _sigmoid(logits)
    biased = scores + bias_ref[...]
    neg_inf = -jnp.inf

    io8 = lax.broadcasted_iota(jnp.int32, (EXPERTS_PER_GROUP, tr), 0)
    group_rows = []
    for g in range(N_EXPERT_GROUPS):
        blk = biased[g * EXPERTS_PER_GROUP:(g + 1) * EXPERTS_PER_GROUP]
        m1, i1 = _first_index_of_max(blk, io8, EXPERTS_PER_GROUP)
        m2 = jnp.max(jnp.where(io8 == i1, neg_inf, blk), axis=0, keepdims=True)
        group_rows.append(m1 + m2)
    gscore = jnp.concatenate(group_rows, axis=0)
    iog = lax.broadcasted_iota(jnp.int32, (N_EXPERT_GROUPS, tr), 0)
    keep = jnp.zeros((N_EXPERT_GROUPS, tr), F32)
    for _ in range(TOPK_GROUPS):
        _, gi = _first_index_of_max(gscore, iog, N_EXPERT_GROUPS)
        hit = iog == gi
        keep = jnp.where(hit, 1.0, keep)
        gscore = jnp.where(hit, neg_inf, gscore)
    masked = jnp.concatenate(
        [jnp.where(keep[g:g + 1] > 0.0, biased[g * EXPERTS_PER_GROUP:(g + 1) * EXPERTS_PER_GROUP], neg_inf)
         for g in range(N_EXPERT_GROUPS)], axis=0)

    ioe = lax.broadcasted_iota(jnp.int32, (N_EXPERTS, tr), 0)
    sel = jnp.zeros((N_EXPERTS, tr), F32)
    e_rows, s_rows, hits = [], [], []
    for _ in range(TOP_K):
        _, ei = _first_index_of_max(masked, ioe, N_EXPERTS)
        hit = ioe == ei
        sel = jnp.where(hit, 1.0, sel)
        masked = jnp.where(hit, neg_inf, masked)
        e_rows.append(ei)
        hits.append(hit)
        s_rows.append(jnp.sum(jnp.where(hit, scores, 0.0), axis=0, keepdims=True))
    sel_scores = jnp.concatenate(s_rows, axis=0)
    gk_ref[...] = sel_scores / jnp.sum(sel_scores, axis=0, keepdims=True) * ROUTED_SCALE
    ek_ref[...] = jnp.concatenate(e_rows, axis=0)

    before = (lax.broadcasted_iota(jnp.int32, (tr, tr), 0) < lax.broadcasted_iota(jnp.int32, (tr, tr), 1))
    prefix = _dot(sel.astype(BF16), jnp.where(before, 1.0, 0.0).astype(BF16))
    rank_full = prefix + carry[...]
    rk_ref[...] = jnp.concatenate(
        [jnp.sum(jnp.where(hit, rank_full, 0.0), axis=0, keepdims=True) for hit in hits], axis=0).astype(jnp.int32)
    total = carry[...] + jnp.sum(sel, axis=1, keepdims=True)
    carry[...] = total
    cnt_ref[...] = jnp.broadcast_to(total, (N_EXPERTS, LANES))


def _router_kernel(x_ref, wr_ref, bias_ref, ek_ref, rk_ref, gk_ref, cnt_ref, carry, *, tr):
    @pl.when(pl.program_id(0) == 0)
    def _():
        carry[...] = jnp.zeros_like(carry)

    _route(_load_token_tiles(x_ref, tr).astype(BF16), wr_ref, bias_ref, ek_ref, rk_ref, gk_ref, cnt_ref, carry)


def _router(x1, w_router_t, bias_col, *, layer):
    T = x1.shape[0] // TILE_SUBLANES
    tr = ROUTER_TILE
    row_spec = pl.BlockSpec((TOP_K, tr), lambda i: (0, i))
    return pl.pallas_call(
        functools.partial(_router_kernel, tr=tr),
        grid=(T // tr,),
        in_specs=[
            pl.BlockSpec((tr * TILE_SUBLANES, LANES), lambda i: (i, 0)),
            pl.BlockSpec((None, N_EXPERTS, D_MODEL), lambda i: (layer, 0, 0)),
            pl.BlockSpec((None, N_EXPERTS, 1), lambda i: (layer, 0, 0)),
        ],
        out_specs=[row_spec, row_spec, row_spec, pl.BlockSpec((N_EXPERTS, LANES), lambda i: (0, 0))],
        out_shape=[
            jax.ShapeDtypeStruct((TOP_K, T), jnp.int32),
            jax.ShapeDtypeStruct((TOP_K, T), jnp.int32),
            jax.ShapeDtypeStruct((TOP_K, T), F32),
            jax.ShapeDtypeStruct((N_EXPERTS, LANES), F32),
        ],
        scratch_shapes=[pltpu.VMEM((N_EXPERTS, 1), F32)],
        compiler_params=pltpu.CompilerParams(dimension_semantics=("arbitrary",), vmem_limit_bytes=VMEM_LIMIT),
        name=f"router_l{layer}",
    )(x1, w_router_t, bias_col)


def _for_each_assignment(fn):
    for k in range(TOP_K):
        for j in range(TOKEN_TILE):
            fn(k, j)


def _dispatch_kernel(fill_ref, nu_ref, pos_hbm, xp_hbm, x_ref, xs_hbm, idx_smem, zbuf, sem_idx, sem_out, sem_fill,
                     *, n_tiles, n_blocks):
    i = pl.program_id(0)
    slot = lax.rem(i, 2)

    def idx_copy(tile, sl):
        return pltpu.make_async_copy(pos_hbm.at[tile], idx_smem.at[sl], sem_idx.at[sl])

    def fill_copy(row0, rows):
        n = rows * PACK_SUBLANES
        return pltpu.make_async_copy(
            zbuf.at[pl.ds(0, n), :], xs_hbm.at[pl.ds(pl.multiple_of(row0 * PACK_SUBLANES, PACK_SUBLANES), n), :],
            sem_fill)

    def pad_fill(e, wait):
        row0 = fill_ref[e]
        n_pad = (0 - row0) & (EXPERT_BLOCK - 1)
        piece = EXPERT_BLOCK // 2
        while piece >= 1:
            has = (n_pad & piece) != 0

            @pl.when(has)
            def _(row0=row0, piece=piece):
                fill_copy(0 if wait else row0, piece).wait() if wait else fill_copy(row0, piece).start()
            row0 = row0 + jnp.where(has, piece, 0)
            piece //= 2

    @pl.when(i == 0)
    def _():
        idx_copy(0, 0).start()
        zbuf[...] = jnp.zeros_like(zbuf)
        lax.fori_loop(0, N_EXPERTS, lambda e, c: (pad_fill(e, False), c)[1], 0)
        lax.fori_loop(nu_ref[0], n_blocks, lambda b, c: (fill_copy(b * EXPERT_BLOCK, EXPERT_BLOCK).start(), c)[1], 0)
        lax.fori_loop(0, N_EXPERTS, lambda e, c: (pad_fill(e, True), c)[1], 0)
        lax.fori_loop(nu_ref[0], n_blocks, lambda b, c: (fill_copy(0, EXPERT_BLOCK).wait(), c)[1], 0)

    idx_copy(i, slot).wait()

    @pl.when(i + 1 < n_tiles)
    def _():
        idx_copy(i + 1, 1 - slot).start()

    def send(k, j):
        src = _packed_token(xp_hbm, i * TOKEN_TILE + j) if k % 2 else _packed_token(x_ref, j)
        pltpu.make_async_copy(src, _packed_token(xs_hbm, idx_smem[slot, k, j]), sem_out).start(priority=j % 2)
    _for_each_assignment(send)

    for _ in range(TOP_K):
        pltpu.make_async_copy(x_ref, x_ref, sem_out).wait()


def _dispatch(fill_start, n_used, pos_tiles, x1p, n_sorted_rows, *, layer):
    T = x1p.shape[0] // PACK_SUBLANES
    n_tiles = T // TOKEN_TILE
    return pl.pallas_call(
        functools.partial(_dispatch_kernel, n_tiles=n_tiles, n_blocks=n_sorted_rows // EXPERT_BLOCK),
        grid_spec=pltpu.PrefetchScalarGridSpec(
            num_scalar_prefetch=2,
            grid=(n_tiles,),
            in_specs=[
                pl.BlockSpec(memory_space=pl.ANY),
                pl.BlockSpec(memory_space=pl.ANY),
                pl.BlockSpec((TOKEN_TILE * PACK_SUBLANES, LANES), lambda i, fill, nu: (i, 0)),
            ],
            out_specs=pl.BlockSpec(memory_space=pl.ANY),
            scratch_shapes=[
                pltpu.SMEM((2, TOP_K, TOKEN_TILE), jnp.int32),
                pltpu.VMEM((EXPERT_BLOCK * PACK_SUBLANES, LANES), jnp.uint32),
                pltpu.SemaphoreType.DMA((2,)),
                pltpu.SemaphoreType.DMA(()),
                pltpu.SemaphoreType.DMA(()),
            ],
        ),
        out_shape=jax.ShapeDtypeStruct((n_sorted_rows * PACK_SUBLANES, LANES), jnp.uint32),
        compiler_params=pltpu.CompilerParams(dimension_semantics=("arbitrary",), vmem_limit_bytes=VMEM_LIMIT),
        name=f"dispatch_l{layer}",
    )(fill_start, n_used, pos_tiles, x1p, x1p)


def _expert_kernel(be_ref, nu_ref, xs_ref, *refs):
    i = pl.program_id(0)
    n_slots = EXPERT_STEP_BLOCKS
    y_ref = refs[3 * n_slots]
    for sub in range(n_slots):
        wg_ref, wu_ref, wd_ref = refs[3 * sub:3 * sub + 3]
        wgu_bf, wd_bf = refs[3 * n_slots + 1 + 2 * sub:3 * n_slots + 3 + 2 * sub]
        blk = i * n_slots + sub
        row_base = sub * EXPERT_BLOCK

        @pl.when(blk < nu_ref[0])
        def _(blk=blk, row_base=row_base, wg_ref=wg_ref, wu_ref=wu_ref, wd_ref=wd_ref, wgu_bf=wgu_bf, wd_bf=wd_bf):
            new_expert = (i == 0) | (be_ref[blk] != be_ref[jnp.maximum(blk - n_slots, 0)])

            @pl.when(new_expert)
            def _():
                wgu_bf[:, 0:EXPERT_FF] = wg_ref[...].astype(BF16)
                wgu_bf[:, EXPERT_FF:2 * EXPERT_FF] = wu_ref[...].astype(BF16)
                wd_bf[...] = wd_ref[...].astype(BF16)

            for r0 in range(row_base, row_base + EXPERT_BLOCK, EXPERT_CHUNK):
                h = _dot(_load_packed_tokens(xs_ref, EXPERT_CHUNK, r0).astype(BF16), wgu_bf[...])
                g = h[:, 0:EXPERT_FF]
                a = (g * _sigmoid(g) * h[:, EXPERT_FF:2 * EXPERT_FF]).astype(BF16)
                _store_packed_tokens(y_ref, _dot(a, wd_bf[...]), r0)

        @pl.when(blk >= nu_ref[0])
        def _(row_base=row_base):
            y_ref[row_base * PACK_SUBLANES:(row_base + EXPERT_BLOCK) * PACK_SUBLANES, :] = jnp.zeros(
                (EXPERT_BLOCK * PACK_SUBLANES, LANES), jnp.uint32)


def _experts(blk_expert, n_used, xs, w_gate, w_up, w_down, *, layer):
    nb = blk_expert.shape[0]
    n_slots = EXPERT_STEP_BLOCKS
    assert nb % n_slots == 0
    step_rows = n_slots * EXPERT_BLOCK
    w_specs = []
    for sub in range(n_slots):
        index = lambda i, be, nu, sub=sub: (layer * N_EXPERTS + be[jnp.minimum(i * n_slots + sub, nu[0] - 1)], 0, 0)
        w_specs += [pl.BlockSpec((None, D_MODEL, EXPERT_FF), index), pl.BlockSpec((None, D_MODEL, EXPERT_FF), index),
                    pl.BlockSpec((None, EXPERT_FF, D_MODEL), index)]
    in_row_spec = pl.BlockSpec((step_rows * PACK_SUBLANES, LANES),
                               lambda i, be, nu: (jnp.minimum(i, (nu[0] - 1) // n_slots), 0))
    return pl.pallas_call(
        _expert_kernel,
        grid_spec=pltpu.PrefetchScalarGridSpec(
            num_scalar_prefetch=2,
            grid=(nb // n_slots,),
            in_specs=[in_row_spec] + w_specs,
            out_specs=pl.BlockSpec((step_rows * PACK_SUBLANES, LANES), lambda i, be, nu: (i, 0)),
            scratch_shapes=[
                pltpu.VMEM((D_MODEL, 2 * EXPERT_FF), BF16),
                pltpu.VMEM((EXPERT_FF, D_MODEL), BF16),
            ] * n_slots,
        ),
        out_shape=jax.ShapeDtypeStruct((nb * EXPERT_BLOCK * PACK_SUBLANES, LANES), jnp.uint32),
        compiler_params=pltpu.CompilerParams(dimension_semantics=("arbitrary",), vmem_limit_bytes=VMEM_LIMIT),
        name=f"experts_l{layer}",
    )(blk_expert, n_used, xs, *([w_gate, w_up, w_down] * n_slots))


def _combine_kernel(pos_hbm, y_hbm, x_ref, gate_ref, wsg_ref, wsu_ref, wsd_ref, ln2g_ref, ln2b_ref, o_ref,
                    idx_smem, ybuf, sem_idx, sem_y, *, n_tiles):
    i = pl.program_id(0)
    slot = lax.rem(i, 2)
    nslot = 1 - slot
    rows_per_slot = TOP_K * TOKEN_TILE

    def idx_copy(tile, sl):
        return pltpu.make_async_copy(pos_hbm.at[tile], idx_smem.at[sl], sem_idx.at[sl])

    def start_gather(sl):
        def fetch(k, j):
            pltpu.make_async_copy(_packed_token(y_hbm, idx_smem[sl, k, j]),
                                  _packed_token(ybuf, sl * rows_per_slot + k * TOKEN_TILE + j), sem_y.at[sl]
                                  ).start(priority=j % 2)
        _for_each_assignment(fetch)

    @pl.when(i == 0)
    def _():
        idx_copy(0, 0).start()
        idx_copy(0, 0).wait()
        start_gather(0)
        if n_tiles > 1:
            idx_copy(1, 1).start()

    @pl.when(i + 1 < n_tiles)
    def _():
        idx_copy(i + 1, nslot).wait()
        start_gather(nslot)

    @pl.when(i + 2 < n_tiles)
    def _():
        idx_copy(i + 2, slot).start()

    x = _load_token_tiles(x_ref, TOKEN_TILE)
    xb = x.astype(BF16)
    g = _dot(xb, wsg_ref[...])
    a = (g * _sigmoid(g) * _dot(xb, wsu_ref[...])).astype(BF16)
    shared = _dot(a, wsd_ref[...])

    base = slot * rows_per_slot
    slot_rows = ybuf.at[pl.ds(pl.multiple_of(base * PACK_SUBLANES, PACK_SUBLANES), rows_per_slot * PACK_SUBLANES), :]
    pltpu.make_async_copy(slot_rows, slot_rows, sem_y.at[slot]).wait()
    gate = gate_ref[...]
    routed = None
    for k in range(TOP_K):
        term = gate[:, k:k + 1] * _load_packed_tokens(ybuf, TOKEN_TILE, base + k * TOKEN_TILE)
        routed = term if routed is None else routed + term
    o_ref[...] = _layer_norm(ALPHA * x + (routed + shared), ln2g_ref[...], ln2b_ref[...])


def _combine(pos_tiles, y_sorted, x1, gate_t, w_sg, w_su, w_sd, ln2_g, ln2_b, *, layer):
    T = x1.shape[0] // TILE_SUBLANES
    tc = TOKEN_TILE
    n_tiles = T // tc
    per_layer = lambda *shape: pl.BlockSpec((None,) + shape, lambda i: (layer,) + (0,) * len(shape))
    return pl.pallas_call(
        functools.partial(_combine_kernel, n_tiles=n_tiles),
        grid=(n_tiles,),
        in_specs=[
            pl.BlockSpec(memory_space=pl.ANY),
            pl.BlockSpec(memory_space=pl.ANY),
            pl.BlockSpec((tc * TILE_SUBLANES, LANES), lambda i: (i, 0)),
            pl.BlockSpec((tc, TOP_K), lambda i: (i, 0)),
            per_layer(D_MODEL, EXPERT_FF),
            per_layer(D_MODEL, EXPERT_FF),
            per_layer(EXPERT_FF, D_MODEL),
            per_layer(1, D_MODEL),
            per_layer(1, D_MODEL),
        ],
        out_specs=pl.BlockSpec((tc, D_MODEL), lambda i: (i, 0)),
        out_shape=jax.ShapeDtypeStruct((T, D_MODEL), F32),
        scratch_shapes=[
            pltpu.SMEM((2, TOP_K, TOKEN_TILE), jnp.int32),
            pltpu.VMEM((2 * TOP_K * tc * PACK_SUBLANES, LANES), jnp.uint32),
            pltpu.SemaphoreType.DMA((2,)),
            pltpu.SemaphoreType.DMA((2,)),
        ],
        compiler_params=pltpu.CompilerParams(dimension_semantics=("arbitrary",), vmem_limit_bytes=VMEM_LIMIT),
        name=f"combine_l{layer}",
    )(pos_tiles, y_sorted, x1, gate_t, w_sg, w_su, w_sd, ln2_g, ln2_b)


def _dispatch_plan(ek, rk, counts_f):
    T = ek.shape[1]
    blk = EXPERT_BLOCK
    nb = (T * TOP_K) // blk + N_EXPERTS
    experts = jnp.arange(N_EXPERTS, dtype=jnp.int32)
    counts = counts_f[:, 0].astype(jnp.int32)
    pcounts = (counts + blk - 1) // blk * blk
    pends = jnp.sum(jnp.where(experts[None, :] <= experts[:, None], pcounts[None, :], 0), axis=1)
    pstarts = pends - pcounts
    pos = jnp.sum(jnp.where(ek[None] == experts[:, None, None], pstarts[:, None, None], 0), axis=0) + rk
    n_used = (pends[-1] // blk).reshape(1)
    block_row0 = jnp.arange(nb, dtype=jnp.int32) * blk
    blk_expert = jnp.minimum(jnp.sum((pends[None, :] <= block_row0[:, None]).astype(jnp.int32), axis=1), N_EXPERTS - 1)
    fill_start = pstarts + counts
    pos_tiles = pos.reshape(TOP_K, T // TOKEN_TILE, TOKEN_TILE).transpose(1, 0, 2)
    return blk_expert, n_used, fill_start, pos_tiles, nb * blk


def kernel(x, ln0_g, ln0_b, w_in, b_in, w_pool, pool_scale, attn_sinks, w_br_pool, w_br_attn, w_out, ln1_g, ln1_b,
           w_router, router_bias, w_exp_gate, w_exp_up, w_exp_down, w_sh_gate, w_sh_up, w_sh_down, ln2_g, ln2_b):
    B, S, D = x.shape
    depth = w_in.shape[0]
    row = lambda a: a.reshape(a.shape[0], 1, a.shape[1])
    w_in_b, w_pool_b = w_in.astype(BF16), w_pool.astype(BF16)
    w_brp_b, w_bra_b, w_out_b = w_br_pool.astype(BF16), w_br_attn.astype(BF16), w_out.astype(BF16)
    w_router_t = jnp.swapaxes(w_router, 1, 2).astype(BF16)
    bias_col = router_bias.reshape(depth, N_EXPERTS, 1)
    w_sg_b, w_su_b, w_sd_b = w_sh_gate.astype(BF16), w_sh_up.astype(BF16), w_sh_down.astype(BF16)
    w_eg = w_exp_gate.reshape(depth * N_EXPERTS, D, EXPERT_FF)
    w_eu = w_exp_up.reshape(depth * N_EXPERTS, D, EXPERT_FF)
    w_ed = w_exp_down.reshape(depth * N_EXPERTS, EXPERT_FF, D)
    ln0_g2, ln0_b2 = ln0_g.reshape(1, D), ln0_b.reshape(1, D)

    mixer_weights = (w_in_b, row(b_in), w_pool_b, row(pool_scale), w_brp_b, w_bra_b, w_out_b, row(ln1_g), row(ln1_b))
    shared_weights = (w_sg_b, w_su_b, w_sd_b, row(ln2_g), row(ln2_b))
    n_s = S // MIXER_TILE
    x1, x1p = _first_mixer(x, attn_sinks, ln0_g2, ln0_b2, mixer_weights)
    for l in range(depth):
        ek, rk, gk, counts_f = _router(x1, w_router_t, bias_col, layer=l)
        blk_expert, n_used, fill_start, pos_tiles, n_sorted = _dispatch_plan(ek, rk, counts_f)
        xs = _dispatch(fill_start, n_used, pos_tiles, x1p, n_sorted + EXPERT_BLOCK, layer=l)
        y_sorted = _experts(blk_expert, n_used, xs, w_eg, w_eu, w_ed, layer=l)
        if l + 1 < depth:
            x1, x1p = _combine_mixer(pos_tiles, y_sorted, x1, gk.T, shared_weights, attn_sinks, mixer_weights,
                                     layer=l + 1, n_s=n_s)
    return _combine(pos_tiles, y_sorted, x1, gk.T, *shared_weights, layer=depth - 1).reshape(B, S, D)
```

```python
import functools

import jax
import jax.numpy as jnp
from jax import lax
from jax.experimental import pallas as pl
from jax.experimental.pallas import tpu as pltpu

D_MODEL = 1024
DEPTH = 4
POOL_GROUPS = 4
POOL_GROUP_CH = 128
POOL_WIDTH = POOL_GROUPS * POOL_GROUP_CH
POOL_WINDOWS = (2, 4, 8, 16)
POOL_HALO = 16
N_Q_HEADS = 8
N_KV_HEADS = 2
HEAD_DIM = 64
Q_WIDTH = N_Q_HEADS * HEAD_DIM
KV_WIDTH = N_KV_HEADS * HEAD_DIM
WINDOW = 128
ATT_BLOCK = 128
D_IN = POOL_WIDTH + Q_WIDTH + 2 * KV_WIDTH + 2 * D_MODEL
QKV_START = POOL_WIDTH
GATE_START = POOL_WIDTH + Q_WIDTH + 2 * KV_WIDTH
N_EXPERTS = 64
EXPERT_FF = 256
TOP_K = 8
N_EXPERT_GROUPS = 8
EXPERTS_PER_GROUP = N_EXPERTS // N_EXPERT_GROUPS
TOPK_GROUPS = 4
ROUTED_SCALE = 2.5
ALPHA = (2.0 * DEPTH) ** 0.25
LN_EPS = 1e-5
SEGMENT_HEADS = (0, 2, 1, 3, 4, 6, 5, 7)
ALIBI_SLOPES = tuple(float(2.0 ** (-8.0 * h / N_Q_HEADS)) for h in range(1, N_Q_HEADS + 1))

LANES = 128
SUBLANES = 8
TILE_SUBLANES = D_MODEL // LANES
PACK_SUBLANES = TILE_SUBLANES // 2
MIXER_TILE = 256
ROUTER_TILE = 1024
EXPERT_BLOCK = 512
EXPERT_CHUNK = 512
EXPERT_STEP_BLOCKS = 4
TOKEN_TILE = 128
VMEM_LIMIT = 48 * 1024 * 1024
FUSED_VMEM_LIMIT = 56 * 1024 * 1024

BF16 = jnp.bfloat16
F32 = jnp.float32

assert TILE_SUBLANES == SUBLANES and TOP_K == SUBLANES and TOKEN_TILE == LANES


def _dot(a, b):
    return jnp.dot(a, b, preferred_element_type=F32)


def _dot_nt(a, b):
    return lax.dot_general(a, b, (((1,), (1,)), ((), ())), preferred_element_type=F32)


def _layer_norm(x, g, b):
    mu = jnp.mean(x, axis=-1, keepdims=True)
    xc = x - mu
    var = jnp.mean(xc * xc, axis=-1, keepdims=True)
    return xc * lax.rsqrt(var + LN_EPS) * g + b


def _sigmoid(x):
    return 0.5 * jnp.tanh(0.5 * x) + 0.5


def _load_token_tiles(ref, rows, row0=0):
    return jnp.concatenate(
        [ref[pl.ds(row0 * TILE_SUBLANES + j, rows, stride=TILE_SUBLANES), :] for j in range(TILE_SUBLANES)], axis=1)


def _store_token_tiles(ref, value, row0=0):
    for j in range(TILE_SUBLANES):
        ref[pl.ds(row0 * TILE_SUBLANES + j, value.shape[0], stride=TILE_SUBLANES), :] = (
            value[:, j * LANES:(j + 1) * LANES])


def _token_tile(ref, row):
    return ref.at[pl.ds(pl.multiple_of(row * TILE_SUBLANES, TILE_SUBLANES), TILE_SUBLANES), :]


def _bf16_bits(x):
    return lax.bitcast_convert_type(x.astype(BF16).astype(F32), jnp.uint32)


def _store_packed_tokens(ref, value, row0=0):
    n = value.shape[0]
    for c in range(PACK_SUBLANES):
        lo = _bf16_bits(value[:, c * LANES:(c + 1) * LANES])
        hi = _bf16_bits(value[:, (c + PACK_SUBLANES) * LANES:(c + PACK_SUBLANES + 1) * LANES])
        ref[pl.ds(row0 * PACK_SUBLANES + c, n, stride=PACK_SUBLANES), :] = (
            lax.shift_right_logical(lo, jnp.uint32(16)) | hi)


def _load_packed_tokens(ref, rows, row0=0):
    los, his = [], []
    for c in range(PACK_SUBLANES):
        w = ref[pl.ds(row0 * PACK_SUBLANES + c, rows, stride=PACK_SUBLANES), :]
        los.append(lax.bitcast_convert_type(lax.shift_left(w, jnp.uint32(16)), F32))
        his.append(lax.bitcast_convert_type(w & jnp.uint32(0xFFFF0000), F32))
    return jnp.concatenate(los + his, axis=1)


def _packed_token(ref, row):
    return ref.at[pl.ds(pl.multiple_of(row * PACK_SUBLANES, PACK_SUBLANES), PACK_SUBLANES), :]


def _mixer_body(x, s, first, sinks_ref, w_in_ref, b_in_ref, w_pool_ref, pscale_ref, w_brp_ref, w_bra_ref, w_out_ref,
                ln1g_ref, ln1b_ref, o_ref, op_ref, ubuf, kvbuf, bias_tab, *, layer, tq):
    @pl.when(s == 0)
    def _():
        ubuf[0:POOL_HALO, :] = jnp.zeros((POOL_HALO, POOL_WIDTH), F32)
        kvbuf[0:ATT_BLOCK, :] = jnp.zeros((ATT_BLOCK, 8 * LANES), BF16)

    xb = x.astype(BF16)

    u = _dot(xb, w_in_ref[:, 0:POOL_WIDTH]) + b_in_ref[:, 0:POOL_WIDTH]
    ubuf[POOL_HALO:POOL_HALO + tq, :] = u
    pos = (s * tq + lax.broadcasted_iota(jnp.int32, (tq, 1), 0)).astype(F32)
    mixed_parts = []
    for g, w in enumerate(POOL_WINDOWS):
        sl = slice(g * POOL_GROUP_CH, (g + 1) * POOL_GROUP_CH)
        cur = ubuf[POOL_HALO:POOL_HALO + tq, sl]
        acc = cur
        for j in range(1, w):
            acc = acc + ubuf[POOL_HALO - j:POOL_HALO - j + tq, sl]
        inv_cnt = 1.0 / jnp.minimum(pos + 1.0, float(w))
        d = (acc * inv_cnt - cur).astype(BF16)
        mixed_parts.append(_dot(d, w_pool_ref[g]) * pscale_ref[:, sl])
    mixed = jnp.concatenate(mixed_parts, axis=1).astype(BF16)
    y_pool = _dot(mixed, w_brp_ref[...])
    ubuf[0:POOL_HALO, :] = ubuf[tq:tq + POOL_HALO, :]

    qkv = _dot(xb, w_in_ref[:, QKV_START:GATE_START]) + b_in_ref[:, QKV_START:GATE_START]
    q = (qkv[:, 0:Q_WIDTH] * (HEAD_DIM ** -0.5)).astype(BF16)
    k = qkv[:, Q_WIDTH:Q_WIDTH + KV_WIDTH]
    v = qkv[:, Q_WIDTH + KV_WIDTH:Q_WIDTH + 2 * KV_WIDTH]
    lo = lax.broadcasted_iota(jnp.int32, (tq, LANES), 1) < HEAD_DIM
    k_sw = pltpu.roll(k, HEAD_DIM, axis=1)
    v_sw = pltpu.roll(v, HEAD_DIM, axis=1)
    zero = jnp.zeros((tq, LANES), F32)
    slabs = (
        jnp.where(lo, k, zero), jnp.where(lo, zero, k_sw),
        jnp.where(lo, k_sw, zero), jnp.where(lo, zero, k),
        jnp.where(lo, v, zero), jnp.where(lo, zero, v_sw),
        jnp.where(lo, v_sw, zero), jnp.where(lo, zero, v),
    )
    for i, slab in enumerate(slabs):
        kvbuf[ATT_BLOCK:ATT_BLOCK + tq, i * LANES:(i + 1) * LANES] = slab.astype(BF16)

    @pl.when(first)
    def _():
        qi = lax.broadcasted_iota(jnp.int32, (ATT_BLOCK, 2 * ATT_BLOCK), 0)
        kj = lax.broadcasted_iota(jnp.int32, (ATT_BLOCK, 2 * ATT_BLOCK), 1)
        dist = ATT_BLOCK + qi - kj
        band_ok = (dist >= 0) & (dist < WINDOW)
        distf = dist.astype(F32)
        for i, h in enumerate(SEGMENT_HEADS):
            bias_tab[i * ATT_BLOCK:(i + 1) * ATT_BLOCK, :] = jnp.where(band_ok, -ALIBI_SLOPES[h] * distf, -jnp.inf)

    sink_col = jnp.concatenate([jnp.full((ATT_BLOCK, 1), sinks_ref[layer, h], F32) for h in SEGMENT_HEADS], axis=0)
    key_col = lax.broadcasted_iota(jnp.int32, (1, 2 * ATT_BLOCK), 1)
    o_blocks = []
    for qb in range(tq // ATT_BLOCK):
        r0 = qb * ATT_BLOCK
        first_key_pos = s * tq + r0 - ATT_BLOCK
        score_parts = []
        for hk in range(N_KV_HEADS):
            q_pairs = jnp.concatenate(
                [q[r0:r0 + ATT_BLOCK, (2 * hk + pj) * LANES:(2 * hk + pj + 1) * LANES] for pj in range(2)], axis=0)
            for half in range(2):
                k_slab = kvbuf[r0:r0 + 2 * ATT_BLOCK, (2 * hk + half) * LANES:(2 * hk + half + 1) * LANES]
                score_parts.append(_dot_nt(q_pairs, k_slab))
        sc = jnp.concatenate(score_parts, axis=0) + bias_tab[...]
        sc = jnp.where(key_col + first_key_pos < 0, -jnp.inf, sc)
        m = jnp.maximum(jnp.max(sc, axis=1, keepdims=True), sink_col)
        p = jnp.exp(sc - m).astype(BF16)
        den = _dot(p, jnp.ones((2 * ATT_BLOCK, LANES), BF16)) + jnp.exp(sink_col - m)
        inv_den = 1.0 / den
        o_pairs = []
        for hk in range(N_KV_HEADS):
            pv = None
            for half in range(2):
                rows = (2 * hk + half) * 2 * ATT_BLOCK
                v_slab = kvbuf[r0:r0 + 2 * ATT_BLOCK, (4 + 2 * hk + half) * LANES:(4 + 2 * hk + half + 1) * LANES]
                contrib = _dot(p[rows:rows + 2 * ATT_BLOCK], v_slab) * inv_den[rows:rows + 2 * ATT_BLOCK]
                pv = contrib if pv is None else pv + contrib
            o_pairs += [pv[0:ATT_BLOCK], pv[ATT_BLOCK:2 * ATT_BLOCK]]
        o_blocks.append(jnp.concatenate(o_pairs, axis=1))
    o = jnp.concatenate(o_blocks, axis=0).astype(BF16)
    y_attn = _dot(o, w_bra_ref[...])
    kvbuf[0:ATT_BLOCK, :] = kvbuf[tq:tq + ATT_BLOCK, :]

    gates = _dot(xb, w_in_ref[:, GATE_START:D_IN]) + b_in_ref[:, GATE_START:D_IN]
    merged = _sigmoid(gates[:, 0:D_MODEL]) * y_pool + _sigmoid(gates[:, D_MODEL:2 * D_MODEL]) * y_attn
    mix = _dot(merged.astype(BF16), w_out_ref[...])
    x1 = _layer_norm(ALPHA * x + mix, ln1g_ref[...], ln1b_ref[...])
    _store_token_tiles(o_ref, x1)
    _store_packed_tokens(op_ref, x1)


def _first_mixer_kernel(sinks_ref, x_ref, ln0g_ref, ln0b_ref, *refs, layer, tq, n_s):
    t = pl.program_id(0)
    x = _layer_norm(x_ref[...], ln0g_ref[...], ln0b_ref[...])
    _mixer_body(x, lax.rem(t, n_s), t == 0, sinks_ref, *refs, layer=layer, tq=tq)


def _combine_mixer_kernel(sinks_ref, pos_hbm, y_hbm, xprev_ref, gate_ref, wsg_ref, wsu_ref, wsd_ref, ln2g_ref, ln2b_ref,
                          w_in_ref, b_in_ref, w_pool_ref, pscale_ref, w_brp_ref, w_bra_ref, w_out_ref, ln1g_ref,
                          ln1b_ref, o_ref, op_ref, ubuf, kvbuf, bias_tab, idx_smem, ybuf, sem_idx, sem_y,
                          *, layer, tq, n_s, n_tiles):
    h = pl.program_id(0)
    gslot = lax.rem(h, 2)
    rows_per_slot = TOP_K * tq
    tiles_per_step = tq // TOKEN_TILE

    def idx_copy(tile):
        return pltpu.make_async_copy(pos_hbm.at[tile], idx_smem, sem_idx)

    @pl.when(h == 0)
    def _():
        idx_copy(0).start()

    @pl.when(h < n_tiles)
    def _():
        idx_copy(h).wait()
        for ti in range(tiles_per_step):
            def fetch(k, j, ti=ti):
                pltpu.make_async_copy(
                    _packed_token(y_hbm, idx_smem[ti * TOP_K + k, j]),
                    _packed_token(ybuf, gslot * rows_per_slot + k * tq + ti * TOKEN_TILE + j), sem_y.at[gslot]
                ).start(priority=j % 2)
            _for_each_assignment(fetch)

    @pl.when(h + 1 < n_tiles)
    def _():
        idx_copy(h + 1).start()

    @pl.when(h > 0)
    def _():
        t = h - 1
        base = (1 - gslot) * rows_per_slot
        slot_rows = ybuf.at[pl.ds(pl.multiple_of(base * PACK_SUBLANES, PACK_SUBLANES), rows_per_slot * PACK_SUBLANES), :]
        pltpu.make_async_copy(slot_rows, slot_rows, sem_y.at[1 - gslot]).wait()
        x1 = _load_token_tiles(xprev_ref, tq)
        x1b = x1.astype(BF16)
        g = _dot(x1b, wsg_ref[...])
        a = (g * _sigmoid(g) * _dot(x1b, wsu_ref[...])).astype(BF16)
        shared = _dot(a, wsd_ref[...])
        gate = gate_ref[...]
        routed = None
        for k in range(TOP_K):
            term = gate[:, k:k + 1] * _load_packed_tokens(ybuf, tq, base + k * tq)
            routed = term if routed is None else routed + term
        x = _layer_norm(ALPHA * x1 + (routed + shared), ln2g_ref[...], ln2b_ref[...])
        _mixer_body(x, lax.rem(t, n_s), t == 0, sinks_ref, w_in_ref, b_in_ref, w_pool_ref, pscale_ref, w_brp_ref,
                    w_bra_ref, w_out_ref, ln1g_ref, ln1b_ref, o_ref, op_ref, ubuf, kvbuf, bias_tab, layer=layer, tq=tq)


def _mixer_specs(layer):
    per_layer = lambda *shape: pl.BlockSpec((None,) + shape, lambda h: (layer,) + (0,) * len(shape))
    return [
        per_layer(D_MODEL, D_IN), per_layer(1, D_IN), per_layer(POOL_GROUPS, POOL_GROUP_CH, POOL_GROUP_CH),
        per_layer(1, POOL_WIDTH), per_layer(POOL_WIDTH, D_MODEL), per_layer(Q_WIDTH, D_MODEL),
        per_layer(D_MODEL, D_MODEL), per_layer(1, D_MODEL), per_layer(1, D_MODEL),
    ]


def _mixer_outputs(T, tq, tile_of_step):
    out_specs = [
        pl.BlockSpec((tq * TILE_SUBLANES, LANES), lambda h: (tile_of_step(h), 0)),
        pl.BlockSpec((tq * PACK_SUBLANES, LANES), lambda h: (tile_of_step(h), 0)),
    ]
    out_shape = [
        jax.ShapeDtypeStruct((T * TILE_SUBLANES, LANES), F32),
        jax.ShapeDtypeStruct((T * PACK_SUBLANES, LANES), jnp.uint32),
    ]
    scratch = [
        pltpu.VMEM((POOL_HALO + tq, POOL_WIDTH), F32),
        pltpu.VMEM((ATT_BLOCK + tq, 8 * LANES), BF16),
        pltpu.VMEM((N_Q_HEADS * ATT_BLOCK, 2 * ATT_BLOCK), F32),
    ]
    return out_specs, out_shape, scratch


def _first_mixer(x, sinks, ln0_g, ln0_b, mixer_weights):
    B, S, D = x.shape
    tq = MIXER_TILE
    n_s = S // tq
    out_specs, out_shape, scratch = _mixer_outputs(B * S, tq, lambda h: h)
    return pl.pallas_call(
        functools.partial(_first_mixer_kernel, layer=0, tq=tq, n_s=n_s),
        grid=(B * n_s,),
        in_specs=[
            pl.BlockSpec(memory_space=pltpu.SMEM),
            pl.BlockSpec((None, tq, D), lambda h: (h // n_s, h % n_s, 0)),
            pl.BlockSpec((1, D), lambda h: (0, 0)),
            pl.BlockSpec((1, D), lambda h: (0, 0)),
        ] + _mixer_specs(0),
        out_specs=out_specs,
        out_shape=out_shape,
        scratch_shapes=scratch,
        compiler_params=pltpu.CompilerParams(dimension_semantics=("arbitrary",), vmem_limit_bytes=VMEM_LIMIT),
        name="mixer_l0",
    )(sinks, x, ln0_g, ln0_b, *mixer_weights)


def _combine_mixer(pos_tiles, y_sorted, x1_prev, gate_t, shared_weights, sinks, mixer_weights, *, layer, n_s):
    tq = MIXER_TILE
    T = x1_prev.shape[0] // TILE_SUBLANES
    n_tiles = T // tq
    tiles_per_step = tq // TOKEN_TILE
    pos_steps = pos_tiles.reshape(n_tiles, tiles_per_step * TOP_K, TOKEN_TILE)
    tile_of_step = lambda h: jnp.maximum(h - 1, 0)
    prev = lambda *shape: pl.BlockSpec((None,) + shape, lambda h: (layer - 1,) + (0,) * len(shape))
    out_specs, out_shape, scratch = _mixer_outputs(T, tq, tile_of_step)
    return pl.pallas_call(
        functools.partial(_combine_mixer_kernel, layer=layer, tq=tq, n_s=n_s, n_tiles=n_tiles),
        grid=(n_tiles + 1,),
        in_specs=[
            pl.BlockSpec(memory_space=pltpu.SMEM),
            pl.BlockSpec(memory_space=pl.ANY),
            pl.BlockSpec(memory_space=pl.ANY),
            pl.BlockSpec((tq * TILE_SUBLANES, LANES), lambda h: (tile_of_step(h), 0)),
            pl.BlockSpec((tq, TOP_K), lambda h: (tile_of_step(h), 0)),
            prev(D_MODEL, EXPERT_FF), prev(D_MODEL, EXPERT_FF), prev(EXPERT_FF, D_MODEL), prev(1, D_MODEL), prev(1, D_MODEL),
        ] + _mixer_specs(layer),
        out_specs=out_specs,
        out_shape=out_shape,
        scratch_shapes=scratch + [
            pltpu.SMEM((tiles_per_step * TOP_K, TOKEN_TILE), jnp.int32),
            pltpu.VMEM((2 * TOP_K * tq * PACK_SUBLANES, LANES), jnp.uint32),
            pltpu.SemaphoreType.DMA(()),
            pltpu.SemaphoreType.DMA((2,)),
        ],
        compiler_params=pltpu.CompilerParams(dimension_semantics=("arbitrary",), vmem_limit_bytes=FUSED_VMEM_LIMIT),
        name=f"combine_mixer_l{layer}",
    )(sinks, pos_steps, y_sorted, x1_prev, gate_t, *shared_weights, *mixer_weights)


def _first_index_of_max(vals, iota, n):
    m = jnp.max(vals, axis=0, keepdims=True)
    idx = jnp.min(jnp.where(vals == m, iota, n), axis=0, keepdims=True)
    return m, idx


def _route(xb, wr_ref, bias_ref, ek_ref, rk_ref, gk_ref, cnt_ref, carry):
    tr = xb.shape[0]
    logits = _dot_nt(wr_ref[...], xb)
    scores = _sigmoid(logits)
    biased = scores + bias_ref[...]
    neg_inf = -jnp.inf

    io8 = lax.broadcasted_iota(jnp.int32, (EXPERTS_PER_GROUP, tr), 0)
    group_rows = []
    for g in range(N_EXPERT_GROUPS):
        blk = biased[g * EXPERTS_PER_GROUP:(g + 1) * EXPERTS_PER_GROUP]
        m1, i1 = _first_index_of_max(blk, io8, EXPERTS_PER_GROUP)
        m2 = jnp.max(jnp.where(io8 == i1, neg_inf, blk), axis=0, keepdims=True)
        group_rows.append(m1 + m2)
    gscore = jnp.concatenate(group_rows, axis=0)
    iog = lax.broadcasted_iota(jnp.int32, (N_EXPERT_GROUPS, tr), 0)
    keep = jnp.zeros((N_EXPERT_GROUPS, tr), F32)
    for _ in range(TOPK_GROUPS):
        _, gi = _first_index_of_max(gscore, iog, N_EXPERT_GROUPS)
        hit = iog == gi
        keep = jnp.where(hit, 1.0, keep)
        gscore = jnp.where(hit, neg_inf, gscore)
    masked = jnp.concatenate(
        [jnp.where(keep[g:g + 1] > 0.0, biased[g * EXPERTS_PER_GROUP:(g + 1) * EXPERTS_PER_GROUP], neg_inf)
         for g in range(N_EXPERT_GROUPS)], axis=0)

    ioe = lax.broadcasted_iota(jnp.int32, (N_EXPERTS, tr), 0)
    sel = jnp.zeros((N_EXPERTS, tr), F32)
    e_rows, s_rows, hits = [], [], []
    for _ in range(TOP_K):
        _, ei = _first_index_of_max(masked, ioe, N_EXPERTS)
        hit = ioe == ei
        sel = jnp.where(hit, 1.0, sel)
        masked = jnp.where(hit, neg_inf, masked)
        e_rows.append(ei)
        hits.append(hit)
        s_rows.append(jnp.sum(jnp.where(hit, scores, 0.0), axis=0, keepdims=True))
    sel_scores = jnp.concatenate(s_rows, axis=0)
    gk_ref[...] = sel_scores / jnp.sum(sel_scores, axis=0, keepdims=True) * ROUTED_SCALE
    ek_ref[...] = jnp.concatenate(e_rows, axis=0)

    before = (lax.broadcasted_iota(jnp.int32, (tr, tr), 0) < lax.broadcasted_iota(jnp.int32, (tr, tr), 1))
    prefix = _dot(sel.astype(BF16), jnp.where(before, 1.0, 0.0).astype(BF16))
    rank_full = prefix + carry[...]
    rk_ref[...] = jnp.concatenate(
        [jnp.sum(jnp.where(hit, rank_full, 0.0), axis=0, keepdims=True) for hit in hits], axis=0).astype(jnp.int32)
    total = carry[...] + jnp.sum(sel, axis=1, keepdims=True)
    carry[...] = total
    cnt_ref[...] = jnp.broadcast_to(total, (N_EXPERTS, LANES))


def _router_kernel(x_ref, wr_ref, bias_ref, ek_ref, rk_ref, gk_ref, cnt_ref, carry, *, tr):
    @pl.when(pl.program_id(0) == 0)
    def _():
        carry[...] = jnp.zeros_like(carry)

    _route(_load_token_tiles(x_ref, tr).astype(BF16), wr_ref, bias_ref, ek_ref, rk_ref, gk_ref, cnt_ref, carry)


def _router(x1, w_router_t, bias_col, *, layer):
    T = x1.shape[0] // TILE_SUBLANES
    tr = ROUTER_TILE
    row_spec = pl.BlockSpec((TOP_K, tr), lambda i: (0, i))
    return pl.pallas_call(
        functools.partial(_router_kernel, tr=tr),
        grid=(T // tr,),
        in_specs=[
            pl.BlockSpec((tr * TILE_SUBLANES, LANES), lambda i: (i, 0)),
            pl.BlockSpec((None, N_EXPERTS, D_MODEL), lambda i: (layer, 0, 0)),
            pl.BlockSpec((None, N_EXPERTS, 1), lambda i: (layer, 0, 0)),
        ],
        out_specs=[row_spec, row_spec, row_spec, pl.BlockSpec((N_EXPERTS, LANES), lambda i: (0, 0))],
        out_shape=[
            jax.ShapeDtypeStruct((TOP_K, T), jnp.int32),
            jax.ShapeDtypeStruct((TOP_K, T), jnp.int32),
            jax.ShapeDtypeStruct((TOP_K, T), F32),
            jax.ShapeDtypeStruct((N_EXPERTS, LANES), F32),
        ],
        scratch_shapes=[pltpu.VMEM((N_EXPERTS, 1), F32)],
        compiler_params=pltpu.CompilerParams(dimension_semantics=("arbitrary",), vmem_limit_bytes=VMEM_LIMIT),
        name=f"router_l{layer}",
    )(x1, w_router_t, bias_col)


def _for_each_assignment(fn):
    for k in range(TOP_K):
        for j in range(TOKEN_TILE):
            fn(k, j)


def _dispatch_kernel(fill_ref, nu_ref, pos_hbm, x_ref, xs_hbm, idx_smem, zbuf, sem_idx, sem_out, sem_fill,
                     *, n_tiles, n_blocks):
    i = pl.program_id(0)
    slot = lax.rem(i, 2)

    def idx_copy(tile, sl):
        return pltpu.make_async_copy(pos_hbm.at[tile], idx_smem.at[sl], sem_idx.at[sl])

    def fill_copy(row0, rows):
        n = rows * PACK_SUBLANES
        return pltpu.make_async_copy(
            zbuf.at[pl.ds(0, n), :], xs_hbm.at[pl.ds(pl.multiple_of(row0 * PACK_SUBLANES, PACK_SUBLANES), n), :],
            sem_fill)

    def pad_fill(e, wait):
        row0 = fill_ref[e]
        n_pad = (0 - row0) & (EXPERT_BLOCK - 1)
        piece = EXPERT_BLOCK // 2
        while piece >= 1:
            has = (n_pad & piece) != 0

            @pl.when(has)
            def _(row0=row0, piece=piece):
                fill_copy(0 if wait else row0, piece).wait() if wait else fill_copy(row0, piece).start()
            row0 = row0 + jnp.where(has, piece, 0)
            piece //= 2

    @pl.when(i == 0)
    def _():
        idx_copy(0, 0).start()
        zbuf[...] = jnp.zeros_like(zbuf)
        lax.fori_loop(0, N_EXPERTS, lambda e, c: (pad_fill(e, False), c)[1], 0)
        lax.fori_loop(nu_ref[0], n_blocks, lambda b, c: (fill_copy(b * EXPERT_BLOCK, EXPERT_BLOCK).start(), c)[1], 0)
        lax.fori_loop(0, N_EXPERTS, lambda e, c: (pad_fill(e, True), c)[1], 0)
        lax.fori_loop(nu_ref[0], n_blocks, lambda b, c: (fill_copy(0, EXPERT_BLOCK).wait(), c)[1], 0)

    idx_copy(i, slot).wait()

    @pl.when(i + 1 < n_tiles)
    def _():
        idx_copy(i + 1, 1 - slot).start()

    def send(k, j):
        pltpu.make_async_copy(_packed_token(x_ref, j), _packed_token(xs_hbm, idx_smem[slot, k, j]), sem_out
                              ).start(priority=j % 2)
    _for_each_assignment(send)

    for _ in range(TOP_K):
        pltpu.make_async_copy(x_ref, x_ref, sem_out).wait()


def _dispatch(fill_start, n_used, pos_tiles, x1p, n_sorted_rows, *, layer):
    T = x1p.shape[0] // PACK_SUBLANES
    n_tiles = T // TOKEN_TILE
    return pl.pallas_call(
        functools.partial(_dispatch_kernel, n_tiles=n_tiles, n_blocks=n_sorted_rows // EXPERT_BLOCK),
        grid_spec=pltpu.PrefetchScalarGridSpec(
            num_scalar_prefetch=2,
            grid=(n_tiles,),
            in_specs=[
                pl.BlockSpec(memory_space=pl.ANY),
                pl.BlockSpec((TOKEN_TILE * PACK_SUBLANES, LANES), lambda i, fill, nu: (i, 0)),
            ],
            out_specs=pl.BlockSpec(memory_space=pl.ANY),
            scratch_shapes=[
                pltpu.SMEM((2, TOP_K, TOKEN_TILE), jnp.int32),
                pltpu.VMEM((EXPERT_BLOCK * PACK_SUBLANES, LANES), jnp.uint32),
                pltpu.SemaphoreType.DMA((2,)),
                pltpu.SemaphoreType.DMA(()),
                pltpu.SemaphoreType.DMA(()),
            ],
        ),
        out_shape=jax.ShapeDtypeStruct((n_sorted_rows * PACK_SUBLANES, LANES), jnp.uint32),
        compiler_params=pltpu.CompilerParams(dimension_semantics=("arbitrary",), vmem_limit_bytes=VMEM_LIMIT),
        name=f"dispatch_l{layer}",
    )(fill_start, n_used, pos_tiles, x1p)


def _expert_kernel(be_ref, nu_ref, xs_ref, *refs):
    i = pl.program_id(0)
    n_slots = EXPERT_STEP_BLOCKS
    y_ref = refs[3 * n_slots]
    for sub in range(n_slots):
        wg_ref, wu_ref, wd_ref = refs[3 * sub:3 * sub + 3]
        wgu_bf, wd_bf = refs[3 * n_slots + 1 + 2 * sub:3 * n_slots + 3 + 2 * sub]
        blk = i * n_slots + sub
        row_base = sub * EXPERT_BLOCK

        @pl.when(blk < nu_ref[0])
        def _(blk=blk, row_base=row_base, wg_ref=wg_ref, wu_ref=wu_ref, wd_ref=wd_ref, wgu_bf=wgu_bf, wd_bf=wd_bf):
            new_expert = (i == 0) | (be_ref[blk] != be_ref[jnp.maximum(blk - n_slots, 0)])

            @pl.when(new_expert)
            def _():
                wgu_bf[:, 0:EXPERT_FF] = wg_ref[...].astype(BF16)
                wgu_bf[:, EXPERT_FF:2 * EXPERT_FF] = wu_ref[...].astype(BF16)
                wd_bf[...] = wd_ref[...].astype(BF16)

            for r0 in range(row_base, row_base + EXPERT_BLOCK, EXPERT_CHUNK):
                h = _dot(_load_packed_tokens(xs_ref, EXPERT_CHUNK, r0).astype(BF16), wgu_bf[...])
                g = h[:, 0:EXPERT_FF]
                a = (g * _sigmoid(g) * h[:, EXPERT_FF:2 * EXPERT_FF]).astype(BF16)
                _store_packed_tokens(y_ref, _dot(a, wd_bf[...]), r0)

        @pl.when(blk >= nu_ref[0])
        def _(row_base=row_base):
            y_ref[row_base * PACK_SUBLANES:(row_base + EXPERT_BLOCK) * PACK_SUBLANES, :] = jnp.zeros(
                (EXPERT_BLOCK * PACK_SUBLANES, LANES), jnp.uint32)


def _experts(blk_expert, n_used, xs, w_gate, w_up, w_down, *, layer):
    nb = blk_expert.shape[0]
    n_slots = EXPERT_STEP_BLOCKS
    assert nb % n_slots == 0
    step_rows = n_slots * EXPERT_BLOCK
    w_specs = []
    for sub in range(n_slots):
        index = lambda i, be, nu, sub=sub: (layer * N_EXPERTS + be[jnp.minimum(i * n_slots + sub, nu[0] - 1)], 0, 0)
        w_specs += [pl.BlockSpec((None, D_MODEL, EXPERT_FF), index), pl.BlockSpec((None, D_MODEL, EXPERT_FF), index),
                    pl.BlockSpec((None, EXPERT_FF, D_MODEL), index)]
    in_row_spec = pl.BlockSpec((step_rows * PACK_SUBLANES, LANES),
                               lambda i, be, nu: (jnp.minimum(i, (nu[0] - 1) // n_slots), 0))
    return pl.pallas_call(
        _expert_kernel,
        grid_spec=pltpu.PrefetchScalarGridSpec(
            num_scalar_prefetch=2,
            grid=(nb // n_slots,),
            in_specs=[in_row_spec] + w_specs,
            out_specs=pl.BlockSpec((step_rows * PACK_SUBLANES, LANES), lambda i, be, nu: (i, 0)),
            scratch_shapes=[
                pltpu.VMEM((D_MODEL, 2 * EXPERT_FF), BF16),
                pltpu.VMEM((EXPERT_FF, D_MODEL), BF16),
            ] * n_slots,
        ),
        out_shape=jax.ShapeDtypeStruct((nb * EXPERT_BLOCK * PACK_SUBLANES, LANES), jnp.uint32),
        compiler_params=pltpu.CompilerParams(dimension_semantics=("arbitrary",), vmem_limit_bytes=VMEM_LIMIT),
        name=f"experts_l{layer}",
    )(blk_expert, n_used, xs, *([w_gate, w_up, w_down] * n_slots))


def _combine_kernel(pos_hbm, y_hbm, x_ref, gate_ref, wsg_ref, wsu_ref, wsd_ref, ln2g_ref, ln2b_ref, o_ref,
                    idx_smem, ybuf, sem_idx, sem_y, *, n_tiles):
    i = pl.program_id(0)
    slot = lax.rem(i, 2)
    nslot = 1 - slot
    rows_per_slot = TOP_K * TOKEN_TILE

    def idx_copy(tile, sl):
        return pltpu.make_async_copy(pos_hbm.at[tile], idx_smem.at[sl], sem_idx.at[sl])

    def start_gather(sl):
        def fetch(k, j):
            pltpu.make_async_copy(_packed_token(y_hbm, idx_smem[sl, k, j]),
                                  _packed_token(ybuf, sl * rows_per_slot + k * TOKEN_TILE + j), sem_y.at[sl]
                                  ).start(priority=j % 2)
        _for_each_assignment(fetch)

    @pl.when(i == 0)
    def _():
        idx_copy(0, 0).start()
        idx_copy(0, 0).wait()
        start_gather(0)
        if n_tiles > 1:
            idx_copy(1, 1).start()

    @pl.when(i + 1 < n_tiles)
    def _():
        idx_copy(i + 1, nslot).wait()
        start_gather(nslot)

    @pl.when(i + 2 < n_tiles)
    def _():
        idx_copy(i + 2, slot).start()

    x = _load_token_tiles(x_ref, TOKEN_TILE)
    xb = x.astype(BF16)
    g = _dot(xb, wsg_ref[...])
    a = (g * _sigmoid(g) * _dot(xb, wsu_ref[...])).astype(BF16)
    shared = _dot(a, wsd_ref[...])

    base = slot * rows_per_slot
    slot_rows = ybuf.at[pl.ds(pl.multiple_of(base * PACK_SUBLANES, PACK_SUBLANES), rows_per_slot * PACK_SUBLANES), :]
    pltpu.make_async_copy(slot_rows, slot_rows, sem_y.at[slot]).wait()
    gate = gate_ref[...]
    routed = None
    for k in range(TOP_K):
        term = gate[:, k:k + 1] * _load_packed_tokens(ybuf, TOKEN_TILE, base + k * TOKEN_TILE)
        routed = term if routed is None else routed + term
    o_ref[...] = _layer_norm(ALPHA * x + (routed + shared), ln2g_ref[...], ln2b_ref[...])


def _combine(pos_tiles, y_sorted, x1, gate_t, w_sg, w_su, w_sd, ln2_g, ln2_b, *, layer):
    T = x1.shape[0] // TILE_SUBLANES
    tc = TOKEN_TILE
    n_tiles = T // tc
    per_layer = lambda *shape: pl.BlockSpec((None,) + shape, lambda i: (layer,) + (0,) * len(shape))
    return pl.pallas_call(
        functools.partial(_combine_kernel, n_tiles=n_tiles),
        grid=(n_tiles,),
        in_specs=[
            pl.BlockSpec(memory_space=pl.ANY),
            pl.BlockSpec(memory_space=pl.ANY),
            pl.BlockSpec((tc * TILE_SUBLANES, LANES), lambda i: (i, 0)),
            pl.BlockSpec((tc, TOP_K), lambda i: (i, 0)),
            per_layer(D_MODEL, EXPERT_FF),
            per_layer(D_MODEL, EXPERT_FF),
            per_layer(EXPERT_FF, D_MODEL),
            per_layer(1, D_MODEL),
            per_layer(1, D_MODEL),
        ],
        out_specs=pl.BlockSpec((tc, D_MODEL), lambda i: (i, 0)),
        out_shape=jax.ShapeDtypeStruct((T, D_MODEL), F32),
        scratch_shapes=[
            pltpu.SMEM((2, TOP_K, TOKEN_TILE), jnp.int32),
            pltpu.VMEM((2 * TOP_K * tc * PACK_SUBLANES, LANES), jnp.uint32),
            pltpu.SemaphoreType.DMA((2,)),
            pltpu.SemaphoreType.DMA((2,)),
        ],
        compiler_params=pltpu.CompilerParams(dimension_semantics=("arbitrary",), vmem_limit_bytes=VMEM_LIMIT),
        name=f"combine_l{layer}",
    )(pos_tiles, y_sorted, x1, gate_t, w_sg, w_su, w_sd, ln2_g, ln2_b)


def _dispatch_plan(ek, rk, counts_f):
    T = ek.shape[1]
    blk = EXPERT_BLOCK
    nb = (T * TOP_K) // blk + N_EXPERTS
    experts = jnp.arange(N_EXPERTS, dtype=jnp.int32)
    counts = counts_f[:, 0].astype(jnp.int32)
    pcounts = (counts + blk - 1) // blk * blk
    pends = jnp.sum(jnp.where(experts[None, :] <= experts[:, None], pcounts[None, :], 0), axis=1)
    pstarts = pends - pcounts
    pos = jnp.sum(jnp.where(ek[None] == experts[:, None, None], pstarts[:, None, None], 0), axis=0) + rk
    n_used = (pends[-1] // blk).reshape(1)
    block_row0 = jnp.arange(nb, dtype=jnp.int32) * blk
    blk_expert = jnp.minimum(jnp.sum((pends[None, :] <= block_row0[:, None]).astype(jnp.int32), axis=1), N_EXPERTS - 1)
    fill_start = pstarts + counts
    pos_tiles = pos.reshape(TOP_K, T // TOKEN_TILE, TOKEN_TILE).transpose(1, 0, 2)
    return blk_expert, n_used, fill_start, pos_tiles, nb * blk


def kernel(x, ln0_g, ln0_b, w_in, b_in, w_pool, pool_scale, attn_sinks, w_br_pool, w_br_attn, w_out, ln1_g, ln1_b,
           w_router, router_bias, w_exp_gate, w_exp_up, w_exp_down, w_sh_gate, w_sh_up, w_sh_down, ln2_g, ln2_b):
    B, S, D = x.shape
    depth = w_in.shape[0]
    row = lambda a: a.reshape(a.shape[0], 1, a.shape[1])
    w_in_b, w_pool_b = w_in.astype(BF16), w_pool.astype(BF16)
    w_brp_b, w_bra_b, w_out_b = w_br_pool.astype(BF16), w_br_attn.astype(BF16), w_out.astype(BF16)
    w_router_t = jnp.swapaxes(w_router, 1, 2).astype(BF16)
    bias_col = router_bias.reshape(depth, N_EXPERTS, 1)
    w_sg_b, w_su_b, w_sd_b = w_sh_gate.astype(BF16), w_sh_up.astype(BF16), w_sh_down.astype(BF16)
    w_eg = w_exp_gate.reshape(depth * N_EXPERTS, D, EXPERT_FF)
    w_eu = w_exp_up.reshape(depth * N_EXPERTS, D, EXPERT_FF)
    w_ed = w_exp_down.reshape(depth * N_EXPERTS, EXPERT_FF, D)
    ln0_g2, ln0_b2 = ln0_g.reshape(1, D), ln0_b.reshape(1, D)

    mixer_weights = (w_in_b, row(b_in), w_pool_b, row(pool_scale), w_brp_b, w_bra_b, w_out_b, row(ln1_g), row(ln1_b))
    shared_weights = (w_sg_b, w_su_b, w_sd_b, row(ln2_g), row(ln2_b))
    n_s = S // MIXER_TILE
    x1, x1p = _first_mixer(x, attn_sinks, ln0_g2, ln0_b2, mixer_weights)
    for l in range(depth):
        ek, rk, gk, counts_f = _router(x1, w_router_t, bias_col, layer=l)
        blk_expert, n_used, fill_start, pos_tiles, n_sorted = _dispatch_plan(ek, rk, counts_f)
        xs = _dispatch(fill_start, n_used, pos_tiles, x1p, n_sorted + EXPERT_BLOCK, layer=l)
        y_sorted = _experts(blk_expert, n_used, xs, w_eg, w_eu, w_ed, layer=l)
        if l + 1 < depth:
            x1, x1p = _combine_mixer(pos_tiles, y_sorted, x1, gk.T, shared_weights, attn_sinks, mixer_weights,
                                     layer=l + 1, n_s=n_s)
    return _combine(pos_tiles, y_sorted, x1, gk.T, *shared_weights, layer=depth - 1).reshape(B, S, D)
```

```python
import functools

import jax
import jax.numpy as jnp
from jax import lax
from jax.experimental import pallas as pl
from jax.experimental.pallas import tpu as pltpu

D_MODEL = 1024
DEPTH = 4
POOL_GROUPS = 4
POOL_GROUP_CH = 128
POOL_WIDTH = POOL_GROUPS * POOL_GROUP_CH
POOL_WINDOWS = (2, 4, 8, 16)
POOL_HALO = 16
N_Q_HEADS = 8
N_KV_HEADS = 2
HEAD_DIM = 64
Q_WIDTH = N_Q_HEADS * HEAD_DIM
KV_WIDTH = N_KV_HEADS * HEAD_DIM
WINDOW = 128
ATT_BLOCK = 128
D_IN = POOL_WIDTH + Q_WIDTH + 2 * KV_WIDTH + 2 * D_MODEL
QKV_START = POOL_WIDTH
GATE_START = POOL_WIDTH + Q_WIDTH + 2 * KV_WIDTH
N_EXPERTS = 64
EXPERT_FF = 256
TOP_K = 8
N_EXPERT_GROUPS = 8
EXPERTS_PER_GROUP = N_EXPERTS // N_EXPERT_GROUPS
TOPK_GROUPS = 4
ROUTED_SCALE = 2.5
ALPHA = (2.0 * DEPTH) ** 0.25
LN_EPS = 1e-5
SEGMENT_HEADS = (0, 2, 1, 3, 4, 6, 5, 7)
ALIBI_SLOPES = tuple(float(2.0 ** (-8.0 * h / N_Q_HEADS)) for h in range(1, N_Q_HEADS + 1))

LANES = 128
SUBLANES = 8
TILE_SUBLANES = D_MODEL // LANES
PACK_SUBLANES = TILE_SUBLANES // 2
MIXER_TILE = 256
ROUTER_TILE = 1024
EXPERT_BLOCK = 512
EXPERT_STEP_BLOCKS = 2
TOKEN_TILE = 128
VMEM_LIMIT = 48 * 1024 * 1024
FUSED_VMEM_LIMIT = 56 * 1024 * 1024

BF16 = jnp.bfloat16
F32 = jnp.float32

assert TILE_SUBLANES == SUBLANES and TOP_K == SUBLANES and TOKEN_TILE == LANES


def _dot(a, b):
    return jnp.dot(a, b, preferred_element_type=F32)


def _dot_nt(a, b):
    return lax.dot_general(a, b, (((1,), (1,)), ((), ())), preferred_element_type=F32)


def _layer_norm(x, g, b):
    mu = jnp.mean(x, axis=-1, keepdims=True)
    xc = x - mu
    var = jnp.mean(xc * xc, axis=-1, keepdims=True)
    return xc * lax.rsqrt(var + LN_EPS) * g + b


def _sigmoid(x):
    return 0.5 * jnp.tanh(0.5 * x) + 0.5


def _load_token_tiles(ref, rows, row0=0):
    return jnp.concatenate(
        [ref[pl.ds(row0 * TILE_SUBLANES + j, rows, stride=TILE_SUBLANES), :] for j in range(TILE_SUBLANES)], axis=1)


def _store_token_tiles(ref, value, row0=0):
    for j in range(TILE_SUBLANES):
        ref[pl.ds(row0 * TILE_SUBLANES + j, value.shape[0], stride=TILE_SUBLANES), :] = (
            value[:, j * LANES:(j + 1) * LANES])


def _bf16_bits(x):
    return lax.bitcast_convert_type(x.astype(BF16).astype(F32), jnp.uint32)


def _store_packed_tokens(ref, value, row0=0):
    n = value.shape[0]
    for c in range(PACK_SUBLANES):
        lo = _bf16_bits(value[:, c * LANES:(c + 1) * LANES])
        hi = _bf16_bits(value[:, (c + PACK_SUBLANES) * LANES:(c + PACK_SUBLANES + 1) * LANES])
        ref[pl.ds(row0 * PACK_SUBLANES + c, n, stride=PACK_SUBLANES), :] = (
            lax.shift_right_logical(lo, jnp.uint32(16)) | hi)


def _load_packed_tokens(ref, rows, row0=0):
    los, his = [], []
    for c in range(PACK_SUBLANES):
        w = ref[pl.ds(row0 * PACK_SUBLANES + c, rows, stride=PACK_SUBLANES), :]
        los.append(lax.bitcast_convert_type(lax.shift_left(w, jnp.uint32(16)), F32))
        his.append(lax.bitcast_convert_type(w & jnp.uint32(0xFFFF0000), F32))
    return jnp.concatenate(los + his, axis=1)


def _packed_token(ref, row):
    return ref.at[pl.ds(pl.multiple_of(row * PACK_SUBLANES, PACK_SUBLANES), PACK_SUBLANES), :]


def _mixer_body(x, s, first, sinks_ref, w_in_ref, b_in_ref, w_pool_ref, pscale_ref, w_brp_ref, w_bra_ref, w_out_ref,
                ln1g_ref, ln1b_ref, o_ref, op_ref, ubuf, kvbuf, bias_tab, *, layer, tq):
    @pl.when(s == 0)
    def _():
        ubuf[0:POOL_HALO, :] = jnp.zeros((POOL_HALO, POOL_WIDTH), F32)
        kvbuf[0:ATT_BLOCK, :] = jnp.zeros((ATT_BLOCK, 8 * LANES), BF16)

    xb = x.astype(BF16)

    u = _dot(xb, w_in_ref[:, 0:POOL_WIDTH]) + b_in_ref[:, 0:POOL_WIDTH]
    ubuf[POOL_HALO:POOL_HALO + tq, :] = u
    pos = (s * tq + lax.broadcasted_iota(jnp.int32, (tq, 1), 0)).astype(F32)
    mixed_parts = []
    for g, w in enumerate(POOL_WINDOWS):
        sl = slice(g * POOL_GROUP_CH, (g + 1) * POOL_GROUP_CH)
        cur = ubuf[POOL_HALO:POOL_HALO + tq, sl]
        acc = cur
        for j in range(1, w):
            acc = acc + ubuf[POOL_HALO - j:POOL_HALO - j + tq, sl]
        inv_cnt = 1.0 / jnp.minimum(pos + 1.0, float(w))
        d = (acc * inv_cnt - cur).astype(BF16)
        mixed_parts.append(_dot(d, w_pool_ref[g]) * pscale_ref[:, sl])
    mixed = jnp.concatenate(mixed_parts, axis=1).astype(BF16)
    y_pool = _dot(mixed, w_brp_ref[...])
    ubuf[0:POOL_HALO, :] = ubuf[tq:tq + POOL_HALO, :]

    qkv = _dot(xb, w_in_ref[:, QKV_START:GATE_START]) + b_in_ref[:, QKV_START:GATE_START]
    q = (qkv[:, 0:Q_WIDTH] * (HEAD_DIM ** -0.5)).astype(BF16)
    k = qkv[:, Q_WIDTH:Q_WIDTH + KV_WIDTH]
    v = qkv[:, Q_WIDTH + KV_WIDTH:Q_WIDTH + 2 * KV_WIDTH]
    lo = lax.broadcasted_iota(jnp.int32, (tq, LANES), 1) < HEAD_DIM
    k_sw = pltpu.roll(k, HEAD_DIM, axis=1)
    v_sw = pltpu.roll(v, HEAD_DIM, axis=1)
    zero = jnp.zeros((tq, LANES), F32)
    slabs = (
        jnp.where(lo, k, zero), jnp.where(lo, zero, k_sw),
        jnp.where(lo, k_sw, zero), jnp.where(lo, zero, k),
        jnp.where(lo, v, zero), jnp.where(lo, zero, v_sw),
        jnp.where(lo, v_sw, zero), jnp.where(lo, zero, v),
    )
    for i, slab in enumerate(slabs):
        kvbuf[ATT_BLOCK:ATT_BLOCK + tq, i * LANES:(i + 1) * LANES] = slab.astype(BF16)

    @pl.when(first)
    def _():
        qi = lax.broadcasted_iota(jnp.int32, (ATT_BLOCK, 2 * ATT_BLOCK), 0)
        kj = lax.broadcasted_iota(jnp.int32, (ATT_BLOCK, 2 * ATT_BLOCK), 1)
        dist = ATT_BLOCK + qi - kj
        band_ok = (dist >= 0) & (dist < WINDOW)
        distf = dist.astype(F32)
        for i, h in enumerate(SEGMENT_HEADS):
            bias_tab[i * ATT_BLOCK:(i + 1) * ATT_BLOCK, :] = jnp.where(band_ok, -ALIBI_SLOPES[h] * distf, -jnp.inf)

    sink_col = jnp.concatenate([jnp.full((ATT_BLOCK, 1), sinks_ref[layer, h], F32) for h in SEGMENT_HEADS], axis=0)
    key_col = lax.broadcasted_iota(jnp.int32, (1, 2 * ATT_BLOCK), 1)
    o_blocks = []
    for qb in range(tq // ATT_BLOCK):
        r0 = qb * ATT_BLOCK
        first_key_pos = s * tq + r0 - ATT_BLOCK
        score_parts = []
        for hk in range(N_KV_HEADS):
            q_pairs = jnp.concatenate(
                [q[r0:r0 + ATT_BLOCK, (2 * hk + pj) * LANES:(2 * hk + pj + 1) * LANES] for pj in range(2)], axis=0)
            for half in range(2):
                k_slab = kvbuf[r0:r0 + 2 * ATT_BLOCK, (2 * hk + half) * LANES:(2 * hk + half + 1) * LANES]
                score_parts.append(_dot_nt(q_pairs, k_slab))
        sc = jnp.concatenate(score_parts, axis=0) + bias_tab[...]
        sc = jnp.where(key_col + first_key_pos < 0, -jnp.inf, sc)
        m = jnp.maximum(jnp.max(sc, axis=1, keepdims=True), sink_col)
        p = jnp.exp(sc - m).astype(BF16)
        den = _dot(p, jnp.ones((2 * ATT_BLOCK, LANES), BF16)) + jnp.exp(sink_col - m)
        inv_den = 1.0 / den
        o_pairs = []
        for hk in range(N_KV_HEADS):
            pv = None
            for half in range(2):
                rows = (2 * hk + half) * 2 * ATT_BLOCK
                v_slab = kvbuf[r0:r0 + 2 * ATT_BLOCK, (4 + 2 * hk + half) * LANES:(4 + 2 * hk + half + 1) * LANES]
                contrib = _dot(p[rows:rows + 2 * ATT_BLOCK], v_slab) * inv_den[rows:rows + 2 * ATT_BLOCK]
                pv = contrib if pv is None else pv + contrib
            o_pairs += [pv[0:ATT_BLOCK], pv[ATT_BLOCK:2 * ATT_BLOCK]]
        o_blocks.append(jnp.concatenate(o_pairs, axis=1))
    o = jnp.concatenate(o_blocks, axis=0).astype(BF16)
    y_attn = _dot(o, w_bra_ref[...])
    kvbuf[0:ATT_BLOCK, :] = kvbuf[tq:tq + ATT_BLOCK, :]

    gates = _dot(xb, w_in_ref[:, GATE_START:D_IN]) + b_in_ref[:, GATE_START:D_IN]
    merged = _sigmoid(gates[:, 0:D_MODEL]) * y_pool + _sigmoid(gates[:, D_MODEL:2 * D_MODEL]) * y_attn
    mix = _dot(merged.astype(BF16), w_out_ref[...])
    x1 = _layer_norm(ALPHA * x + mix, ln1g_ref[...], ln1b_ref[...])
    _store_token_tiles(o_ref, x1)
    _store_packed_tokens(op_ref, x1)


def _first_mixer_kernel(sinks_ref, x_ref, ln0g_ref, ln0b_ref, *refs, layer, tq, n_s):
    t = pl.program_id(0)
    x = _layer_norm(x_ref[...], ln0g_ref[...], ln0b_ref[...])
    _mixer_body(x, lax.rem(t, n_s), t == 0, sinks_ref, *refs, layer=layer, tq=tq)


def _combine_mixer_kernel(sinks_ref, pos_hbm, y_hbm, xprev_ref, gate_ref, wsg_ref, wsu_ref, wsd_ref, ln2g_ref, ln2b_ref,
                          w_in_ref, b_in_ref, w_pool_ref, pscale_ref, w_brp_ref, w_bra_ref, w_out_ref, ln1g_ref,
                          ln1b_ref, o_ref, op_ref, ubuf, kvbuf, bias_tab, idx_smem, ybuf, sem_idx, sem_y,
                          *, layer, tq, n_s, n_tiles):
    h = pl.program_id(0)
    gslot = lax.rem(h, 2)
    rows_per_slot = TOP_K * tq
    tiles_per_step = tq // TOKEN_TILE

    def idx_copy(tile, sl):
        return pltpu.make_async_copy(pos_hbm.at[tile], idx_smem.at[sl], sem_idx.at[sl])

    @pl.when(h == 0)
    def _():
        idx_copy(0, 0).start()

    @pl.when(h < n_tiles)
    def _():
        idx_copy(h, gslot).wait()
        for ti in range(tiles_per_step):
            def fetch(k, j, ti=ti):
                pltpu.make_async_copy(
                    _packed_token(y_hbm, idx_smem[gslot, ti * TOP_K + k, j]),
                    _packed_token(ybuf, gslot * rows_per_slot + k * tq + ti * TOKEN_TILE + j), sem_y.at[gslot]
                ).start(priority=j % 2)
            _for_each_assignment(fetch)

    @pl.when(h + 1 < n_tiles)
    def _():
        idx_copy(h + 1, 1 - gslot).start()

    @pl.when(h > 0)
    def _():
        t = h - 1
        base = (1 - gslot) * rows_per_slot
        slot_rows = ybuf.at[pl.ds(pl.multiple_of(base * PACK_SUBLANES, PACK_SUBLANES), rows_per_slot * PACK_SUBLANES), :]
        pltpu.make_async_copy(slot_rows, slot_rows, sem_y.at[1 - gslot]).wait()
        x1 = _load_token_tiles(xprev_ref, tq)
        x1b = x1.astype(BF16)
        g = _dot(x1b, wsg_ref[...])
        a = (g * _sigmoid(g) * _dot(x1b, wsu_ref[...])).astype(BF16)
        shared = _dot(a, wsd_ref[...])
        gate = gate_ref[...]
        routed = None
        for k in range(TOP_K):
            term = gate[:, k:k + 1] * _load_packed_tokens(ybuf, tq, base + k * tq)
            routed = term if routed is None else routed + term
        x = _layer_norm(ALPHA * x1 + (routed + shared), ln2g_ref[...], ln2b_ref[...])
        _mixer_body(x, lax.rem(t, n_s), t == 0, sinks_ref, w_in_ref, b_in_ref, w_pool_ref, pscale_ref, w_brp_ref,
                    w_bra_ref, w_out_ref, ln1g_ref, ln1b_ref, o_ref, op_ref, ubuf, kvbuf, bias_tab, layer=layer, tq=tq)


def _mixer_specs(layer):
    per_layer = lambda *shape: pl.BlockSpec((None,) + shape, lambda h: (layer,) + (0,) * len(shape))
    return [
        per_layer(D_MODEL, D_IN), per_layer(1, D_IN), per_layer(POOL_GROUPS, POOL_GROUP_CH, POOL_GROUP_CH),
        per_layer(1, POOL_WIDTH), per_layer(POOL_WIDTH, D_MODEL), per_layer(Q_WIDTH, D_MODEL),
        per_layer(D_MODEL, D_MODEL), per_layer(1, D_MODEL), per_layer(1, D_MODEL),
    ]


def _mixer_outputs(T, tq, tile_of_step):
    out_specs = [
        pl.BlockSpec((tq * TILE_SUBLANES, LANES), lambda h: (tile_of_step(h), 0)),
        pl.BlockSpec((tq * PACK_SUBLANES, LANES), lambda h: (tile_of_step(h), 0)),
    ]
    out_shape = [
        jax.ShapeDtypeStruct((T * TILE_SUBLANES, LANES), F32),
        jax.ShapeDtypeStruct((T * PACK_SUBLANES, LANES), jnp.uint32),
    ]
    scratch = [
        pltpu.VMEM((POOL_HALO + tq, POOL_WIDTH), F32),
        pltpu.VMEM((ATT_BLOCK + tq, 8 * LANES), BF16),
        pltpu.VMEM((N_Q_HEADS * ATT_BLOCK, 2 * ATT_BLOCK), F32),
    ]
    return out_specs, out_shape, scratch


def _first_mixer(x, sinks, ln0_g, ln0_b, mixer_weights):
    B, S, D = x.shape
    tq = MIXER_TILE
    n_s = S // tq
    out_specs, out_shape, scratch = _mixer_outputs(B * S, tq, lambda h: h)
    return pl.pallas_call(
        functools.partial(_first_mixer_kernel, layer=0, tq=tq, n_s=n_s),
        grid=(B * n_s,),
        in_specs=[
            pl.BlockSpec(memory_space=pltpu.SMEM),
            pl.BlockSpec((None, tq, D), lambda h: (h // n_s, h % n_s, 0)),
            pl.BlockSpec((1, D), lambda h: (0, 0)),
            pl.BlockSpec((1, D), lambda h: (0, 0)),
        ] + _mixer_specs(0),
        out_specs=out_specs,
        out_shape=out_shape,
        scratch_shapes=scratch,
        compiler_params=pltpu.CompilerParams(dimension_semantics=("arbitrary",), vmem_limit_bytes=VMEM_LIMIT),
        name="mixer_l0",
    )(sinks, x, ln0_g, ln0_b, *mixer_weights)


def _combine_mixer(pos_tiles, y_sorted, x1_prev, gate_t, shared_weights, sinks, mixer_weights, *, layer, n_s):
    tq = MIXER_TILE
    T = x1_prev.shape[0] // TILE_SUBLANES
    n_tiles = T // tq
    tiles_per_step = tq // TOKEN_TILE
    pos_steps = pos_tiles.reshape(n_tiles, tiles_per_step * TOP_K, TOKEN_TILE)
    tile_of_step = lambda h: jnp.maximum(h - 1, 0)
    prev = lambda *shape: pl.BlockSpec((None,) + shape, lambda h: (layer - 1,) + (0,) * len(shape))
    out_specs, out_shape, scratch = _mixer_outputs(T, tq, tile_of_step)
    return pl.pallas_call(
        functools.partial(_combine_mixer_kernel, layer=layer, tq=tq, n_s=n_s, n_tiles=n_tiles),
        grid=(n_tiles + 1,),
        in_specs=[
            pl.BlockSpec(memory_space=pltpu.SMEM),
            pl.BlockSpec(memory_space=pl.ANY),
            pl.BlockSpec(memory_space=pl.ANY),
            pl.BlockSpec((tq * TILE_SUBLANES, LANES), lambda h: (tile_of_step(h), 0)),
            pl.BlockSpec((tq, TOP_K), lambda h: (tile_of_step(h), 0)),
            prev(D_MODEL, EXPERT_FF), prev(D_MODEL, EXPERT_FF), prev(EXPERT_FF, D_MODEL), prev(1, D_MODEL), prev(1, D_MODEL),
        ] + _mixer_specs(layer),
        out_specs=out_specs,
        out_shape=out_shape,
        scratch_shapes=scratch + [
            pltpu.SMEM((2, tiles_per_step * TOP_K, TOKEN_TILE), jnp.int32),
            pltpu.VMEM((2 * TOP_K * tq * PACK_SUBLANES, LANES), jnp.uint32),
            pltpu.SemaphoreType.DMA((2,)),
            pltpu.SemaphoreType.DMA((2,)),
        ],
        compiler_params=pltpu.CompilerParams(dimension_semantics=("arbitrary",), vmem_limit_bytes=FUSED_VMEM_LIMIT),
        name=f"combine_mixer_l{layer}",
    )(sinks, pos_steps, y_sorted, x1_prev, gate_t, *shared_weights, *mixer_weights)


def _first_index_of_max(vals, iota, n):
    m = jnp.max(vals, axis=0, keepdims=True)
    idx = jnp.min(jnp.where(vals == m, iota, n), axis=0, keepdims=True)
    return m, idx


def _route(xb, wr_ref, bias_ref, ek_ref, rk_ref, gk_ref, cnt_ref, carry):
    tr = xb.shape[0]
    logits = _dot_nt(wr_ref[...], xb)
    scores = _sigmoid(logits)
    biased = scores + bias_ref[...]
    neg_inf = -jnp.inf

    io8 = lax.broadcasted_iota(jnp.int32, (EXPERTS_PER_GROUP, tr), 0)
    group_rows = []
    for g in range(N_EXPERT_GROUPS):
        blk = biased[g * EXPERTS_PER_GROUP:(g + 1) * EXPERTS_PER_GROUP]
        m1, i1 = _first_index_of_max(blk, io8, EXPERTS_PER_GROUP)
        m2 = jnp.max(jnp.where(io8 == i1, neg_inf, blk), axis=0, keepdims=True)
        group_rows.append(m1 + m2)
    gscore = jnp.concatenate(group_rows, axis=0)
    iog = lax.broadcasted_iota(jnp.int32, (N_EXPERT_GROUPS, tr), 0)
    keep = jnp.zeros((N_EXPERT_GROUPS, tr), F32)
    for _ in range(TOPK_GROUPS):
        _, gi = _first_index_of_max(gscore, iog, N_EXPERT_GROUPS)
        hit = iog == gi
        keep = jnp.where(hit, 1.0, keep)
        gscore = jnp.where(hit, neg_inf, gscore)
    masked = jnp.concatenate(
        [jnp.where(keep[g:g + 1] > 0.0, biased[g * EXPERTS_PER_GROUP:(g + 1) * EXPERTS_PER_GROUP], neg_inf)
         for g in range(N_EXPERT_GROUPS)], axis=0)

    ioe = lax.broadcasted_iota(jnp.int32, (N_EXPERTS, tr), 0)
    sel = jnp.zeros((N_EXPERTS, tr), F32)
    e_rows, s_rows, hits = [], [], []
    for _ in range(TOP_K):
        _, ei = _first_index_of_max(masked, ioe, N_EXPERTS)
        hit = ioe == ei
        sel = jnp.where(hit, 1.0, sel)
        masked = jnp.where(hit, neg_inf, masked)
        e_rows.append(ei)
        hits.append(hit)
        s_rows.append(jnp.sum(jnp.where(hit, scores, 0.0), axis=0, keepdims=True))
    sel_scores = jnp.concatenate(s_rows, axis=0)
    gk_ref[...] = sel_scores / jnp.sum(sel_scores, axis=0, keepdims=True) * ROUTED_SCALE
    ek_ref[...] = jnp.concatenate(e_rows, axis=0)

    before = (lax.broadcasted_iota(jnp.int32, (tr, tr), 0) < lax.broadcasted_iota(jnp.int32, (tr, tr), 1))
    prefix = _dot(sel.astype(BF16), jnp.where(before, 1.0, 0.0).astype(BF16))
    rank_full = prefix + carry[...]
    rk_ref[...] = jnp.concatenate(
        [jnp.sum(jnp.where(hit, rank_full, 0.0), axis=0, keepdims=True) for hit in hits], axis=0).astype(jnp.int32)
    total = carry[...] + jnp.sum(sel, axis=1, keepdims=True)
    carry[...] = total
    cnt_ref[...] = jnp.broadcast_to(total, (N_EXPERTS, LANES))


def _router_kernel(x_ref, wr_ref, bias_ref, ek_ref, rk_ref, gk_ref, cnt_ref, carry, *, tr):
    @pl.when(pl.program_id(0) == 0)
    def _():
        carry[...] = jnp.zeros_like(carry)

    _route(_load_token_tiles(x_ref, tr).astype(BF16), wr_ref, bias_ref, ek_ref, rk_ref, gk_ref, cnt_ref, carry)


def _router(x1, w_router_t, bias_col, *, layer):
    T = x1.shape[0] // TILE_SUBLANES
    tr = ROUTER_TILE
    row_spec = pl.BlockSpec((TOP_K, tr), lambda i: (0, i))
    return pl.pallas_call(
        functools.partial(_router_kernel, tr=tr),
        grid=(T // tr,),
        in_specs=[
            pl.BlockSpec((tr * TILE_SUBLANES, LANES), lambda i: (i, 0)),
            pl.BlockSpec((None, N_EXPERTS, D_MODEL), lambda i: (layer, 0, 0)),
            pl.BlockSpec((None, N_EXPERTS, 1), lambda i: (layer, 0, 0)),
        ],
        out_specs=[row_spec, row_spec, row_spec, pl.BlockSpec((N_EXPERTS, LANES), lambda i: (0, 0))],
        out_shape=[
            jax.ShapeDtypeStruct((TOP_K, T), jnp.int32),
            jax.ShapeDtypeStruct((TOP_K, T), jnp.int32),
            jax.ShapeDtypeStruct((TOP_K, T), F32),
            jax.ShapeDtypeStruct((N_EXPERTS, LANES), F32),
        ],
        scratch_shapes=[pltpu.VMEM((N_EXPERTS, 1), F32)],
        compiler_params=pltpu.CompilerParams(dimension_semantics=("arbitrary",), vmem_limit_bytes=VMEM_LIMIT),
        name=f"router_l{layer}",
    )(x1, w_router_t, bias_col)


def _for_each_assignment(fn):
    for k in range(TOP_K):
        for j in range(TOKEN_TILE):
            fn(k, j)


def _dispatch_kernel(fill_ref, nu_ref, pos_hbm, x_ref, xs_hbm, idx_smem, zbuf, sem_idx, sem_out, sem_fill,
                     *, n_tiles, n_blocks):
    i = pl.program_id(0)
    slot = lax.rem(i, 2)

    def idx_copy(tile, sl):
        return pltpu.make_async_copy(pos_hbm.at[tile], idx_smem.at[sl], sem_idx.at[sl])

    def fill_copy(row0, rows):
        n = rows * PACK_SUBLANES
        return pltpu.make_async_copy(
            zbuf.at[pl.ds(0, n), :], xs_hbm.at[pl.ds(pl.multiple_of(row0 * PACK_SUBLANES, PACK_SUBLANES), n), :],
            sem_fill)

    def pad_fill(e, wait):
        row0 = fill_ref[e]
        n_pad = (0 - row0) & (EXPERT_BLOCK - 1)
        piece = EXPERT_BLOCK // 2
        while piece >= 1:
            has = (n_pad & piece) != 0

            @pl.when(has)
            def _(row0=row0, piece=piece):
                fill_copy(0 if wait else row0, piece).wait() if wait else fill_copy(row0, piece).start()
            row0 = row0 + jnp.where(has, piece, 0)
            piece //= 2

    @pl.when(i == 0)
    def _():
        idx_copy(0, 0).start()
        zbuf[...] = jnp.zeros_like(zbuf)
        lax.fori_loop(0, N_EXPERTS, lambda e, c: (pad_fill(e, False), c)[1], 0)
        lax.fori_loop(nu_ref[0], n_blocks, lambda b, c: (fill_copy(b * EXPERT_BLOCK, EXPERT_BLOCK).start(), c)[1], 0)
        lax.fori_loop(0, N_EXPERTS, lambda e, c: (pad_fill(e, True), c)[1], 0)
        lax.fori_loop(nu_ref[0], n_blocks, lambda b, c: (fill_copy(0, EXPERT_BLOCK).wait(), c)[1], 0)

    idx_copy(i, slot).wait()

    @pl.when(i + 1 < n_tiles)
    def _():
        idx_copy(i + 1, 1 - slot).start()

    def send(k, j):
        pltpu.make_async_copy(_packed_token(x_ref, j), _packed_token(xs_hbm, idx_smem[slot, k, j]), sem_out
                              ).start(priority=j % 2)
    _for_each_assignment(send)

    for _ in range(TOP_K):
        pltpu.make_async_copy(x_ref, x_ref, sem_out).wait()


def _dispatch(fill_start, n_used, pos_tiles, x1p, n_sorted_rows, *, layer):
    T = x1p.shape[0] // PACK_SUBLANES
    n_tiles = T // TOKEN_TILE
    return pl.pallas_call(
        functools.partial(_dispatch_kernel, n_tiles=n_tiles, n_blocks=n_sorted_rows // EXPERT_BLOCK),
        grid_spec=pltpu.PrefetchScalarGridSpec(
            num_scalar_prefetch=2,
            grid=(n_tiles,),
            in_specs=[
                pl.BlockSpec(memory_space=pl.ANY),
                pl.BlockSpec((TOKEN_TILE * PACK_SUBLANES, LANES), lambda i, fill, nu: (i, 0)),
            ],
            out_specs=pl.BlockSpec(memory_space=pl.ANY),
            scratch_shapes=[
                pltpu.SMEM((2, TOP_K, TOKEN_TILE), jnp.int32),
                pltpu.VMEM((EXPERT_BLOCK * PACK_SUBLANES, LANES), jnp.uint32),
                pltpu.SemaphoreType.DMA((2,)),
                pltpu.SemaphoreType.DMA(()),
                pltpu.SemaphoreType.DMA(()),
            ],
        ),
        out_shape=jax.ShapeDtypeStruct((n_sorted_rows * PACK_SUBLANES, LANES), jnp.uint32),
        compiler_params=pltpu.CompilerParams(dimension_semantics=("arbitrary",), vmem_limit_bytes=VMEM_LIMIT),
        name=f"dispatch_l{layer}",
    )(fill_start, n_used, pos_tiles, x1p)


def _expert_kernel(be_ref, nu_ref, xs_ref, *refs):
    i = pl.program_id(0)
    n_slots = EXPERT_STEP_BLOCKS
    y_ref = refs[3 * n_slots]
    for sub in range(n_slots):
        wg_ref, wu_ref, wd_ref = refs[3 * sub:3 * sub + 3]
        wgu_bf, wd_bf = refs[3 * n_slots + 1 + 2 * sub:3 * n_slots + 3 + 2 * sub]
        blk = i * n_slots + sub
        row_base = sub * EXPERT_BLOCK

        @pl.when(blk < nu_ref[0])
        def _(blk=blk, row_base=row_base, wg_ref=wg_ref, wu_ref=wu_ref, wd_ref=wd_ref, wgu_bf=wgu_bf, wd_bf=wd_bf):
            new_expert = (i == 0) | (be_ref[blk] != be_ref[jnp.maximum(blk - n_slots, 0)])

            @pl.when(new_expert)
            def _():
                wgu_bf[:, 0:EXPERT_FF] = wg_ref[...].astype(BF16)
                wgu_bf[:, EXPERT_FF:2 * EXPERT_FF] = wu_ref[...].astype(BF16)
                wd_bf[...] = wd_ref[...].astype(BF16)

            h = _dot(_load_packed_tokens(xs_ref, EXPERT_BLOCK, row_base).astype(BF16), wgu_bf[...])
            g = h[:, 0:EXPERT_FF]
            a = (g * _sigmoid(g) * h[:, EXPERT_FF:2 * EXPERT_FF]).astype(BF16)
            _store_packed_tokens(y_ref, _dot(a, wd_bf[...]), row_base)

        @pl.when(blk >= nu_ref[0])
        def _(row_base=row_base):
            y_ref[row_base * PACK_SUBLANES:(row_base + EXPERT_BLOCK) * PACK_SUBLANES, :] = jnp.zeros(
                (EXPERT_BLOCK * PACK_SUBLANES, LANES), jnp.uint32)


def _experts(blk_expert, n_used, xs, w_gate, w_up, w_down, *, layer):
    nb = blk_expert.shape[0]
    n_slots = EXPERT_STEP_BLOCKS
    assert nb % n_slots == 0
    step_rows = n_slots * EXPERT_BLOCK
    w_specs = []
    for sub in range(n_slots):
        index = lambda i, be, nu, sub=sub: (layer * N_EXPERTS + be[jnp.minimum(i * n_slots + sub, nu[0] - 1)], 0, 0)
        w_specs += [pl.BlockSpec((None, D_MODEL, EXPERT_FF), index), pl.BlockSpec((None, D_MODEL, EXPERT_FF), index),
                    pl.BlockSpec((None, EXPERT_FF, D_MODEL), index)]
    in_row_spec = pl.BlockSpec((step_rows * PACK_SUBLANES, LANES),
                               lambda i, be, nu: (jnp.minimum(i, (nu[0] - 1) // n_slots), 0))
    return pl.pallas_call(
        _expert_kernel,
        grid_spec=pltpu.PrefetchScalarGridSpec(
            num_scalar_prefetch=2,
            grid=(nb // n_slots,),
            in_specs=[in_row_spec] + w_specs,
            out_specs=pl.BlockSpec((step_rows * PACK_SUBLANES, LANES), lambda i, be, nu: (i, 0)),
            scratch_shapes=[
                pltpu.VMEM((D_MODEL, 2 * EXPERT_FF), BF16),
                pltpu.VMEM((EXPERT_FF, D_MODEL), BF16),
            ] * n_slots,
        ),
        out_shape=jax.ShapeDtypeStruct((nb * EXPERT_BLOCK * PACK_SUBLANES, LANES), jnp.uint32),
        compiler_params=pltpu.CompilerParams(dimension_semantics=("arbitrary",), vmem_limit_bytes=VMEM_LIMIT),
        name=f"experts_l{layer}",
    )(blk_expert, n_used, xs, *([w_gate, w_up, w_down] * n_slots))


def _combine_kernel(pos_hbm, y_hbm, x_ref, gate_ref, wsg_ref, wsu_ref, wsd_ref, ln2g_ref, ln2b_ref, o_ref,
                    idx_smem, ybuf, sem_idx, sem_y, *, n_tiles):
    i = pl.program_id(0)
    slot = lax.rem(i, 2)
    nslot = 1 - slot
    rows_per_slot = TOP_K * TOKEN_TILE

    def idx_copy(tile, sl):
        return pltpu.make_async_copy(pos_hbm.at[tile], idx_smem.at[sl], sem_idx.at[sl])

    def start_gather(sl):
        def fetch(k, j):
            pltpu.make_async_copy(_packed_token(y_hbm, idx_smem[sl, k, j]),
                                  _packed_token(ybuf, sl * rows_per_slot + k * TOKEN_TILE + j), sem_y.at[sl]
                                  ).start(priority=j % 2)
        _for_each_assignment(fetch)

    @pl.when(i == 0)
    def _():
        idx_copy(0, 0).start()
        idx_copy(0, 0).wait()
        start_gather(0)
        if n_tiles > 1:
            idx_copy(1, 1).start()

    @pl.when(i + 1 < n_tiles)
    def _():
        idx_copy(i + 1, nslot).wait()
        start_gather(nslot)

    @pl.when(i + 2 < n_tiles)
    def _():
        idx_copy(i + 2, slot).start()

    x = _load_token_tiles(x_ref, TOKEN_TILE)
    xb = x.astype(BF16)
    g = _dot(xb, wsg_ref[...])
    a = (g * _sigmoid(g) * _dot(xb, wsu_ref[...])).astype(BF16)
    shared = _dot(a, wsd_ref[...])

    base = slot * rows_per_slot
    slot_rows = ybuf.at[pl.ds(pl.multiple_of(base * PACK_SUBLANES, PACK_SUBLANES), rows_per_slot * PACK_SUBLANES), :]
    pltpu.make_async_copy(slot_rows, slot_rows, sem_y.at[slot]).wait()
    gate = gate_ref[...]
    routed = None
    for k in range(TOP_K):
        term = gate[:, k:k + 1] * _load_packed_tokens(ybuf, TOKEN_TILE, base + k * TOKEN_TILE)
        routed = term if routed is None else routed + term
    o_ref[...] = _layer_norm(ALPHA * x + (routed + shared), ln2g_ref[...], ln2b_ref[...])


def _combine(pos_tiles, y_sorted, x1, gate_t, w_sg, w_su, w_sd, ln2_g, ln2_b, *, layer):
    T = x1.shape[0] // TILE_SUBLANES
    tc = TOKEN_TILE
    n_tiles = T // tc
    per_layer = lambda *shape: pl.BlockSpec((None,) + shape, lambda i: (layer,) + (0,) * len(shape))
    return pl.pallas_call(
        functools.partial(_combine_kernel, n_tiles=n_tiles),
        grid=(n_tiles,),
        in_specs=[
            pl.BlockSpec(memory_space=pl.ANY),
            pl.BlockSpec(memory_space=pl.ANY),
            pl.BlockSpec((tc * TILE_SUBLANES, LANES), lambda i: (i, 0)),
            pl.BlockSpec((tc, TOP_K), lambda i: (i, 0)),
            per_layer(D_MODEL, EXPERT_FF),
            per_layer(D_MODEL, EXPERT_FF),
            per_layer(EXPERT_FF, D_MODEL),
            per_layer(1, D_MODEL),
            per_layer(1, D_MODEL),
        ],
        out_specs=pl.BlockSpec((tc, D_MODEL), lambda i: (i, 0)),
        out_shape=jax.ShapeDtypeStruct((T, D_MODEL), F32),
        scratch_shapes=[
            pltpu.SMEM((2, TOP_K, TOKEN_TILE), jnp.int32),
            pltpu.VMEM((2 * TOP_K * tc * PACK_SUBLANES, LANES), jnp.uint32),
            pltpu.SemaphoreType.DMA((2,)),
            pltpu.SemaphoreType.DMA((2,)),
        ],
        compiler_params=pltpu.CompilerParams(dimension_semantics=("arbitrary",), vmem_limit_bytes=VMEM_LIMIT),
        name=f"combine_l{layer}",
    )(pos_tiles, y_sorted, x1, gate_t, w_sg, w_su, w_sd, ln2_g, ln2_b)


def _dispatch_plan(ek, rk, counts_f):
    T = ek.shape[1]
    blk = EXPERT_BLOCK
    nb = (T * TOP_K) // blk + N_EXPERTS
    experts = jnp.arange(N_EXPERTS, dtype=jnp.int32)
    counts = counts_f[:, 0].astype(jnp.int32)
    pcounts = (counts + blk - 1) // blk * blk
    pends = jnp.sum(jnp.where(experts[None, :] <= experts[:, None], pcounts[None, :], 0), axis=1)
    pstarts = pends - pcounts
    pos = jnp.sum(jnp.where(ek[None] == experts[:, None, None], pstarts[:, None, None], 0), axis=0) + rk
    n_used = (pends[-1] // blk).reshape(1)
    block_row0 = jnp.arange(nb, dtype=jnp.int32) * blk
    blk_expert = jnp.minimum(jnp.sum((pends[None, :] <= block_row0[:, None]).astype(jnp.int32), axis=1), N_EXPERTS - 1)
    fill_start = pstarts + counts
    pos_tiles = pos.reshape(TOP_K, T // TOKEN_TILE, TOKEN_TILE).transpose(1, 0, 2)
    return blk_expert, n_used, fill_start, pos_tiles, nb * blk


def kernel(x, ln0_g, ln0_b, w_in, b_in, w_pool, pool_scale, attn_sinks, w_br_pool, w_br_attn, w_out, ln1_g, ln1_b,
           w_router, router_bias, w_exp_gate, w_exp_up, w_exp_down, w_sh_gate, w_sh_up, w_sh_down, ln2_g, ln2_b):
    B, S, D = x.shape
    depth = w_in.shape[0]
    row = lambda a: a.reshape(a.shape[0], 1, a.shape[1])
    w_in_b, w_pool_b = w_in.astype(BF16), w_pool.astype(BF16)
    w_brp_b, w_bra_b, w_out_b = w_br_pool.astype(BF16), w_br_attn.astype(BF16), w_out.astype(BF16)
    w_router_t = jnp.swapaxes(w_router, 1, 2).astype(BF16)
    bias_col = router_bias.reshape(depth, N_EXPERTS, 1)
    w_sg_b, w_su_b, w_sd_b = w_sh_gate.astype(BF16), w_sh_up.astype(BF16), w_sh_down.astype(BF16)
    w_eg = w_exp_gate.reshape(depth * N_EXPERTS, D, EXPERT_FF)
    w_eu = w_exp_up.reshape(depth * N_EXPERTS, D, EXPERT_FF)
    w_ed = w_exp_down.reshape(depth * N_EXPERTS, EXPERT_FF, D)
    ln0_g2, ln0_b2 = ln0_g.reshape(1, D), ln0_b.reshape(1, D)

    mixer_weights = (w_in_b, row(b_in), w_pool_b, row(pool_scale), w_brp_b, w_bra_b, w_out_b, row(ln1_g), row(ln1_b))
    shared_weights = (w_sg_b, w_su_b, w_sd_b, row(ln2_g), row(ln2_b))
    n_s = S // MIXER_TILE
    x1, x1p = _first_mixer(x, attn_sinks, ln0_g2, ln0_b2, mixer_weights)
    for l in range(depth):
        ek, rk, gk, counts_f = _router(x1, w_router_t, bias_col, layer=l)
        blk_expert, n_used, fill_start, pos_tiles, n_sorted = _dispatch_plan(ek, rk, counts_f)
        xs = _dispatch(fill_start, n_used, pos_tiles, x1p, n_sorted + EXPERT_BLOCK, layer=l)
        y_sorted = _experts(blk_expert, n_used, xs, w_eg, w_eu, w_ed, layer=l)
        if l + 1 < depth:
            x1, x1p = _combine_mixer(pos_tiles, y_sorted, x1, gk.T, shared_weights, attn_sinks, mixer_weights,
                                     layer=l + 1, n_s=n_s)
    return _combine(pos_tiles, y_sorted, x1, gk.T, *shared_weights, layer=depth - 1).reshape(B, S, D)
```

```python
import functools

import jax
import jax.numpy as jnp
from jax import lax
from jax.experimental import pallas as pl
from jax.experimental.pallas import tpu as pltpu

D_MODEL = 1024
DEPTH = 4
POOL_GROUPS = 4
POOL_GROUP_CH = 128
POOL_WIDTH = POOL_GROUPS * POOL_GROUP_CH
POOL_WINDOWS = (2, 4, 8, 16)
POOL_HALO = 16
N_Q_HEADS = 8
N_KV_HEADS = 2
HEAD_DIM = 64
Q_WIDTH = N_Q_HEADS * HEAD_DIM
KV_WIDTH = N_KV_HEADS * HEAD_DIM
WINDOW = 128
ATT_BLOCK = 128
D_IN = POOL_WIDTH + Q_WIDTH + 2 * KV_WIDTH + 2 * D_MODEL
QKV_START = POOL_WIDTH
GATE_START = POOL_WIDTH + Q_WIDTH + 2 * KV_WIDTH
N_EXPERTS = 64
EXPERT_FF = 256
TOP_K = 8
N_EXPERT_GROUPS = 8
EXPERTS_PER_GROUP = N_EXPERTS // N_EXPERT_GROUPS
TOPK_GROUPS = 4
ROUTED_SCALE = 2.5
ALPHA = (2.0 * DEPTH) ** 0.25
LN_EPS = 1e-5
SEGMENT_HEADS = (0, 2, 1, 3, 4, 6, 5, 7)
ALIBI_SLOPES = tuple(float(2.0 ** (-8.0 * h / N_Q_HEADS)) for h in range(1, N_Q_HEADS + 1))

LANES = 128
SUBLANES = 8
TILE_SUBLANES = D_MODEL // LANES
PACK_SUBLANES = TILE_SUBLANES // 2
MIXER_TILE = 256
ROUTER_TILE = 1024
EXPERT_BLOCK = 512
GATHER_CHUNKS = 16
EXPERT_STEP_BLOCKS = 2
TOKEN_TILE = 128
VMEM_LIMIT = 48 * 1024 * 1024
FUSED_VMEM_LIMIT = 56 * 1024 * 1024

BF16 = jnp.bfloat16
F32 = jnp.float32

assert TILE_SUBLANES == SUBLANES and TOP_K == SUBLANES and TOKEN_TILE == LANES


def _dot(a, b):
    return jnp.dot(a, b, preferred_element_type=F32)


def _dot_nt(a, b):
    return lax.dot_general(a, b, (((1,), (1,)), ((), ())), preferred_element_type=F32)


def _layer_norm(x, g, b):
    mu = jnp.mean(x, axis=-1, keepdims=True)
    xc = x - mu
    var = jnp.mean(xc * xc, axis=-1, keepdims=True)
    return xc * lax.rsqrt(var + LN_EPS) * g + b


def _sigmoid(x):
    return 0.5 * jnp.tanh(0.5 * x) + 0.5


def _load_token_tiles(ref, rows, row0=0):
    return jnp.concatenate(
        [ref[pl.ds(row0 * TILE_SUBLANES + j, rows, stride=TILE_SUBLANES), :] for j in range(TILE_SUBLANES)], axis=1)


def _store_token_tiles(ref, value, row0=0):
    for j in range(TILE_SUBLANES):
        ref[pl.ds(row0 * TILE_SUBLANES + j, value.shape[0], stride=TILE_SUBLANES), :] = (
            value[:, j * LANES:(j + 1) * LANES])


def _bf16_bits(x):
    return lax.bitcast_convert_type(x.astype(BF16).astype(F32), jnp.uint32)


def _store_packed_tokens(ref, value, row0=0):
    n = value.shape[0]
    for c in range(PACK_SUBLANES):
        lo = _bf16_bits(value[:, c * LANES:(c + 1) * LANES])
        hi = _bf16_bits(value[:, (c + PACK_SUBLANES) * LANES:(c + PACK_SUBLANES + 1) * LANES])
        ref[pl.ds(row0 * PACK_SUBLANES + c, n, stride=PACK_SUBLANES), :] = (
            lax.shift_right_logical(lo, jnp.uint32(16)) | hi)


def _load_packed_tokens(ref, rows, row0=0):
    los, his = [], []
    for c in range(PACK_SUBLANES):
        w = ref[pl.ds(row0 * PACK_SUBLANES + c, rows, stride=PACK_SUBLANES), :]
        los.append(lax.bitcast_convert_type(lax.shift_left(w, jnp.uint32(16)), F32))
        his.append(lax.bitcast_convert_type(w & jnp.uint32(0xFFFF0000), F32))
    return jnp.concatenate(los + his, axis=1)


def _packed_token(ref, row):
    return ref.at[pl.ds(pl.multiple_of(row * PACK_SUBLANES, PACK_SUBLANES), PACK_SUBLANES), :]


def _mixer_body(x, s, first, sinks_ref, w_in_ref, b_in_ref, w_pool_ref, pscale_ref, w_brp_ref, w_bra_ref, w_out_ref,
                ln1g_ref, ln1b_ref, o_ref, op_ref, ubuf, kvbuf, bias_tab, *, layer, tq, between_phases=lambda: None):
    @pl.when(s == 0)
    def _():
        ubuf[0:POOL_HALO, :] = jnp.zeros((POOL_HALO, POOL_WIDTH), F32)
        kvbuf[0:ATT_BLOCK, :] = jnp.zeros((ATT_BLOCK, 8 * LANES), BF16)

    xb = x.astype(BF16)

    u = _dot(xb, w_in_ref[:, 0:POOL_WIDTH]) + b_in_ref[:, 0:POOL_WIDTH]
    ubuf[POOL_HALO:POOL_HALO + tq, :] = u
    between_phases()
    pos = (s * tq + lax.broadcasted_iota(jnp.int32, (tq, 1), 0)).astype(F32)
    mixed_parts = []
    for g, w in enumerate(POOL_WINDOWS):
        sl = slice(g * POOL_GROUP_CH, (g + 1) * POOL_GROUP_CH)
        cur = ubuf[POOL_HALO:POOL_HALO + tq, sl]
        acc = cur
        for j in range(1, w):
            acc = acc + ubuf[POOL_HALO - j:POOL_HALO - j + tq, sl]
        inv_cnt = 1.0 / jnp.minimum(pos + 1.0, float(w))
        d = (acc * inv_cnt - cur).astype(BF16)
        mixed_parts.append(_dot(d, w_pool_ref[g]) * pscale_ref[:, sl])
        between_phases()
    mixed = jnp.concatenate(mixed_parts, axis=1).astype(BF16)
    y_pool = _dot(mixed, w_brp_ref[...])
    ubuf[0:POOL_HALO, :] = ubuf[tq:tq + POOL_HALO, :]
    between_phases()

    qkv = _dot(xb, w_in_ref[:, QKV_START:GATE_START]) + b_in_ref[:, QKV_START:GATE_START]
    q = (qkv[:, 0:Q_WIDTH] * (HEAD_DIM ** -0.5)).astype(BF16)
    k = qkv[:, Q_WIDTH:Q_WIDTH + KV_WIDTH]
    v = qkv[:, Q_WIDTH + KV_WIDTH:Q_WIDTH + 2 * KV_WIDTH]
    lo = lax.broadcasted_iota(jnp.int32, (tq, LANES), 1) < HEAD_DIM
    k_sw = pltpu.roll(k, HEAD_DIM, axis=1)
    v_sw = pltpu.roll(v, HEAD_DIM, axis=1)
    zero = jnp.zeros((tq, LANES), F32)
    slabs = (
        jnp.where(lo, k, zero), jnp.where(lo, zero, k_sw),
        jnp.where(lo, k_sw, zero), jnp.where(lo, zero, k),
        jnp.where(lo, v, zero), jnp.where(lo, zero, v_sw),
        jnp.where(lo, v_sw, zero), jnp.where(lo, zero, v),
    )
    for i, slab in enumerate(slabs):
        kvbuf[ATT_BLOCK:ATT_BLOCK + tq, i * LANES:(i + 1) * LANES] = slab.astype(BF16)
    between_phases()

    @pl.when(first)
    def _():
        qi = lax.broadcasted_iota(jnp.int32, (ATT_BLOCK, 2 * ATT_BLOCK), 0)
        kj = lax.broadcasted_iota(jnp.int32, (ATT_BLOCK, 2 * ATT_BLOCK), 1)
        dist = ATT_BLOCK + qi - kj
        band_ok = (dist >= 0) & (dist < WINDOW)
        distf = dist.astype(F32)
        for i, h in enumerate(SEGMENT_HEADS):
            bias_tab[i * ATT_BLOCK:(i + 1) * ATT_BLOCK, :] = jnp.where(band_ok, -ALIBI_SLOPES[h] * distf, -jnp.inf)

    sink_col = jnp.concatenate([jnp.full((ATT_BLOCK, 1), sinks_ref[layer, h], F32) for h in SEGMENT_HEADS], axis=0)
    key_col = lax.broadcasted_iota(jnp.int32, (1, 2 * ATT_BLOCK), 1)
    o_blocks = []
    for qb in range(tq // ATT_BLOCK):
        r0 = qb * ATT_BLOCK
        first_key_pos = s * tq + r0 - ATT_BLOCK
        score_parts = []
        for hk in range(N_KV_HEADS):
            q_pairs = jnp.concatenate(
                [q[r0:r0 + ATT_BLOCK, (2 * hk + pj) * LANES:(2 * hk + pj + 1) * LANES] for pj in range(2)], axis=0)
            for half in range(2):
                k_slab = kvbuf[r0:r0 + 2 * ATT_BLOCK, (2 * hk + half) * LANES:(2 * hk + half + 1) * LANES]
                score_parts.append(_dot_nt(q_pairs, k_slab))
        sc = jnp.concatenate(score_parts, axis=0) + bias_tab[...]
        between_phases()
        sc = jnp.where(key_col + first_key_pos < 0, -jnp.inf, sc)
        m = jnp.maximum(jnp.max(sc, axis=1, keepdims=True), sink_col)
        p = jnp.exp(sc - m).astype(BF16)
        den = _dot(p, jnp.ones((2 * ATT_BLOCK, LANES), BF16)) + jnp.exp(sink_col - m)
        inv_den = 1.0 / den
        between_phases()
        o_pairs = []
        for hk in range(N_KV_HEADS):
            pv = None
            for half in range(2):
                rows = (2 * hk + half) * 2 * ATT_BLOCK
                v_slab = kvbuf[r0:r0 + 2 * ATT_BLOCK, (4 + 2 * hk + half) * LANES:(4 + 2 * hk + half + 1) * LANES]
                contrib = _dot(p[rows:rows + 2 * ATT_BLOCK], v_slab) * inv_den[rows:rows + 2 * ATT_BLOCK]
                pv = contrib if pv is None else pv + contrib
            o_pairs += [pv[0:ATT_BLOCK], pv[ATT_BLOCK:2 * ATT_BLOCK]]
        o_blocks.append(jnp.concatenate(o_pairs, axis=1))
        between_phases()
    o = jnp.concatenate(o_blocks, axis=0).astype(BF16)
    y_attn = _dot(o, w_bra_ref[...])
    kvbuf[0:ATT_BLOCK, :] = kvbuf[tq:tq + ATT_BLOCK, :]
    between_phases()

    gates = _dot(xb, w_in_ref[:, GATE_START:D_IN]) + b_in_ref[:, GATE_START:D_IN]
    merged = _sigmoid(gates[:, 0:D_MODEL]) * y_pool + _sigmoid(gates[:, D_MODEL:2 * D_MODEL]) * y_attn
    between_phases()
    mix = _dot(merged.astype(BF16), w_out_ref[...])
    between_phases()
    x1 = _layer_norm(ALPHA * x + mix, ln1g_ref[...], ln1b_ref[...])
    _store_token_tiles(o_ref, x1)
    _store_packed_tokens(op_ref, x1)


def _first_mixer_kernel(sinks_ref, x_ref, ln0g_ref, ln0b_ref, *refs, layer, tq, n_s):
    t = pl.program_id(0)
    x = _layer_norm(x_ref[...], ln0g_ref[...], ln0b_ref[...])
    _mixer_body(x, lax.rem(t, n_s), t == 0, sinks_ref, *refs, layer=layer, tq=tq)


def _combine_mixer_kernel(sinks_ref, pos_hbm, y_hbm, xprev_ref, gate_ref, wsg_ref, wsu_ref, wsd_ref, ln2g_ref, ln2b_ref,
                          w_in_ref, b_in_ref, w_pool_ref, pscale_ref, w_brp_ref, w_bra_ref, w_out_ref, ln1g_ref,
                          ln1b_ref, o_ref, op_ref, ubuf, kvbuf, bias_tab, idx_smem, ybuf, sem_idx, sem_y,
                          *, layer, tq, n_s, n_tiles):
    h = pl.program_id(0)
    gslot = lax.rem(h, 2)
    rows_per_slot = TOP_K * tq
    tiles_per_step = tq // TOKEN_TILE
    assignments = [(ti, k, j) for ti in range(tiles_per_step) for k in range(TOP_K) for j in range(TOKEN_TILE)]
    chunk = len(assignments) // GATHER_CHUNKS
    issued = [0]

    def idx_copy(tile, sl):
        return pltpu.make_async_copy(pos_hbm.at[tile], idx_smem.at[sl], sem_idx.at[sl])

    def issue_next_chunk():
        c = issued[0]
        if c == GATHER_CHUNKS:
            return
        issued[0] = c + 1

        @pl.when(h < n_tiles)
        def _():
            if c == 0:
                idx_copy(h, gslot).wait()
            for ti, k, j in assignments[c * chunk:(c + 1) * chunk]:
                pltpu.make_async_copy(
                    _packed_token(y_hbm, idx_smem[gslot, ti * TOP_K + k, j]),
                    _packed_token(ybuf, gslot * rows_per_slot + k * tq + ti * TOKEN_TILE + j), sem_y.at[gslot]
                ).start(priority=j % 2)

    @pl.when(h == 0)
    def _():
        idx_copy(0, 0).start()
        ybuf[rows_per_slot * PACK_SUBLANES:2 * rows_per_slot * PACK_SUBLANES, :] = jnp.zeros(
            (rows_per_slot * PACK_SUBLANES, LANES), jnp.uint32)

    @pl.when(h + 1 < n_tiles)
    def _():
        idx_copy(h + 1, 1 - gslot).start()

    t = jnp.maximum(h - 1, 0)
    base = (1 - gslot) * rows_per_slot

    @pl.when(h > 0)
    def _():
        slot_rows = ybuf.at[pl.ds(pl.multiple_of(base * PACK_SUBLANES, PACK_SUBLANES), rows_per_slot * PACK_SUBLANES), :]
        pltpu.make_async_copy(slot_rows, slot_rows, sem_y.at[1 - gslot]).wait()

    issue_next_chunk()
    x1 = _load_token_tiles(xprev_ref, tq)
    x1b = x1.astype(BF16)
    g = _dot(x1b, wsg_ref[...])
    a = (g * _sigmoid(g) * _dot(x1b, wsu_ref[...])).astype(BF16)
    shared = _dot(a, wsd_ref[...])
    issue_next_chunk()
    gate = gate_ref[...]
    routed = None
    for k in range(TOP_K):
        term = gate[:, k:k + 1] * _load_packed_tokens(ybuf, tq, base + k * tq)
        routed = term if routed is None else routed + term
        if k % 2 == 1:
            issue_next_chunk()
    x = _layer_norm(ALPHA * x1 + (routed + shared), ln2g_ref[...], ln2b_ref[...])
    _mixer_body(x, lax.rem(t, n_s), t == 0, sinks_ref, w_in_ref, b_in_ref, w_pool_ref, pscale_ref, w_brp_ref,
                w_bra_ref, w_out_ref, ln1g_ref, ln1b_ref, o_ref, op_ref, ubuf, kvbuf, bias_tab, layer=layer, tq=tq,
                between_phases=issue_next_chunk)
    while issued[0] < GATHER_CHUNKS:
        issue_next_chunk()


def _mixer_specs(layer):
    per_layer = lambda *shape: pl.BlockSpec((None,) + shape, lambda h: (layer,) + (0,) * len(shape))
    return [
        per_layer(D_MODEL, D_IN), per_layer(1, D_IN), per_layer(POOL_GROUPS, POOL_GROUP_CH, POOL_GROUP_CH),
        per_layer(1, POOL_WIDTH), per_layer(POOL_WIDTH, D_MODEL), per_layer(Q_WIDTH, D_MODEL),
        per_layer(D_MODEL, D_MODEL), per_layer(1, D_MODEL), per_layer(1, D_MODEL),
    ]


def _mixer_outputs(T, tq, tile_of_step):
    out_specs = [
        pl.BlockSpec((tq * TILE_SUBLANES, LANES), lambda h: (tile_of_step(h), 0)),
        pl.BlockSpec((tq * PACK_SUBLANES, LANES), lambda h: (tile_of_step(h), 0)),
    ]
    out_shape = [
        jax.ShapeDtypeStruct((T * TILE_SUBLANES, LANES), F32),
        jax.ShapeDtypeStruct((T * PACK_SUBLANES, LANES), jnp.uint32),
    ]
    scratch = [
        pltpu.VMEM((POOL_HALO + tq, POOL_WIDTH), F32),
        pltpu.VMEM((ATT_BLOCK + tq, 8 * LANES), BF16),
        pltpu.VMEM((N_Q_HEADS * ATT_BLOCK, 2 * ATT_BLOCK), F32),
    ]
    return out_specs, out_shape, scratch


def _first_mixer(x, sinks, ln0_g, ln0_b, mixer_weights):
    B, S, D = x.shape
    tq = MIXER_TILE
    n_s = S // tq
    out_specs, out_shape, scratch = _mixer_outputs(B * S, tq, lambda h: h)
    return pl.pallas_call(
        functools.partial(_first_mixer_kernel, layer=0, tq=tq, n_s=n_s),
        grid=(B * n_s,),
        in_specs=[
            pl.BlockSpec(memory_space=pltpu.SMEM),
            pl.BlockSpec((None, tq, D), lambda h: (h // n_s, h % n_s, 0)),
            pl.BlockSpec((1, D), lambda h: (0, 0)),
            pl.BlockSpec((1, D), lambda h: (0, 0)),
        ] + _mixer_specs(0),
        out_specs=out_specs,
        out_shape=out_shape,
        scratch_shapes=scratch,
        compiler_params=pltpu.CompilerParams(dimension_semantics=("arbitrary",), vmem_limit_bytes=VMEM_LIMIT),
        name="mixer_l0",
    )(sinks, x, ln0_g, ln0_b, *mixer_weights)


def _combine_mixer(pos_tiles, y_sorted, x1_prev, gate_t, shared_weights, sinks, mixer_weights, *, layer, n_s):
    tq = MIXER_TILE
    T = x1_prev.shape[0] // TILE_SUBLANES
    n_tiles = T // tq
    tiles_per_step = tq // TOKEN_TILE
    pos_steps = pos_tiles.reshape(n_tiles, tiles_per_step * TOP_K, TOKEN_TILE)
    tile_of_step = lambda h: jnp.maximum(h - 1, 0)
    prev = lambda *shape: pl.BlockSpec((None,) + shape, lambda h: (layer - 1,) + (0,) * len(shape))
    out_specs, out_shape, scratch = _mixer_outputs(T, tq, tile_of_step)
    return pl.pallas_call(
        functools.partial(_combine_mixer_kernel, layer=layer, tq=tq, n_s=n_s, n_tiles=n_tiles),
        grid=(n_tiles + 1,),
        in_specs=[
            pl.BlockSpec(memory_space=pltpu.SMEM),
            pl.BlockSpec(memory_space=pl.ANY),
            pl.BlockSpec(memory_space=pl.ANY),
            pl.BlockSpec((tq * TILE_SUBLANES, LANES), lambda h: (tile_of_step(h), 0)),
            pl.BlockSpec((tq, TOP_K), lambda h: (tile_of_step(h), 0)),
            prev(D_MODEL, EXPERT_FF), prev(D_MODEL, EXPERT_FF), prev(EXPERT_FF, D_MODEL), prev(1, D_MODEL), prev(1, D_MODEL),
        ] + _mixer_specs(layer),
        out_specs=out_specs,
        out_shape=out_shape,
        scratch_shapes=scratch + [
            pltpu.SMEM((2, tiles_per_step * TOP_K, TOKEN_TILE), jnp.int32),
            pltpu.VMEM((2 * TOP_K * tq * PACK_SUBLANES, LANES), jnp.uint32),
            pltpu.SemaphoreType.DMA((2,)),
            pltpu.SemaphoreType.DMA((2,)),
        ],
        compiler_params=pltpu.CompilerParams(dimension_semantics=("arbitrary",), vmem_limit_bytes=FUSED_VMEM_LIMIT),
        name=f"combine_mixer_l{layer}",
    )(sinks, pos_steps, y_sorted, x1_prev, gate_t, *shared_weights, *mixer_weights)


def _first_index_of_max(vals, iota, n):
    m = jnp.max(vals, axis=0, keepdims=True)
    idx = jnp.min(jnp.where(vals == m, iota, n), axis=0, keepdims=True)
    return m, idx


def _route(xb, wr_ref, bias_ref, ek_ref, rk_ref, gk_ref, cnt_ref, carry):
    tr = xb.shape[0]
    logits = _dot_nt(wr_ref[...], xb)
    scores = _sigmoid(logits)
    biased = scores + bias_ref[...]
    neg_inf = -jnp.inf

    io8 = lax.broadcasted_iota(jnp.int32, (EXPERTS_PER_GROUP, tr), 0)
    group_rows = []
    for g in range(N_EXPERT_GROUPS):
        blk = biased[g * EXPERTS_PER_GROUP:(g + 1) * EXPERTS_PER_GROUP]
        m1, i1 = _first_index_of_max(blk, io8, EXPERTS_PER_GROUP)
        m2 = jnp.max(jnp.where(io8 == i1, neg_inf, blk), axis=0, keepdims=True)
        group_rows.append(m1 + m2)
    gscore = jnp.concatenate(group_rows, axis=0)
    iog = lax.broadcasted_iota(jnp.int32, (N_EXPERT_GROUPS, tr), 0)
    keep = jnp.zeros((N_EXPERT_GROUPS, tr), F32)
    for _ in range(TOPK_GROUPS):
        _, gi = _first_index_of_max(gscore, iog, N_EXPERT_GROUPS)
        hit = iog == gi
        keep = jnp.where(hit, 1.0, keep)
        gscore = jnp.where(hit, neg_inf, gscore)
    masked = jnp.concatenate(
        [jnp.where(keep[g:g + 1] > 0.0, biased[g * EXPERTS_PER_GROUP:(g + 1) * EXPERTS_PER_GROUP], neg_inf)
         for g in range(N_EXPERT_GROUPS)], axis=0)

    ioe = lax.broadcasted_iota(jnp.int32, (N_EXPERTS, tr), 0)
    sel = jnp.zeros((N_EXPERTS, tr), F32)
    e_rows, s_rows, hits = [], [], []
    for _ in range(TOP_K):
        _, ei = _first_index_of_max(masked, ioe, N_EXPERTS)
        hit = ioe == ei
        sel = jnp.where(hit, 1.0, sel)
        masked = jnp.where(hit, neg_inf, masked)
        e_rows.append(ei)
        hits.append(hit)
        s_rows.append(jnp.sum(jnp.where(hit, scores, 0.0), axis=0, keepdims=True))
    sel_scores = jnp.concatenate(s_rows, axis=0)
    gk_ref[...] = sel_scores / jnp.sum(sel_scores, axis=0, keepdims=True) * ROUTED_SCALE
    ek_ref[...] = jnp.concatenate(e_rows, axis=0)

    before = (lax.broadcasted_iota(jnp.int32, (tr, tr), 0) < lax.broadcasted_iota(jnp.int32, (tr, tr), 1))
    prefix = _dot(sel.astype(BF16), jnp.where(before, 1.0, 0.0).astype(BF16))
    rank_full = prefix + carry[...]
    rk_ref[...] = jnp.concatenate(
        [jnp.sum(jnp.where(hit, rank_full, 0.0), axis=0, keepdims=True) for hit in hits], axis=0).astype(jnp.int32)
    total = carry[...] + jnp.sum(sel, axis=1, keepdims=True)
    carry[...] = total
    cnt_ref[...] = jnp.broadcast_to(total, (N_EXPERTS, LANES))


def _router_kernel(x_ref, wr_ref, bias_ref, ek_ref, rk_ref, gk_ref, cnt_ref, carry, *, tr):
    @pl.when(pl.program_id(0) == 0)
    def _():
        carry[...] = jnp.zeros_like(carry)

    _route(_load_token_tiles(x_ref, tr).astype(BF16), wr_ref, bias_ref, ek_ref, rk_ref, gk_ref, cnt_ref, carry)


def _router(x1, w_router_t, bias_col, *, layer):
    T = x1.shape[0] // TILE_SUBLANES
    tr = ROUTER_TILE
    row_spec = pl.BlockSpec((TOP_K, tr), lambda i: (0, i))
    return pl.pallas_call(
        functools.partial(_router_kernel, tr=tr),
        grid=(T // tr,),
        in_specs=[
            pl.BlockSpec((tr * TILE_SUBLANES, LANES), lambda i: (i, 0)),
            pl.BlockSpec((None, N_EXPERTS, D_MODEL), lambda i: (layer, 0, 0)),
            pl.BlockSpec((None, N_EXPERTS, 1), lambda i: (layer, 0, 0)),
        ],
        out_specs=[row_spec, row_spec, row_spec, pl.BlockSpec((N_EXPERTS, LANES), lambda i: (0, 0))],
        out_shape=[
            jax.ShapeDtypeStruct((TOP_K, T), jnp.int32),
            jax.ShapeDtypeStruct((TOP_K, T), jnp.int32),
            jax.ShapeDtypeStruct((TOP_K, T), F32),
            jax.ShapeDtypeStruct((N_EXPERTS, LANES), F32),
        ],
        scratch_shapes=[pltpu.VMEM((N_EXPERTS, 1), F32)],
        compiler_params=pltpu.CompilerParams(dimension_semantics=("arbitrary",), vmem_limit_bytes=VMEM_LIMIT),
        name=f"router_l{layer}",
    )(x1, w_router_t, bias_col)


def _for_each_assignment(fn):
    for k in range(TOP_K):
        for j in range(TOKEN_TILE):
            fn(k, j)


def _dispatch_kernel(fill_ref, nu_ref, pos_hbm, x_ref, xs_hbm, idx_smem, zbuf, sem_idx, sem_out, sem_fill,
                     *, n_tiles, n_blocks):
    i = pl.program_id(0)
    slot = lax.rem(i, 2)

    def idx_copy(tile, sl):
        return pltpu.make_async_copy(pos_hbm.at[tile], idx_smem.at[sl], sem_idx.at[sl])

    def fill_copy(row0, rows):
        n = rows * PACK_SUBLANES
        return pltpu.make_async_copy(
            zbuf.at[pl.ds(0, n), :], xs_hbm.at[pl.ds(pl.multiple_of(row0 * PACK_SUBLANES, PACK_SUBLANES), n), :],
            sem_fill)

    def pad_fill(e, wait):
        row0 = fill_ref[e]
        n_pad = (0 - row0) & (EXPERT_BLOCK - 1)
        piece = EXPERT_BLOCK // 2
        while piece >= 1:
            has = (n_pad & piece) != 0

            @pl.when(has)
            def _(row0=row0, piece=piece):
                fill_copy(0 if wait else row0, piece).wait() if wait else fill_copy(row0, piece).start()
            row0 = row0 + jnp.where(has, piece, 0)
            piece //= 2

    @pl.when(i == 0)
    def _():
        idx_copy(0, 0).start()
        zbuf[...] = jnp.zeros_like(zbuf)
        lax.fori_loop(0, N_EXPERTS, lambda e, c: (pad_fill(e, False), c)[1], 0)
        lax.fori_loop(nu_ref[0], n_blocks, lambda b, c: (fill_copy(b * EXPERT_BLOCK, EXPERT_BLOCK).start(), c)[1], 0)
        lax.fori_loop(0, N_EXPERTS, lambda e, c: (pad_fill(e, True), c)[1], 0)
        lax.fori_loop(nu_ref[0], n_blocks, lambda b, c: (fill_copy(0, EXPERT_BLOCK).wait(), c)[1], 0)

    idx_copy(i, slot).wait()

    @pl.when(i + 1 < n_tiles)
    def _():
        idx_copy(i + 1, 1 - slot).start()

    def send(k, j):
        pltpu.make_async_copy(_packed_token(x_ref, j), _packed_token(xs_hbm, idx_smem[slot, k, j]), sem_out
                              ).start(priority=j % 2)
    _for_each_assignment(send)

    for _ in range(TOP_K):
        pltpu.make_async_copy(x_ref, x_ref, sem_out).wait()


def _dispatch(fill_start, n_used, pos_tiles, x1p, n_sorted_rows, *, layer):
    T = x1p.shape[0] // PACK_SUBLANES
    n_tiles = T // TOKEN_TILE
    return pl.pallas_call(
        functools.partial(_dispatch_kernel, n_tiles=n_tiles, n_blocks=n_sorted_rows // EXPERT_BLOCK),
        grid_spec=pltpu.PrefetchScalarGridSpec(
            num_scalar_prefetch=2,
            grid=(n_tiles,),
            in_specs=[
                pl.BlockSpec(memory_space=pl.ANY),
                pl.BlockSpec((TOKEN_TILE * PACK_SUBLANES, LANES), lambda i, fill, nu: (i, 0)),
            ],
            out_specs=pl.BlockSpec(memory_space=pl.ANY),
            scratch_shapes=[
                pltpu.SMEM((2, TOP_K, TOKEN_TILE), jnp.int32),
                pltpu.VMEM((EXPERT_BLOCK * PACK_SUBLANES, LANES), jnp.uint32),
                pltpu.SemaphoreType.DMA((2,)),
                pltpu.SemaphoreType.DMA(()),
                pltpu.SemaphoreType.DMA(()),
            ],
        ),
        out_shape=jax.ShapeDtypeStruct((n_sorted_rows * PACK_SUBLANES, LANES), jnp.uint32),
        compiler_params=pltpu.CompilerParams(dimension_semantics=("arbitrary",), vmem_limit_bytes=VMEM_LIMIT),
        name=f"dispatch_l{layer}",
    )(fill_start, n_used, pos_tiles, x1p)


def _expert_kernel(be_ref, nu_ref, xs_ref, *refs):
    i = pl.program_id(0)
    n_slots = EXPERT_STEP_BLOCKS
    y_ref = refs[3 * n_slots]
    for sub in range(n_slots):
        wg_ref, wu_ref, wd_ref = refs[3 * sub:3 * sub + 3]
        wgu_bf, wd_bf = refs[3 * n_slots + 1 + 2 * sub:3 * n_slots + 3 + 2 * sub]
        blk = i * n_slots + sub
        row_base = sub * EXPERT_BLOCK

        @pl.when(blk < nu_ref[0])
        def _(blk=blk, row_base=row_base, wg_ref=wg_ref, wu_ref=wu_ref, wd_ref=wd_ref, wgu_bf=wgu_bf, wd_bf=wd_bf):
            new_expert = (i == 0) | (be_ref[blk] != be_ref[jnp.maximum(blk - n_slots, 0)])

            @pl.when(new_expert)
            def _():
                wgu_bf[:, 0:EXPERT_FF] = wg_ref[...].astype(BF16)
                wgu_bf[:, EXPERT_FF:2 * EXPERT_FF] = wu_ref[...].astype(BF16)
                wd_bf[...] = wd_ref[...].astype(BF16)

            h = _dot(_load_packed_tokens(xs_ref, EXPERT_BLOCK, row_base).astype(BF16), wgu_bf[...])
            g = h[:, 0:EXPERT_FF]
            a = (g * _sigmoid(g) * h[:, EXPERT_FF:2 * EXPERT_FF]).astype(BF16)
            _store_packed_tokens(y_ref, _dot(a, wd_bf[...]), row_base)

        @pl.when(blk >= nu_ref[0])
        def _(row_base=row_base):
            y_ref[row_base * PACK_SUBLANES:(row_base + EXPERT_BLOCK) * PACK_SUBLANES, :] = jnp.zeros(
                (EXPERT_BLOCK * PACK_SUBLANES, LANES), jnp.uint32)


def _experts(blk_expert, n_used, xs, w_gate, w_up, w_down, *, layer):
    nb = blk_expert.shape[0]
    n_slots = EXPERT_STEP_BLOCKS
    assert nb % n_slots == 0
    step_rows = n_slots * EXPERT_BLOCK
    w_specs = []
    for sub in range(n_slots):
        index = lambda i, be, nu, sub=sub: (layer * N_EXPERTS + be[jnp.minimum(i * n_slots + sub, nu[0] - 1)], 0, 0)
        w_specs += [pl.BlockSpec((None, D_MODEL, EXPERT_FF), index), pl.BlockSpec((None, D_MODEL, EXPERT_FF), index),
                    pl.BlockSpec((None, EXPERT_FF, D_MODEL), index)]
    in_row_spec = pl.BlockSpec((step_rows * PACK_SUBLANES, LANES),
                               lambda i, be, nu: (jnp.minimum(i, (nu[0] - 1) // n_slots), 0))
    return pl.pallas_call(
        _expert_kernel,
        grid_spec=pltpu.PrefetchScalarGridSpec(
            num_scalar_prefetch=2,
            grid=(nb // n_slots,),
            in_specs=[in_row_spec] + w_specs,
            out_specs=pl.BlockSpec((step_rows * PACK_SUBLANES, LANES), lambda i, be, nu: (i, 0)),
            scratch_shapes=[
                pltpu.VMEM((D_MODEL, 2 * EXPERT_FF), BF16),
                pltpu.VMEM((EXPERT_FF, D_MODEL), BF16),
            ] * n_slots,
        ),
        out_shape=jax.ShapeDtypeStruct((nb * EXPERT_BLOCK * PACK_SUBLANES, LANES), jnp.uint32),
        compiler_params=pltpu.CompilerParams(dimension_semantics=("arbitrary",), vmem_limit_bytes=VMEM_LIMIT),
        name=f"experts_l{layer}",
    )(blk_expert, n_used, xs, *([w_gate, w_up, w_down] * n_slots))


def _combine_kernel(pos_hbm, y_hbm, x_ref, gate_ref, wsg_ref, wsu_ref, wsd_ref, ln2g_ref, ln2b_ref, o_ref,
                    idx_smem, ybuf, sem_idx, sem_y, *, n_tiles):
    i = pl.program_id(0)
    slot = lax.rem(i, 2)
    nslot = 1 - slot
    rows_per_slot = TOP_K * TOKEN_TILE

    def idx_copy(tile, sl):
        return pltpu.make_async_copy(pos_hbm.at[tile], idx_smem.at[sl], sem_idx.at[sl])

    def start_gather(sl):
        def fetch(k, j):
            pltpu.make_async_copy(_packed_token(y_hbm, idx_smem[sl, k, j]),
                                  _packed_token(ybuf, sl * rows_per_slot + k * TOKEN_TILE + j), sem_y.at[sl]
                                  ).start(priority=j % 2)
        _for_each_assignment(fetch)

    @pl.when(i == 0)
    def _():
        idx_copy(0, 0).start()
        idx_copy(0, 0).wait()
        start_gather(0)
        if n_tiles > 1:
            idx_copy(1, 1).start()

    @pl.when(i + 1 < n_tiles)
    def _():
        idx_copy(i + 1, nslot).wait()
        start_gather(nslot)

    @pl.when(i + 2 < n_tiles)
    def _():
        idx_copy(i + 2, slot).start()

    x = _load_token_tiles(x_ref, TOKEN_TILE)
    xb = x.astype(BF16)
    g = _dot(xb, wsg_ref[...])
    a = (g * _sigmoid(g) * _dot(xb, wsu_ref[...])).astype(BF16)
    shared = _dot(a, wsd_ref[...])

    base = slot * rows_per_slot
    slot_rows = ybuf.at[pl.ds(pl.multiple_of(base * PACK_SUBLANES, PACK_SUBLANES), rows_per_slot * PACK_SUBLANES), :]
    pltpu.make_async_copy(slot_rows, slot_rows, sem_y.at[slot]).wait()
    gate = gate_ref[...]
    routed = None
    for k in range(TOP_K):
        term = gate[:, k:k + 1] * _load_packed_tokens(ybuf, TOKEN_TILE, base + k * TOKEN_TILE)
        routed = term if routed is None else routed + term
    o_ref[...] = _layer_norm(ALPHA * x + (routed + shared), ln2g_ref[...], ln2b_ref[...])


def _combine(pos_tiles, y_sorted, x1, gate_t, w_sg, w_su, w_sd, ln2_g, ln2_b, *, layer):
    T = x1.shape[0] // TILE_SUBLANES
    tc = TOKEN_TILE
    n_tiles = T // tc
    per_layer = lambda *shape: pl.BlockSpec((None,) + shape, lambda i: (layer,) + (0,) * len(shape))
    return pl.pallas_call(
        functools.partial(_combine_kernel, n_tiles=n_tiles),
        grid=(n_tiles,),
        in_specs=[
            pl.BlockSpec(memory_space=pl.ANY),
            pl.BlockSpec(memory_space=pl.ANY),
            pl.BlockSpec((tc * TILE_SUBLANES, LANES), lambda i: (i, 0)),
            pl.BlockSpec((tc, TOP_K), lambda i: (i, 0)),
            per_layer(D_MODEL, EXPERT_FF),
            per_layer(D_MODEL, EXPERT_FF),
            per_layer(EXPERT_FF, D_MODEL),
            per_layer(1, D_MODEL),
            per_layer(1, D_MODEL),
        ],
        out_specs=pl.BlockSpec((tc, D_MODEL), lambda i: (i, 0)),
        out_shape=jax.ShapeDtypeStruct((T, D_MODEL), F32),
        scratch_shapes=[
            pltpu.SMEM((2, TOP_K, TOKEN_TILE), jnp.int32),
            pltpu.VMEM((2 * TOP_K * tc * PACK_SUBLANES, LANES), jnp.uint32),
            pltpu.SemaphoreType.DMA((2,)),
            pltpu.SemaphoreType.DMA((2,)),
        ],
        compiler_params=pltpu.CompilerParams(dimension_semantics=("arbitrary",), vmem_limit_bytes=VMEM_LIMIT),
        name=f"combine_l{layer}",
    )(pos_tiles, y_sorted, x1, gate_t, w_sg, w_su, w_sd, ln2_g, ln2_b)


def _dispatch_plan(ek, rk, counts_f):
    T = ek.shape[1]
    blk = EXPERT_BLOCK
    nb = (T * TOP_K) // blk + N_EXPERTS
    experts = jnp.arange(N_EXPERTS, dtype=jnp.int32)
    counts = counts_f[:, 0].astype(jnp.int32)
    pcounts = (counts + blk - 1) // blk * blk
    pends = jnp.sum(jnp.where(experts[None, :] <= experts[:, None], pcounts[None, :], 0), axis=1)
    pstarts = pends - pcounts
    pos = jnp.sum(jnp.where(ek[None] == experts[:, None, None], pstarts[:, None, None], 0), axis=0) + rk
    n_used = (pends[-1] // blk).reshape(1)
    block_row0 = jnp.arange(nb, dtype=jnp.int32) * blk
    blk_expert = jnp.minimum(jnp.sum((pends[None, :] <= block_row0[:, None]).astype(jnp.int32), axis=1), N_EXPERTS - 1)
    fill_start = pstarts + counts
    pos_tiles = pos.reshape(TOP_K, T // TOKEN_TILE, TOKEN_TILE).transpose(1, 0, 2)
    return blk_expert, n_used, fill_start, pos_tiles, nb * blk


def kernel(x, ln0_g, ln0_b, w_in, b_in, w_pool, pool_scale, attn_sinks, w_br_pool, w_br_attn, w_out, ln1_g, ln1_b,
           w_router, router_bias, w_exp_gate, w_exp_up, w_exp_down, w_sh_gate, w_sh_up, w_sh_down, ln2_g, ln2_b):
    B, S, D = x.shape
    depth = w_in.shape[0]
    row = lambda a: a.reshape(a.shape[0], 1, a.shape[1])
    w_in_b, w_pool_b = w_in.astype(BF16), w_pool.astype(BF16)
    w_brp_b, w_bra_b, w_out_b = w_br_pool.astype(BF16), w_br_attn.astype(BF16), w_out.astype(BF16)
    w_router_t = jnp.swapaxes(w_router, 1, 2).astype(BF16)
    bias_col = router_bias.reshape(depth, N_EXPERTS, 1)
    w_sg_b, w_su_b, w_sd_b = w_sh_gate.astype(BF16), w_sh_up.astype(BF16), w_sh_down.astype(BF16)
    w_eg = w_exp_gate.reshape(depth * N_EXPERTS, D, EXPERT_FF)
    w_eu = w_exp_up.reshape(depth * N_EXPERTS, D, EXPERT_FF)
    w_ed = w_exp_down.reshape(depth * N_EXPERTS, EXPERT_FF, D)
    ln0_g2, ln0_b2 = ln0_g.reshape(1, D), ln0_b.reshape(1, D)

    mixer_weights = (w_in_b, row(b_in), w_pool_b, row(pool_scale), w_brp_b, w_bra_b, w_out_b, row(ln1_g), row(ln1_b))
    shared_weights = (w_sg_b, w_su_b, w_sd_b, row(ln2_g), row(ln2_b))
    n_s = S // MIXER_TILE
    x1, x1p = _first_mixer(x, attn_sinks, ln0_g2, ln0_b2, mixer_weights)
    for l in range(depth):
        ek, rk, gk, counts_f = _router(x1, w_router_t, bias_col, layer=l)
        blk_expert, n_used, fill_start, pos_tiles, n_sorted = _dispatch_plan(ek, rk, counts_f)
        xs = _dispatch(fill_start, n_used, pos_tiles, x1p, n_sorted + EXPERT_BLOCK, layer=l)
        y_sorted = _experts(blk_expert, n_used, xs, w_eg, w_eu, w_ed, layer=l)
        if l + 1 < depth:
            x1, x1p = _combine_mixer(pos_tiles, y_sorted, x1, gk.T, shared_weights, attn_sinks, mixer_weights,
                                     layer=l + 1, n_s=n_s)
    return _combine(pos_tiles, y_sorted, x1, gk.T, *shared_weights, layer=depth - 1).reshape(B, S, D)
```

```python
import functools

import jax
import jax.numpy as jnp
from jax import lax
from jax.experimental import pallas as pl
from jax.experimental.pallas import tpu as pltpu

D_MODEL = 1024
DEPTH = 4
POOL_GROUPS = 4
POOL_GROUP_CH = 128
POOL_WIDTH = POOL_GROUPS * POOL_GROUP_CH
POOL_WINDOWS = (2, 4, 8, 16)
POOL_HALO = 16
N_Q_HEADS = 8
N_KV_HEADS = 2
HEAD_DIM = 64
Q_WIDTH = N_Q_HEADS * HEAD_DIM
KV_WIDTH = N_KV_HEADS * HEAD_DIM
WINDOW = 128
ATT_BLOCK = 128
D_IN = POOL_WIDTH + Q_WIDTH + 2 * KV_WIDTH + 2 * D_MODEL
QKV_START = POOL_WIDTH
GATE_START = POOL_WIDTH + Q_WIDTH + 2 * KV_WIDTH
N_EXPERTS = 64
EXPERT_FF = 256
TOP_K = 8
N_EXPERT_GROUPS = 8
EXPERTS_PER_GROUP = N_EXPERTS // N_EXPERT_GROUPS
TOPK_GROUPS = 4
ROUTED_SCALE = 2.5
ALPHA = (2.0 * DEPTH) ** 0.25
LN_EPS = 1e-5
SEGMENT_HEADS = (0, 2, 1, 3, 4, 6, 5, 7)
ALIBI_SLOPES = tuple(float(2.0 ** (-8.0 * h / N_Q_HEADS)) for h in range(1, N_Q_HEADS + 1))

LANES = 128
SUBLANES = 8
TILE_SUBLANES = D_MODEL // LANES
PACK_SUBLANES = TILE_SUBLANES // 2
MIXER_TILE = 256
ROUTER_TILE = 1024
EXPERT_BLOCK = 512
GATHER_CHUNKS = 22
EXPERT_STEP_BLOCKS = 2
TOKEN_TILE = 128
VMEM_LIMIT = 48 * 1024 * 1024
FUSED_VMEM_LIMIT = 56 * 1024 * 1024

BF16 = jnp.bfloat16
F32 = jnp.float32

assert TILE_SUBLANES == SUBLANES and TOP_K == SUBLANES and TOKEN_TILE == LANES


def _dot(a, b):
    return jnp.dot(a, b, preferred_element_type=F32)


def _dot_nt(a, b):
    return lax.dot_general(a, b, (((1,), (1,)), ((), ())), preferred_element_type=F32)


def _layer_norm(x, g, b):
    mu = jnp.mean(x, axis=-1, keepdims=True)
    xc = x - mu
    var = jnp.mean(xc * xc, axis=-1, keepdims=True)
    return xc * lax.rsqrt(var + LN_EPS) * g + b


def _sigmoid(x):
    return 0.5 * jnp.tanh(0.5 * x) + 0.5


def _load_token_tiles(ref, rows, row0=0):
    return jnp.concatenate(
        [ref[pl.ds(row0 * TILE_SUBLANES + j, rows, stride=TILE_SUBLANES), :] for j in range(TILE_SUBLANES)], axis=1)


def _store_token_tiles(ref, value, row0=0):
    for j in range(TILE_SUBLANES):
        ref[pl.ds(row0 * TILE_SUBLANES + j, value.shape[0], stride=TILE_SUBLANES), :] = (
            value[:, j * LANES:(j + 1) * LANES])


def _bf16_bits(x):
    return lax.bitcast_convert_type(x.astype(BF16).astype(F32), jnp.uint32)


def _store_packed_tokens(ref, value, row0=0):
    n = value.shape[0]
    for c in range(PACK_SUBLANES):
        lo = _bf16_bits(value[:, c * LANES:(c + 1) * LANES])
        hi = _bf16_bits(value[:, (c + PACK_SUBLANES) * LANES:(c + PACK_SUBLANES + 1) * LANES])
        ref[pl.ds(row0 * PACK_SUBLANES + c, n, stride=PACK_SUBLANES), :] = (
            lax.shift_right_logical(lo, jnp.uint32(16)) | hi)


def _load_packed_tokens(ref, rows, row0=0):
    los, his = [], []
    for c in range(PACK_SUBLANES):
        w = ref[pl.ds(row0 * PACK_SUBLANES + c, rows, stride=PACK_SUBLANES), :]
        los.append(lax.bitcast_convert_type(lax.shift_left(w, jnp.uint32(16)), F32))
        his.append(lax.bitcast_convert_type(w & jnp.uint32(0xFFFF0000), F32))
    return jnp.concatenate(los + his, axis=1)


def _packed_token(ref, row):
    return ref.at[pl.ds(pl.multiple_of(row * PACK_SUBLANES, PACK_SUBLANES), PACK_SUBLANES), :]


def _mixer_body(x, s, first, sinks_ref, w_in_ref, b_in_ref, w_pool_ref, pscale_ref, w_brp_ref, w_bra_ref, w_out_ref,
                ln1g_ref, ln1b_ref, o_ref, op_ref, ubuf, kvbuf, bias_tab, *, layer, tq, between_phases=lambda: None):
    @pl.when(s == 0)
    def _():
        ubuf[0:POOL_HALO, :] = jnp.zeros((POOL_HALO, POOL_WIDTH), F32)
        kvbuf[0:ATT_BLOCK, :] = jnp.zeros((ATT_BLOCK, 8 * LANES), BF16)

    xb = x.astype(BF16)

    u = _dot(xb, w_in_ref[:, 0:POOL_WIDTH]) + b_in_ref[:, 0:POOL_WIDTH]
    ubuf[POOL_HALO:POOL_HALO + tq, :] = u
    between_phases()
    pos = (s * tq + lax.broadcasted_iota(jnp.int32, (tq, 1), 0)).astype(F32)
    mixed_parts = []
    for g, w in enumerate(POOL_WINDOWS):
        sl = slice(g * POOL_GROUP_CH, (g + 1) * POOL_GROUP_CH)
        cur = ubuf[POOL_HALO:POOL_HALO + tq, sl]
        acc = cur
        for j in range(1, w):
            acc = acc + ubuf[POOL_HALO - j:POOL_HALO - j + tq, sl]
        inv_cnt = 1.0 / jnp.minimum(pos + 1.0, float(w))
        d = (acc * inv_cnt - cur).astype(BF16)
        mixed_parts.append(_dot(d, w_pool_ref[g]) * pscale_ref[:, sl])
        between_phases()
    mixed = jnp.concatenate(mixed_parts, axis=1).astype(BF16)
    y_pool = _dot(mixed, w_brp_ref[...])
    ubuf[0:POOL_HALO, :] = ubuf[tq:tq + POOL_HALO, :]
    between_phases()

    qkv = _dot(xb, w_in_ref[:, QKV_START:GATE_START]) + b_in_ref[:, QKV_START:GATE_START]
    q = (qkv[:, 0:Q_WIDTH] * (HEAD_DIM ** -0.5)).astype(BF16)
    k = qkv[:, Q_WIDTH:Q_WIDTH + KV_WIDTH]
    v = qkv[:, Q_WIDTH + KV_WIDTH:Q_WIDTH + 2 * KV_WIDTH]
    lo = lax.broadcasted_iota(jnp.int32, (tq, LANES), 1) < HEAD_DIM
    k_sw = pltpu.roll(k, HEAD_DIM, axis=1)
    v_sw = pltpu.roll(v, HEAD_DIM, axis=1)
    zero = jnp.zeros((tq, LANES), F32)
    slabs = (
        jnp.where(lo, k, zero), jnp.where(lo, zero, k_sw),
        jnp.where(lo, k_sw, zero), jnp.where(lo, zero, k),
        jnp.where(lo, v, zero), jnp.where(lo, zero, v_sw),
        jnp.where(lo, v_sw, zero), jnp.where(lo, zero, v),
    )
    for i, slab in enumerate(slabs):
        kvbuf[ATT_BLOCK:ATT_BLOCK + tq, i * LANES:(i + 1) * LANES] = slab.astype(BF16)
    between_phases()

    @pl.when(first)
    def _():
        qi = lax.broadcasted_iota(jnp.int32, (ATT_BLOCK, 2 * ATT_BLOCK), 0)
        kj = lax.broadcasted_iota(jnp.int32, (ATT_BLOCK, 2 * ATT_BLOCK), 1)
        dist = ATT_BLOCK + qi - kj
        band_ok = (dist >= 0) & (dist < WINDOW)
        distf = dist.astype(F32)
        for i, h in enumerate(SEGMENT_HEADS):
            bias_tab[i * ATT_BLOCK:(i + 1) * ATT_BLOCK, :] = jnp.where(band_ok, -ALIBI_SLOPES[h] * distf, -jnp.inf)

    sink_col = jnp.concatenate([jnp.full((ATT_BLOCK, 1), sinks_ref[layer, h], F32) for h in SEGMENT_HEADS], axis=0)
    key_col = lax.broadcasted_iota(jnp.int32, (1, 2 * ATT_BLOCK), 1)
    o_blocks = []
    for qb in range(tq // ATT_BLOCK):
        r0 = qb * ATT_BLOCK
        first_key_pos = s * tq + r0 - ATT_BLOCK
        score_parts = []
        for hk in range(N_KV_HEADS):
            q_pairs = jnp.concatenate(
                [q[r0:r0 + ATT_BLOCK, (2 * hk + pj) * LANES:(2 * hk + pj + 1) * LANES] for pj in range(2)], axis=0)
            for half in range(2):
                k_slab = kvbuf[r0:r0 + 2 * ATT_BLOCK, (2 * hk + half) * LANES:(2 * hk + half + 1) * LANES]
                score_parts.append(_dot_nt(q_pairs, k_slab))
        sc = jnp.concatenate(score_parts, axis=0) + bias_tab[...]
        between_phases()
        sc = jnp.where(key_col + first_key_pos < 0, -jnp.inf, sc)
        m = jnp.maximum(jnp.max(sc, axis=1, keepdims=True), sink_col)
        p = jnp.exp(sc - m).astype(BF16)
        den = _dot(p, jnp.ones((2 * ATT_BLOCK, LANES), BF16)) + jnp.exp(sink_col - m)
        inv_den = 1.0 / den
        between_phases()
        o_pairs = []
        for hk in range(N_KV_HEADS):
            pv = None
            for half in range(2):
                rows = (2 * hk + half) * 2 * ATT_BLOCK
                v_slab = kvbuf[r0:r0 + 2 * ATT_BLOCK, (4 + 2 * hk + half) * LANES:(4 + 2 * hk + half + 1) * LANES]
                contrib = _dot(p[rows:rows + 2 * ATT_BLOCK], v_slab) * inv_den[rows:rows + 2 * ATT_BLOCK]
                pv = contrib if pv is None else pv + contrib
            o_pairs += [pv[0:ATT_BLOCK], pv[ATT_BLOCK:2 * ATT_BLOCK]]
        o_blocks.append(jnp.concatenate(o_pairs, axis=1))
        between_phases()
    o = jnp.concatenate(o_blocks, axis=0).astype(BF16)
    y_attn = _dot(o, w_bra_ref[...])
    kvbuf[0:ATT_BLOCK, :] = kvbuf[tq:tq + ATT_BLOCK, :]
    between_phases()

    gates = _dot(xb, w_in_ref[:, GATE_START:D_IN]) + b_in_ref[:, GATE_START:D_IN]
    merged = _sigmoid(gates[:, 0:D_MODEL]) * y_pool + _sigmoid(gates[:, D_MODEL:2 * D_MODEL]) * y_attn
    between_phases()
    mix = _dot(merged.astype(BF16), w_out_ref[...])
    between_phases()
    x1 = _layer_norm(ALPHA * x + mix, ln1g_ref[...], ln1b_ref[...])
    _store_token_tiles(o_ref, x1)
    _store_packed_tokens(op_ref, x1)


def _first_mixer_kernel(sinks_ref, x_ref, ln0g_ref, ln0b_ref, *refs, layer, tq, n_s):
    t = pl.program_id(0)
    x = _layer_norm(x_ref[...], ln0g_ref[...], ln0b_ref[...])
    _mixer_body(x, lax.rem(t, n_s), t == 0, sinks_ref, *refs, layer=layer, tq=tq)


def _combine_mixer_kernel(sinks_ref, pos_hbm, y_hbm, xprev_ref, gate_ref, wsg_ref, wsu_ref, wsd_ref, ln2g_ref, ln2b_ref,
                          w_in_ref, b_in_ref, w_pool_ref, pscale_ref, w_brp_ref, w_bra_ref, w_out_ref, ln1g_ref,
                          ln1b_ref, o_ref, op_ref, ubuf, kvbuf, bias_tab, idx_smem, ybuf, sem_idx, sem_y,
                          *, layer, tq, n_s, n_tiles):
    h = pl.program_id(0)
    gslot = lax.rem(h, 2)
    rows_per_slot = TOP_K * tq
    tiles_per_step = tq // TOKEN_TILE
    assignments = [(ti, k, j) for ti in range(tiles_per_step) for k in range(TOP_K) for j in range(TOKEN_TILE)]
    bounds = [len(assignments) * c // GATHER_CHUNKS for c in range(GATHER_CHUNKS + 1)]
    issued = [0]

    def idx_copy(tile, sl):
        return pltpu.make_async_copy(pos_hbm.at[tile], idx_smem.at[sl], sem_idx.at[sl])

    def issue_next_chunk():
        c = issued[0]
        if c == GATHER_CHUNKS:
            return
        issued[0] = c + 1

        @pl.when(h < n_tiles)
        def _():
            if c == 0:
                idx_copy(h, gslot).wait()
            for ti, k, j in assignments[bounds[c]:bounds[c + 1]]:
                pltpu.make_async_copy(
                    _packed_token(y_hbm, idx_smem[gslot, ti * TOP_K + k, j]),
                    _packed_token(ybuf, gslot * rows_per_slot + k * tq + ti * TOKEN_TILE + j), sem_y.at[gslot]
                ).start(priority=j % 2)

    @pl.when(h == 0)
    def _():
        idx_copy(0, 0).start()
        ybuf[rows_per_slot * PACK_SUBLANES:2 * rows_per_slot * PACK_SUBLANES, :] = jnp.zeros(
            (rows_per_slot * PACK_SUBLANES, LANES), jnp.uint32)

    @pl.when(h + 1 < n_tiles)
    def _():
        idx_copy(h + 1, 1 - gslot).start()

    t = jnp.maximum(h - 1, 0)
    base = (1 - gslot) * rows_per_slot

    @pl.when(h > 0)
    def _():
        slot_rows = ybuf.at[pl.ds(pl.multiple_of(base * PACK_SUBLANES, PACK_SUBLANES), rows_per_slot * PACK_SUBLANES), :]
        pltpu.make_async_copy(slot_rows, slot_rows, sem_y.at[1 - gslot]).wait()

    issue_next_chunk()
    x1 = _load_token_tiles(xprev_ref, tq)
    x1b = x1.astype(BF16)
    g = _dot(x1b, wsg_ref[...])
    a = (g * _sigmoid(g) * _dot(x1b, wsu_ref[...])).astype(BF16)
    shared = _dot(a, wsd_ref[...])
    issue_next_chunk()
    gate = gate_ref[...]
    routed = None
    for k in range(TOP_K):
        term = gate[:, k:k + 1] * _load_packed_tokens(ybuf, tq, base + k * tq)
        routed = term if routed is None else routed + term
        if k % 2 == 1:
            issue_next_chunk()
    x = _layer_norm(ALPHA * x1 + (routed + shared), ln2g_ref[...], ln2b_ref[...])
    _mixer_body(x, lax.rem(t, n_s), t == 0, sinks_ref, w_in_ref, b_in_ref, w_pool_ref, pscale_ref, w_brp_ref,
                w_bra_ref, w_out_ref, ln1g_ref, ln1b_ref, o_ref, op_ref, ubuf, kvbuf, bias_tab, layer=layer, tq=tq,
                between_phases=issue_next_chunk)
    while issued[0] < GATHER_CHUNKS:
        issue_next_chunk()


def _mixer_specs(layer):
    per_layer = lambda *shape: pl.BlockSpec((None,) + shape, lambda h: (layer,) + (0,) * len(shape))
    return [
        per_layer(D_MODEL, D_IN), per_layer(1, D_IN), per_layer(POOL_GROUPS, POOL_GROUP_CH, POOL_GROUP_CH),
        per_layer(1, POOL_WIDTH), per_layer(POOL_WIDTH, D_MODEL), per_layer(Q_WIDTH, D_MODEL),
        per_layer(D_MODEL, D_MODEL), per_layer(1, D_MODEL), per_layer(1, D_MODEL),
    ]


def _mixer_outputs(T, tq, tile_of_step):
    out_specs = [
        pl.BlockSpec((tq * TILE_SUBLANES, LANES), lambda h: (tile_of_step(h), 0)),
        pl.BlockSpec((tq * PACK_SUBLANES, LANES), lambda h: (tile_of_step(h), 0)),
    ]
    out_shape = [
        jax.ShapeDtypeStruct((T * TILE_SUBLANES, LANES), F32),
        jax.ShapeDtypeStruct((T * PACK_SUBLANES, LANES), jnp.uint32),
    ]
    scratch = [
        pltpu.VMEM((POOL_HALO + tq, POOL_WIDTH), F32),
        pltpu.VMEM((ATT_BLOCK + tq, 8 * LANES), BF16),
        pltpu.VMEM((N_Q_HEADS * ATT_BLOCK, 2 * ATT_BLOCK), F32),
    ]
    return out_specs, out_shape, scratch


def _first_mixer(x, sinks, ln0_g, ln0_b, mixer_weights):
    B, S, D = x.shape
    tq = MIXER_TILE
    n_s = S // tq
    out_specs, out_shape, scratch = _mixer_outputs(B * S, tq, lambda h: h)
    return pl.pallas_call(
        functools.partial(_first_mixer_kernel, layer=0, tq=tq, n_s=n_s),
        grid=(B * n_s,),
        in_specs=[
            pl.BlockSpec(memory_space=pltpu.SMEM),
            pl.BlockSpec((None, tq, D), lambda h: (h // n_s, h % n_s, 0)),
            pl.BlockSpec((1, D), lambda h: (0, 0)),
            pl.BlockSpec((1, D), lambda h: (0, 0)),
        ] + _mixer_specs(0),
        out_specs=out_specs,
        out_shape=out_shape,
        scratch_shapes=scratch,
        compiler_params=pltpu.CompilerParams(dimension_semantics=("arbitrary",), vmem_limit_bytes=VMEM_LIMIT),
        name="mixer_l0",
    )(sinks, x, ln0_g, ln0_b, *mixer_weights)


def _combine_mixer(pos_tiles, y_sorted, x1_prev, gate_t, shared_weights, sinks, mixer_weights, *, layer, n_s):
    tq = MIXER_TILE
    T = x1_prev.shape[0] // TILE_SUBLANES
    n_tiles = T // tq
    tiles_per_step = tq // TOKEN_TILE
    pos_steps = pos_tiles.reshape(n_tiles, tiles_per_step * TOP_K, TOKEN_TILE)
    tile_of_step = lambda h: jnp.maximum(h - 1, 0)
    prev = lambda *shape: pl.BlockSpec((None,) + shape, lambda h: (layer - 1,) + (0,) * len(shape))
    out_specs, out_shape, scratch = _mixer_outputs(T, tq, tile_of_step)
    return pl.pallas_call(
        functools.partial(_combine_mixer_kernel, layer=layer, tq=tq, n_s=n_s, n_tiles=n_tiles),
        grid=(n_tiles + 1,),
        in_specs=[
            pl.BlockSpec(memory_space=pltpu.SMEM),
            pl.BlockSpec(memory_space=pl.ANY),
            pl.BlockSpec(memory_space=pl.ANY),
            pl.BlockSpec((tq * TILE_SUBLANES, LANES), lambda h: (tile_of_step(h), 0)),
            pl.BlockSpec((tq, TOP_K), lambda h: (tile_of_step(h), 0)),
            prev(D_MODEL, EXPERT_FF), prev(D_MODEL, EXPERT_FF), prev(EXPERT_FF, D_MODEL), prev(1, D_MODEL), prev(1, D_MODEL),
        ] + _mixer_specs(layer),
        out_specs=out_specs,
        out_shape=out_shape,
        scratch_shapes=scratch + [
            pltpu.SMEM((2, tiles_per_step * TOP_K, TOKEN_TILE), jnp.int32),
            pltpu.VMEM((2 * TOP_K * tq * PACK_SUBLANES, LANES), jnp.uint32),
            pltpu.SemaphoreType.DMA((2,)),
            pltpu.SemaphoreType.DMA((2,)),
        ],
        compiler_params=pltpu.CompilerParams(dimension_semantics=("arbitrary",), vmem_limit_bytes=FUSED_VMEM_LIMIT),
        name=f"combine_mixer_l{layer}",
    )(sinks, pos_steps, y_sorted, x1_prev, gate_t, *shared_weights, *mixer_weights)


def _first_index_of_max(vals, iota, n):
    m = jnp.max(vals, axis=0, keepdims=True)
    idx = jnp.min(jnp.where(vals == m, iota, n), axis=0, keepdims=True)
    return m, idx


def _route(xb, wr_ref, bias_ref, ek_ref, rk_ref, gk_ref, cnt_ref, carry):
    tr = xb.shape[0]
    logits = _dot_nt(wr_ref[...], xb)
    scores = _sigmoid(logits)
    biased = scores + bias_ref[...]
    neg_inf = -jnp.inf

    io8 = lax.broadcasted_iota(jnp.int32, (EXPERTS_PER_GROUP, tr), 0)
    group_rows = []
    for g in range(N_EXPERT_GROUPS):
        blk = biased[g * EXPERTS_PER_GROUP:(g + 1) * EXPERTS_PER_GROUP]
        m1, i1 = _first_index_of_max(blk, io8, EXPERTS_PER_GROUP)
        m2 = jnp.max(jnp.where(io8 == i1, neg_inf, blk), axis=0, keepdims=True)
        group_rows.append(m1 + m2)
    gscore = jnp.concatenate(group_rows, axis=0)
    iog = lax.broadcasted_iota(jnp.int32, (N_EXPERT_GROUPS, tr), 0)
    keep = jnp.zeros((N_EXPERT_GROUPS, tr), F32)
    for _ in range(TOPK_GROUPS):
        _, gi = _first_index_of_max(gscore, iog, N_EXPERT_GROUPS)
        hit = iog == gi
        keep = jnp.where(hit, 1.0, keep)
        gscore = jnp.where(hit, neg_inf, gscore)
    masked = jnp.concatenate(
        [jnp.where(keep[g:g + 1] > 0.0, biased[g * EXPERTS_PER_GROUP:(g + 1) * EXPERTS_PER_GROUP], neg_inf)
         for g in range(N_EXPERT_GROUPS)], axis=0)

    ioe = lax.broadcasted_iota(jnp.int32, (N_EXPERTS, tr), 0)
    sel = jnp.zeros((N_EXPERTS, tr), F32)
    e_rows, s_rows, hits = [], [], []
    for _ in range(TOP_K):
        _, ei = _first_index_of_max(masked, ioe, N_EXPERTS)
        hit = ioe == ei
        sel = jnp.where(hit, 1.0, sel)
        masked = jnp.where(hit, neg_inf, masked)
        e_rows.append(ei)
        hits.append(hit)
        s_rows.append(jnp.sum(jnp.where(hit, scores, 0.0), axis=0, keepdims=True))
    sel_scores = jnp.concatenate(s_rows, axis=0)
    gk_ref[...] = sel_scores / jnp.sum(sel_scores, axis=0, keepdims=True) * ROUTED_SCALE
    ek_ref[...] = jnp.concatenate(e_rows, axis=0)

    before = (lax.broadcasted_iota(jnp.int32, (tr, tr), 0) < lax.broadcasted_iota(jnp.int32, (tr, tr), 1))
    prefix = _dot(sel.astype(BF16), jnp.where(before, 1.0, 0.0).astype(BF16))
    rank_full = prefix + carry[...]
    rk_ref[...] = jnp.concatenate(
        [jnp.sum(jnp.where(hit, rank_full, 0.0), axis=0, keepdims=True) for hit in hits], axis=0).astype(jnp.int32)
    total = carry[...] + jnp.sum(sel, axis=1, keepdims=True)
    carry[...] = total
    cnt_ref[...] = jnp.broadcast_to(total, (N_EXPERTS, LANES))


def _router_kernel(x_ref, wr_ref, bias_ref, ek_ref, rk_ref, gk_ref, cnt_ref, carry, *, tr):
    @pl.when(pl.program_id(0) == 0)
    def _():
        carry[...] = jnp.zeros_like(carry)

    _route(_load_token_tiles(x_ref, tr).astype(BF16), wr_ref, bias_ref, ek_ref, rk_ref, gk_ref, cnt_ref, carry)


def _router(x1, w_router_t, bias_col, *, layer):
    T = x1.shape[0] // TILE_SUBLANES
    tr = ROUTER_TILE
    row_spec = pl.BlockSpec((TOP_K, tr), lambda i: (0, i))
    return pl.pallas_call(
        functools.partial(_router_kernel, tr=tr),
        grid=(T // tr,),
        in_specs=[
            pl.BlockSpec((tr * TILE_SUBLANES, LANES), lambda i: (i, 0)),
            pl.BlockSpec((None, N_EXPERTS, D_MODEL), lambda i: (layer, 0, 0)),
            pl.BlockSpec((None, N_EXPERTS, 1), lambda i: (layer, 0, 0)),
        ],
        out_specs=[row_spec, row_spec, row_spec, pl.BlockSpec((N_EXPERTS, LANES), lambda i: (0, 0))],
        out_shape=[
            jax.ShapeDtypeStruct((TOP_K, T), jnp.int32),
            jax.ShapeDtypeStruct((TOP_K, T), jnp.int32),
            jax.ShapeDtypeStruct((TOP_K, T), F32),
            jax.ShapeDtypeStruct((N_EXPERTS, LANES), F32),
        ],
        scratch_shapes=[pltpu.VMEM((N_EXPERTS, 1), F32)],
        compiler_params=pltpu.CompilerParams(dimension_semantics=("arbitrary",), vmem_limit_bytes=VMEM_LIMIT),
        name=f"router_l{layer}",
    )(x1, w_router_t, bias_col)


def _for_each_assignment(fn):
    for k in range(TOP_K):
        for j in range(TOKEN_TILE):
            fn(k, j)


def _dispatch_kernel(fill_ref, nu_ref, pos_hbm, x_ref, xs_hbm, idx_smem, zbuf, sem_idx, sem_out, sem_fill,
                     *, n_tiles, n_blocks):
    i = pl.program_id(0)
    slot = lax.rem(i, 2)

    def idx_copy(tile, sl):
        return pltpu.make_async_copy(pos_hbm.at[tile], idx_smem.at[sl], sem_idx.at[sl])

    def fill_copy(row0, rows):
        n = rows * PACK_SUBLANES
        return pltpu.make_async_copy(
            zbuf.at[pl.ds(0, n), :], xs_hbm.at[pl.ds(pl.multiple_of(row0 * PACK_SUBLANES, PACK_SUBLANES), n), :],
            sem_fill)

    def pad_fill(e, wait):
        row0 = fill_ref[e]
        n_pad = (0 - row0) & (EXPERT_BLOCK - 1)
        piece = EXPERT_BLOCK // 2
        while piece >= 1:
            has = (n_pad & piece) != 0

            @pl.when(has)
            def _(row0=row0, piece=piece):
                fill_copy(0 if wait else row0, piece).wait() if wait else fill_copy(row0, piece).start()
            row0 = row0 + jnp.where(has, piece, 0)
            piece //= 2

    @pl.when(i == 0)
    def _():
        idx_copy(0, 0).start()
        zbuf[...] = jnp.zeros_like(zbuf)
        lax.fori_loop(0, N_EXPERTS, lambda e, c: (pad_fill(e, False), c)[1], 0)
        lax.fori_loop(nu_ref[0], n_blocks, lambda b, c: (fill_copy(b * EXPERT_BLOCK, EXPERT_BLOCK).start(), c)[1], 0)
        lax.fori_loop(0, N_EXPERTS, lambda e, c: (pad_fill(e, True), c)[1], 0)
        lax.fori_loop(nu_ref[0], n_blocks, lambda b, c: (fill_copy(0, EXPERT_BLOCK).wait(), c)[1], 0)

    idx_copy(i, slot).wait()

    @pl.when(i + 1 < n_tiles)
    def _():
        idx_copy(i + 1, 1 - slot).start()

    def send(k, j):
        pltpu.make_async_copy(_packed_token(x_ref, j), _packed_token(xs_hbm, idx_smem[slot, k, j]), sem_out
                              ).start(priority=j % 2)
    _for_each_assignment(send)

    for _ in range(TOP_K):
        pltpu.make_async_copy(x_ref, x_ref, sem_out).wait()


def _dispatch(fill_start, n_used, pos_tiles, x1p, n_sorted_rows, *, layer):
    T = x1p.shape[0] // PACK_SUBLANES
    n_tiles = T // TOKEN_TILE
    return pl.pallas_call(
        functools.partial(_dispatch_kernel, n_tiles=n_tiles, n_blocks=n_sorted_rows // EXPERT_BLOCK),
        grid_spec=pltpu.PrefetchScalarGridSpec(
            num_scalar_prefetch=2,
            grid=(n_tiles,),
            in_specs=[
                pl.BlockSpec(memory_space=pl.ANY),
                pl.BlockSpec((TOKEN_TILE * PACK_SUBLANES, LANES), lambda i, fill, nu: (i, 0)),
            ],
            out_specs=pl.BlockSpec(memory_space=pl.ANY),
            scratch_shapes=[
                pltpu.SMEM((2, TOP_K, TOKEN_TILE), jnp.int32),
                pltpu.VMEM((EXPERT_BLOCK * PACK_SUBLANES, LANES), jnp.uint32),
                pltpu.SemaphoreType.DMA((2,)),
                pltpu.SemaphoreType.DMA(()),
                pltpu.SemaphoreType.DMA(()),
            ],
        ),
        out_shape=jax.ShapeDtypeStruct((n_sorted_rows * PACK_SUBLANES, LANES), jnp.uint32),
        compiler_params=pltpu.CompilerParams(dimension_semantics=("arbitrary",), vmem_limit_bytes=VMEM_LIMIT),
        name=f"dispatch_l{layer}",
    )(fill_start, n_used, pos_tiles, x1p)


def _expert_kernel(be_ref, nu_ref, xs_ref, *refs):
    i = pl.program_id(0)
    n_slots = EXPERT_STEP_BLOCKS
    y_ref = refs[3 * n_slots]
    for sub in range(n_slots):
        wg_ref, wu_ref, wd_ref = refs[3 * sub:3 * sub + 3]
        wgu_bf, wd_bf = refs[3 * n_slots + 1 + 2 * sub:3 * n_slots + 3 + 2 * sub]
        blk = i * n_slots + sub
        row_base = sub * EXPERT_BLOCK

        @pl.when(blk < nu_ref[0])
        def _(blk=blk, row_base=row_base, wg_ref=wg_ref, wu_ref=wu_ref, wd_ref=wd_ref, wgu_bf=wgu_bf, wd_bf=wd_bf):
            new_expert = (i == 0) | (be_ref[blk] != be_ref[jnp.maximum(blk - n_slots, 0)])

            @pl.when(new_expert)
            def _():
                wgu_bf[:, 0:EXPERT_FF] = wg_ref[...].astype(BF16)
                wgu_bf[:, EXPERT_FF:2 * EXPERT_FF] = wu_ref[...].astype(BF16)
                wd_bf[...] = wd_ref[...].astype(BF16)

            h = _dot(_load_packed_tokens(xs_ref, EXPERT_BLOCK, row_base).astype(BF16), wgu_bf[...])
            g = h[:, 0:EXPERT_FF]
            a = (g * _sigmoid(g) * h[:, EXPERT_FF:2 * EXPERT_FF]).astype(BF16)
            _store_packed_tokens(y_ref, _dot(a, wd_bf[...]), row_base)

        @pl.when(blk >= nu_ref[0])
        def _(row_base=row_base):
            y_ref[row_base * PACK_SUBLANES:(row_base + EXPERT_BLOCK) * PACK_SUBLANES, :] = jnp.zeros(
                (EXPERT_BLOCK * PACK_SUBLANES, LANES), jnp.uint32)


def _experts(blk_expert, n_used, xs, w_gate, w_up, w_down, *, layer):
    nb = blk_expert.shape[0]
    n_slots = EXPERT_STEP_BLOCKS
    assert nb % n_slots == 0
    step_rows = n_slots * EXPERT_BLOCK
    w_specs = []
    for sub in range(n_slots):
        index = lambda i, be, nu, sub=sub: (layer * N_EXPERTS + be[jnp.minimum(i * n_slots + sub, nu[0] - 1)], 0, 0)
        w_specs += [pl.BlockSpec((None, D_MODEL, EXPERT_FF), index), pl.BlockSpec((None, D_MODEL, EXPERT_FF), index),
                    pl.BlockSpec((None, EXPERT_FF, D_MODEL), index)]
    in_row_spec = pl.BlockSpec((step_rows * PACK_SUBLANES, LANES),
                               lambda i, be, nu: (jnp.minimum(i, (nu[0] - 1) // n_slots), 0))
    return pl.pallas_call(
        _expert_kernel,
        grid_spec=pltpu.PrefetchScalarGridSpec(
            num_scalar_prefetch=2,
            grid=(nb // n_slots,),
            in_specs=[in_row_spec] + w_specs,
            out_specs=pl.BlockSpec((step_rows * PACK_SUBLANES, LANES), lambda i, be, nu: (i, 0)),
            scratch_shapes=[
                pltpu.VMEM((D_MODEL, 2 * EXPERT_FF), BF16),
                pltpu.VMEM((EXPERT_FF, D_MODEL), BF16),
            ] * n_slots,
        ),
        out_shape=jax.ShapeDtypeStruct((nb * EXPERT_BLOCK * PACK_SUBLANES, LANES), jnp.uint32),
        compiler_params=pltpu.CompilerParams(dimension_semantics=("arbitrary",), vmem_limit_bytes=VMEM_LIMIT),
        name=f"experts_l{layer}",
    )(blk_expert, n_used, xs, *([w_gate, w_up, w_down] * n_slots))


def _combine_kernel(pos_hbm, y_hbm, x_ref, gate_ref, wsg_ref, wsu_ref, wsd_ref, ln2g_ref, ln2b_ref, o_ref,
                    idx_smem, ybuf, sem_idx, sem_y, *, n_tiles):
    i = pl.program_id(0)
    slot = lax.rem(i, 2)
    nslot = 1 - slot
    rows_per_slot = TOP_K * TOKEN_TILE

    def idx_copy(tile, sl):
        return pltpu.make_async_copy(pos_hbm.at[tile], idx_smem.at[sl], sem_idx.at[sl])

    def start_gather(sl):
        def fetch(k, j):
            pltpu.make_async_copy(_packed_token(y_hbm, idx_smem[sl, k, j]),
                                  _packed_token(ybuf, sl * rows_per_slot + k * TOKEN_TILE + j), sem_y.at[sl]
                                  ).start(priority=j % 2)
        _for_each_assignment(fetch)

    @pl.when(i == 0)
    def _():
        idx_copy(0, 0).start()
        idx_copy(0, 0).wait()
        start_gather(0)
        if n_tiles > 1:
            idx_copy(1, 1).start()

    @pl.when(i + 1 < n_tiles)
    def _():
        idx_copy(i + 1, nslot).wait()
        start_gather(nslot)

    @pl.when(i + 2 < n_tiles)
    def _():
        idx_copy(i + 2, slot).start()

    x = _load_token_tiles(x_ref, TOKEN_TILE)
    xb = x.astype(BF16)
    g = _dot(xb, wsg_ref[...])
    a = (g * _sigmoid(g) * _dot(xb, wsu_ref[...])).astype(BF16)
    shared = _dot(a, wsd_ref[...])

    base = slot * rows_per_slot
    slot_rows = ybuf.at[pl.ds(pl.multiple_of(base * PACK_SUBLANES, PACK_SUBLANES), rows_per_slot * PACK_SUBLANES), :]
    pltpu.make_async_copy(slot_rows, slot_rows, sem_y.at[slot]).wait()
    gate = gate_ref[...]
    routed = None
    for k in range(TOP_K):
        term = gate[:, k:k + 1] * _load_packed_tokens(ybuf, TOKEN_TILE, base + k * TOKEN_TILE)
        routed = term if routed is None else routed + term
    o_ref[...] = _layer_norm(ALPHA * x + (routed + shared), ln2g_ref[...], ln2b_ref[...])


def _combine(pos_tiles, y_sorted, x1, gate_t, w_sg, w_su, w_sd, ln2_g, ln2_b, *, layer):
    T = x1.shape[0] // TILE_SUBLANES
    tc = TOKEN_TILE
    n_tiles = T // tc
    per_layer = lambda *shape: pl.BlockSpec((None,) + shape, lambda i: (layer,) + (0,) * len(shape))
    return pl.pallas_call(
        functools.partial(_combine_kernel, n_tiles=n_tiles),
        grid=(n_tiles,),
        in_specs=[
            pl.BlockSpec(memory_space=pl.ANY),
            pl.BlockSpec(memory_space=pl.ANY),
            pl.BlockSpec((tc * TILE_SUBLANES, LANES), lambda i: (i, 0)),
            pl.BlockSpec((tc, TOP_K), lambda i: (i, 0)),
            per_layer(D_MODEL, EXPERT_FF),
            per_layer(D_MODEL, EXPERT_FF),
            per_layer(EXPERT_FF, D_MODEL),
            per_layer(1, D_MODEL),
            per_layer(1, D_MODEL),
        ],
        out_specs=pl.BlockSpec((tc, D_MODEL), lambda i: (i, 0)),
        out_shape=jax.ShapeDtypeStruct((T, D_MODEL), F32),
        scratch_shapes=[
            pltpu.SMEM((2, TOP_K, TOKEN_TILE), jnp.int32),
            pltpu.VMEM((2 * TOP_K * tc * PACK_SUBLANES, LANES), jnp.uint32),
            pltpu.SemaphoreType.DMA((2,)),
            pltpu.SemaphoreType.DMA((2,)),
        ],
        compiler_params=pltpu.CompilerParams(dimension_semantics=("arbitrary",), vmem_limit_bytes=VMEM_LIMIT),
        name=f"combine_l{layer}",
    )(pos_tiles, y_sorted, x1, gate_t, w_sg, w_su, w_sd, ln2_g, ln2_b)


def _dispatch_plan(ek, rk, counts_f):
    T = ek.shape[1]
    blk = EXPERT_BLOCK
    nb = (T * TOP_K) // blk + N_EXPERTS
    experts = jnp.arange(N_EXPERTS, dtype=jnp.int32)
    counts = counts_f[:, 0].astype(jnp.int32)
    pcounts = (counts + blk - 1) // blk * blk
    pends = jnp.sum(jnp.where(experts[None, :] <= experts[:, None], pcounts[None, :], 0), axis=1)
    pstarts = pends - pcounts
    pos = jnp.sum(jnp.where(ek[None] == experts[:, None, None], pstarts[:, None, None], 0), axis=0) + rk
    n_used = (pends[-1] // blk).reshape(1)
    block_row0 = jnp.arange(nb, dtype=jnp.int32) * blk
    blk_expert = jnp.minimum(jnp.sum((pends[None, :] <= block_row0[:, None]).astype(jnp.int32), axis=1), N_EXPERTS - 1)
    fill_start = pstarts + counts
    pos_tiles = pos.reshape(TOP_K, T // TOKEN_TILE, TOKEN_TILE).transpose(1, 0, 2)
    return blk_expert, n_used, fill_start, pos_tiles, nb * blk


def kernel(x, ln0_g, ln0_b, w_in, b_in, w_pool, pool_scale, attn_sinks, w_br_pool, w_br_attn, w_out, ln1_g, ln1_b,
           w_router, router_bias, w_exp_gate, w_exp_up, w_exp_down, w_sh_gate, w_sh_up, w_sh_down, ln2_g, ln2_b):
    B, S, D = x.shape
    depth = w_in.shape[0]
    row = lambda a: a.reshape(a.shape[0], 1, a.shape[1])
    w_in_b, w_pool_b = w_in.astype(BF16), w_pool.astype(BF16)
    w_brp_b, w_bra_b, w_out_b = w_br_pool.astype(BF16), w_br_attn.astype(BF16), w_out.astype(BF16)
    w_router_t = jnp.swapaxes(w_router, 1, 2).astype(BF16)
    bias_col = router_bias.reshape(depth, N_EXPERTS, 1)
    w_sg_b, w_su_b, w_sd_b = w_sh_gate.astype(BF16), w_sh_up.astype(BF16), w_sh_down.astype(BF16)
    w_eg = w_exp_gate.reshape(depth * N_EXPERTS, D, EXPERT_FF)
    w_eu = w_exp_up.reshape(depth * N_EXPERTS, D, EXPERT_FF)
    w_ed = w_exp_down.reshape(depth * N_EXPERTS, EXPERT_FF, D)
    ln0_g2, ln0_b2 = ln0_g.reshape(1, D), ln0_b.reshape(1, D)

    mixer_weights = (w_in_b, row(b_in), w_pool_b, row(pool_scale), w_brp_b, w_bra_b, w_out_b, row(ln1_g), row(ln1_b))
    shared_weights = (w_sg_b, w_su_b, w_sd_b, row(ln2_g), row(ln2_b))
    n_s = S // MIXER_TILE
    x1, x1p = _first_mixer(x, attn_sinks, ln0_g2, ln0_b2, mixer_weights)
    for l in range(depth):
        ek, rk, gk, counts_f = _router(x1, w_router_t, bias_col, layer=l)
        blk_expert, n_used, fill_start, pos_tiles, n_sorted = _dispatch_plan(ek, rk, counts_f)
        xs = _dispatch(fill_start, n_used, pos_tiles, x1p, n_sorted + EXPERT_BLOCK, layer=l)
        y_sorted = _experts(blk_expert, n_used, xs, w_eg, w_eu, w_ed, layer=l)
        if l + 1 < depth:
            x1, x1p = _combine_mixer(pos_tiles, y_sorted, x1, gk.T, shared_weights, attn_sinks, mixer_weights,
                                     layer=l + 1, n_s=n_s)
    return _combine(pos_tiles, y_sorted, x1, gk.T, *shared_weights, layer=depth - 1).reshape(B, S, D)
```

```python
import functools

import jax
import jax.numpy as jnp
from jax import lax
from jax.experimental import pallas as pl
from jax.experimental.pallas import tpu as pltpu

D_MODEL = 1024
DEPTH = 4
POOL_GROUPS = 4
POOL_GROUP_CH = 128
POOL_WIDTH = POOL_GROUPS * POOL_GROUP_CH
POOL_WINDOWS = (2, 4, 8, 16)
POOL_HALO = 16
N_Q_HEADS = 8
N_KV_HEADS = 2
HEAD_DIM = 64
Q_WIDTH = N_Q_HEADS * HEAD_DIM
KV_WIDTH = N_KV_HEADS * HEAD_DIM
WINDOW = 128
ATT_BLOCK = 128
D_IN = POOL_WIDTH + Q_WIDTH + 2 * KV_WIDTH + 2 * D_MODEL
QKV_START = POOL_WIDTH
GATE_START = POOL_WIDTH + Q_WIDTH + 2 * KV_WIDTH
N_EXPERTS = 64
EXPERT_FF = 256
TOP_K = 8
N_EXPERT_GROUPS = 8
EXPERTS_PER_GROUP = N_EXPERTS // N_EXPERT_GROUPS
TOPK_GROUPS = 4
ROUTED_SCALE = 2.5
ALPHA = (2.0 * DEPTH) ** 0.25
LN_EPS = 1e-5
SEGMENT_HEADS = (0, 2, 1, 3, 4, 6, 5, 7)
ALIBI_SLOPES = tuple(float(2.0 ** (-8.0 * h / N_Q_HEADS)) for h in range(1, N_Q_HEADS + 1))

LANES = 128
SUBLANES = 8
TILE_SUBLANES = D_MODEL // LANES
PACK_SUBLANES = TILE_SUBLANES // 2
MIXER_TILE = 256
ROUTER_TILE = 1024
EXPERT_BLOCK = 1024
GATHER_CHUNKS = 16
EXPERT_STEP_BLOCKS = 1
TOKEN_TILE = 128
VMEM_LIMIT = 48 * 1024 * 1024
FUSED_VMEM_LIMIT = 56 * 1024 * 1024

BF16 = jnp.bfloat16
F32 = jnp.float32

assert TILE_SUBLANES == SUBLANES and TOP_K == SUBLANES and TOKEN_TILE == LANES


def _dot(a, b):
    return jnp.dot(a, b, preferred_element_type=F32)


def _dot_nt(a, b):
    return lax.dot_general(a, b, (((1,), (1,)), ((), ())), preferred_element_type=F32)


def _layer_norm(x, g, b):
    mu = jnp.mean(x, axis=-1, keepdims=True)
    xc = x - mu
    var = jnp.mean(xc * xc, axis=-1, keepdims=True)
    return xc * lax.rsqrt(var + LN_EPS) * g + b


def _sigmoid(x):
    return 0.5 * jnp.tanh(0.5 * x) + 0.5


def _load_token_tiles(ref, rows, row0=0):
    return jnp.concatenate(
        [ref[pl.ds(row0 * TILE_SUBLANES + j, rows, stride=TILE_SUBLANES), :] for j in range(TILE_SUBLANES)], axis=1)


def _store_token_tiles(ref, value, row0=0):
    for j in range(TILE_SUBLANES):
        ref[pl.ds(row0 * TILE_SUBLANES + j, value.shape[0], stride=TILE_SUBLANES), :] = (
            value[:, j * LANES:(j + 1) * LANES])


def _bf16_bits(x):
    return lax.bitcast_convert_type(x.astype(BF16).astype(F32), jnp.uint32)


def _store_packed_tokens(ref, value, row0=0):
    n = value.shape[0]
    for c in range(PACK_SUBLANES):
        lo = _bf16_bits(value[:, c * LANES:(c + 1) * LANES])
        hi = _bf16_bits(value[:, (c + PACK_SUBLANES) * LANES:(c + PACK_SUBLANES + 1) * LANES])
        ref[pl.ds(row0 * PACK_SUBLANES + c, n, stride=PACK_SUBLANES), :] = (
            lax.shift_right_logical(lo, jnp.uint32(16)) | hi)


def _load_packed_tokens(ref, rows, row0=0):
    los, his = [], []
    for c in range(PACK_SUBLANES):
        w = ref[pl.ds(row0 * PACK_SUBLANES + c, rows, stride=PACK_SUBLANES), :]
        los.append(lax.bitcast_convert_type(lax.shift_left(w, jnp.uint32(16)), F32))
        his.append(lax.bitcast_convert_type(w & jnp.uint32(0xFFFF0000), F32))
    return jnp.concatenate(los + his, axis=1)


def _packed_token(ref, row):
    return ref.at[pl.ds(pl.multiple_of(row * PACK_SUBLANES, PACK_SUBLANES), PACK_SUBLANES), :]


def _mixer_body(x, s, first, sinks_ref, w_in_ref, b_in_ref, w_pool_ref, pscale_ref, w_brp_ref, w_bra_ref, w_out_ref,
                ln1g_ref, ln1b_ref, o_ref, op_ref, ubuf, kvbuf, bias_tab, *, layer, tq, between_phases=lambda: None):
    @pl.when(s == 0)
    def _():
        ubuf[0:POOL_HALO, :] = jnp.zeros((POOL_HALO, POOL_WIDTH), F32)
        kvbuf[0:ATT_BLOCK, :] = jnp.zeros((ATT_BLOCK, 8 * LANES), BF16)

    xb = x.astype(BF16)

    u = _dot(xb, w_in_ref[:, 0:POOL_WIDTH]) + b_in_ref[:, 0:POOL_WIDTH]
    ubuf[POOL_HALO:POOL_HALO + tq, :] = u
    between_phases()
    pos = (s * tq + lax.broadcasted_iota(jnp.int32, (tq, 1), 0)).astype(F32)
    mixed_parts = []
    for g, w in enumerate(POOL_WINDOWS):
        sl = slice(g * POOL_GROUP_CH, (g + 1) * POOL_GROUP_CH)
        cur = ubuf[POOL_HALO:POOL_HALO + tq, sl]
        acc = cur
        for j in range(1, w):
            acc = acc + ubuf[POOL_HALO - j:POOL_HALO - j + tq, sl]
        inv_cnt = 1.0 / jnp.minimum(pos + 1.0, float(w))
        d = (acc * inv_cnt - cur).astype(BF16)
        mixed_parts.append(_dot(d, w_pool_ref[g]) * pscale_ref[:, sl])
        between_phases()
    mixed = jnp.concatenate(mixed_parts, axis=1).astype(BF16)
    y_pool = _dot(mixed, w_brp_ref[...])
    ubuf[0:POOL_HALO, :] = ubuf[tq:tq + POOL_HALO, :]
    between_phases()

    qkv = _dot(xb, w_in_ref[:, QKV_START:GATE_START]) + b_in_ref[:, QKV_START:GATE_START]
    q = (qkv[:, 0:Q_WIDTH] * (HEAD_DIM ** -0.5)).astype(BF16)
    k = qkv[:, Q_WIDTH:Q_WIDTH + KV_WIDTH]
    v = qkv[:, Q_WIDTH + KV_WIDTH:Q_WIDTH + 2 * KV_WIDTH]
    lo = lax.broadcasted_iota(jnp.int32, (tq, LANES), 1) < HEAD_DIM
    k_sw = pltpu.roll(k, HEAD_DIM, axis=1)
    v_sw = pltpu.roll(v, HEAD_DIM, axis=1)
    zero = jnp.zeros((tq, LANES), F32)
    slabs = (
        jnp.where(lo, k, zero), jnp.where(lo, zero, k_sw),
        jnp.where(lo, k_sw, zero), jnp.where(lo, zero, k),
        jnp.where(lo, v, zero), jnp.where(lo, zero, v_sw),
        jnp.where(lo, v_sw, zero), jnp.where(lo, zero, v),
    )
    for i, slab in enumerate(slabs):
        kvbuf[ATT_BLOCK:ATT_BLOCK + tq, i * LANES:(i + 1) * LANES] = slab.astype(BF16)
    between_phases()

    @pl.when(first)
    def _():
        qi = lax.broadcasted_iota(jnp.int32, (ATT_BLOCK, 2 * ATT_BLOCK), 0)
        kj = lax.broadcasted_iota(jnp.int32, (ATT_BLOCK, 2 * ATT_BLOCK), 1)
        dist = ATT_BLOCK + qi - kj
        band_ok = (dist >= 0) & (dist < WINDOW)
        distf = dist.astype(F32)
        for i, h in enumerate(SEGMENT_HEADS):
            bias_tab[i * ATT_BLOCK:(i + 1) * ATT_BLOCK, :] = jnp.where(band_ok, -ALIBI_SLOPES[h] * distf, -jnp.inf)

    sink_col = jnp.concatenate([jnp.full((ATT_BLOCK, 1), sinks_ref[layer, h], F32) for h in SEGMENT_HEADS], axis=0)
    key_col = lax.broadcasted_iota(jnp.int32, (1, 2 * ATT_BLOCK), 1)
    o_blocks = []
    for qb in range(tq // ATT_BLOCK):
        r0 = qb * ATT_BLOCK
        first_key_pos = s * tq + r0 - ATT_BLOCK
        score_parts = []
        for hk in range(N_KV_HEADS):
            q_pairs = jnp.concatenate(
                [q[r0:r0 + ATT_BLOCK, (2 * hk + pj) * LANES:(2 * hk + pj + 1) * LANES] for pj in range(2)], axis=0)
            for half in range(2):
                k_slab = kvbuf[r0:r0 + 2 * ATT_BLOCK, (2 * hk + half) * LANES:(2 * hk + half + 1) * LANES]
                score_parts.append(_dot_nt(q_pairs, k_slab))
        sc = jnp.concatenate(score_parts, axis=0) + bias_tab[...]
        between_phases()
        sc = jnp.where(key_col + first_key_pos < 0, -jnp.inf, sc)
        m = jnp.maximum(jnp.max(sc, axis=1, keepdims=True), sink_col)
        p = jnp.exp(sc - m).astype(BF16)
        den = _dot(p, jnp.ones((2 * ATT_BLOCK, LANES), BF16)) + jnp.exp(sink_col - m)
        inv_den = 1.0 / den
        between_phases()
        o_pairs = []
        for hk in range(N_KV_HEADS):
            pv = None
            for half in range(2):
                rows = (2 * hk + half) * 2 * ATT_BLOCK
                v_slab = kvbuf[r0:r0 + 2 * ATT_BLOCK, (4 + 2 * hk + half) * LANES:(4 + 2 * hk + half + 1) * LANES]
                contrib = _dot(p[rows:rows + 2 * ATT_BLOCK], v_slab) * inv_den[rows:rows + 2 * ATT_BLOCK]
                pv = contrib if pv is None else pv + contrib
            o_pairs += [pv[0:ATT_BLOCK], pv[ATT_BLOCK:2 * ATT_BLOCK]]
        o_blocks.append(jnp.concatenate(o_pairs, axis=1))
        between_phases()
    o = jnp.concatenate(o_blocks, axis=0).astype(BF16)
    y_attn = _dot(o, w_bra_ref[...])
    kvbuf[0:ATT_BLOCK, :] = kvbuf[tq:tq + ATT_BLOCK, :]
    between_phases()

    gates = _dot(xb, w_in_ref[:, GATE_START:D_IN]) + b_in_ref[:, GATE_START:D_IN]
    merged = _sigmoid(gates[:, 0:D_MODEL]) * y_pool + _sigmoid(gates[:, D_MODEL:2 * D_MODEL]) * y_attn
    between_phases()
    mix = _dot(merged.astype(BF16), w_out_ref[...])
    between_phases()
    x1 = _layer_norm(ALPHA * x + mix, ln1g_ref[...], ln1b_ref[...])
    _store_token_tiles(o_ref, x1)
    _store_packed_tokens(op_ref, x1)


def _first_mixer_kernel(sinks_ref, x_ref, ln0g_ref, ln0b_ref, *refs, layer, tq, n_s):
    t = pl.program_id(0)
    x = _layer_norm(x_ref[...], ln0g_ref[...], ln0b_ref[...])
    _mixer_body(x, lax.rem(t, n_s), t == 0, sinks_ref, *refs, layer=layer, tq=tq)


def _combine_mixer_kernel(sinks_ref, pos_hbm, y_hbm, xprev_ref, gate_ref, wsg_ref, wsu_ref, wsd_ref, ln2g_ref, ln2b_ref,
                          w_in_ref, b_in_ref, w_pool_ref, pscale_ref, w_brp_ref, w_bra_ref, w_out_ref, ln1g_ref,
                          ln1b_ref, o_ref, op_ref, ubuf, kvbuf, bias_tab, idx_smem, ybuf, sem_idx, sem_y,
                          *, layer, tq, n_s, n_tiles):
    h = pl.program_id(0)
    gslot = lax.rem(h, 2)
    rows_per_slot = TOP_K * tq
    tiles_per_step = tq // TOKEN_TILE
    assignments = [(ti, k, j) for ti in range(tiles_per_step) for k in range(TOP_K) for j in range(TOKEN_TILE)]
    chunk = len(assignments) // GATHER_CHUNKS
    issued = [0]

    def idx_copy(tile, sl):
        return pltpu.make_async_copy(pos_hbm.at[tile], idx_smem.at[sl], sem_idx.at[sl])

    def issue_next_chunk():
        c = issued[0]
        if c == GATHER_CHUNKS:
            return
        issued[0] = c + 1

        @pl.when(h < n_tiles)
        def _():
            if c == 0:
                idx_copy(h, gslot).wait()
            for ti, k, j in assignments[c * chunk:(c + 1) * chunk]:
                pltpu.make_async_copy(
                    _packed_token(y_hbm, idx_smem[gslot, ti * TOP_K + k, j]),
                    _packed_token(ybuf, gslot * rows_per_slot + k * tq + ti * TOKEN_TILE + j), sem_y.at[gslot]
                ).start(priority=j % 2)

    @pl.when(h == 0)
    def _():
        idx_copy(0, 0).start()
        ybuf[rows_per_slot * PACK_SUBLANES:2 * rows_per_slot * PACK_SUBLANES, :] = jnp.zeros(
            (rows_per_slot * PACK_SUBLANES, LANES), jnp.uint32)

    @pl.when(h + 1 < n_tiles)
    def _():
        idx_copy(h + 1, 1 - gslot).start()

    t = jnp.maximum(h - 1, 0)
    base = (1 - gslot) * rows_per_slot

    @pl.when(h > 0)
    def _():
        slot_rows = ybuf.at[pl.ds(pl.multiple_of(base * PACK_SUBLANES, PACK_SUBLANES), rows_per_slot * PACK_SUBLANES), :]
        pltpu.make_async_copy(slot_rows, slot_rows, sem_y.at[1 - gslot]).wait()

    issue_next_chunk()
    x1 = _load_token_tiles(xprev_ref, tq)
    x1b = x1.astype(BF16)
    g = _dot(x1b, wsg_ref[...])
    a = (g * _sigmoid(g) * _dot(x1b, wsu_ref[...])).astype(BF16)
    shared = _dot(a, wsd_ref[...])
    issue_next_chunk()
    gate = gate_ref[...]
    routed = None
    for k in range(TOP_K):
        term = gate[:, k:k + 1] * _load_packed_tokens(ybuf, tq, base + k * tq)
        routed = term if routed is None else routed + term
        if k % 2 == 1:
            issue_next_chunk()
    x = _layer_norm(ALPHA * x1 + (routed + shared), ln2g_ref[...], ln2b_ref[...])
    _mixer_body(x, lax.rem(t, n_s), t == 0, sinks_ref, w_in_ref, b_in_ref, w_pool_ref, pscale_ref, w_brp_ref,
                w_bra_ref, w_out_ref, ln1g_ref, ln1b_ref, o_ref, op_ref, ubuf, kvbuf, bias_tab, layer=layer, tq=tq,
                between_phases=issue_next_chunk)
    while issued[0] < GATHER_CHUNKS:
        issue_next_chunk()


def _mixer_specs(layer):
    per_layer = lambda *shape: pl.BlockSpec((None,) + shape, lambda h: (layer,) + (0,) * len(shape))
    return [
        per_layer(D_MODEL, D_IN), per_layer(1, D_IN), per_layer(POOL_GROUPS, POOL_GROUP_CH, POOL_GROUP_CH),
        per_layer(1, POOL_WIDTH), per_layer(POOL_WIDTH, D_MODEL), per_layer(Q_WIDTH, D_MODEL),
        per_layer(D_MODEL, D_MODEL), per_layer(1, D_MODEL), per_layer(1, D_MODEL),
    ]


def _mixer_outputs(T, tq, tile_of_step):
    out_specs = [
        pl.BlockSpec((tq * TILE_SUBLANES, LANES), lambda h: (tile_of_step(h), 0)),
        pl.BlockSpec((tq * PACK_SUBLANES, LANES), lambda h: (tile_of_step(h), 0)),
    ]
    out_shape = [
        jax.ShapeDtypeStruct((T * TILE_SUBLANES, LANES), F32),
        jax.ShapeDtypeStruct((T * PACK_SUBLANES, LANES), jnp.uint32),
    ]
    scratch = [
        pltpu.VMEM((POOL_HALO + tq, POOL_WIDTH), F32),
        pltpu.VMEM((ATT_BLOCK + tq, 8 * LANES), BF16),
        pltpu.VMEM((N_Q_HEADS * ATT_BLOCK, 2 * ATT_BLOCK), F32),
    ]
    return out_specs, out_shape, scratch


def _first_mixer(x, sinks, ln0_g, ln0_b, mixer_weights):
    B, S, D = x.shape
    tq = MIXER_TILE
    n_s = S // tq
    out_specs, out_shape, scratch = _mixer_outputs(B * S, tq, lambda h: h)
    return pl.pallas_call(
        functools.partial(_first_mixer_kernel, layer=0, tq=tq, n_s=n_s),
        grid=(B * n_s,),
        in_specs=[
            pl.BlockSpec(memory_space=pltpu.SMEM),
            pl.BlockSpec((None, tq, D), lambda h: (h // n_s, h % n_s, 0)),
            pl.BlockSpec((1, D), lambda h: (0, 0)),
            pl.BlockSpec((1, D), lambda h: (0, 0)),
        ] + _mixer_specs(0),
        out_specs=out_specs,
        out_shape=out_shape,
        scratch_shapes=scratch,
        compiler_params=pltpu.CompilerParams(dimension_semantics=("arbitrary",), vmem_limit_bytes=VMEM_LIMIT),
        name="mixer_l0",
    )(sinks, x, ln0_g, ln0_b, *mixer_weights)


def _combine_mixer(pos_tiles, y_sorted, x1_prev, gate_t, shared_weights, sinks, mixer_weights, *, layer, n_s):
    tq = MIXER_TILE
    T = x1_prev.shape[0] // TILE_SUBLANES
    n_tiles = T // tq
    tiles_per_step = tq // TOKEN_TILE
    pos_steps = pos_tiles.reshape(n_tiles, tiles_per_step * TOP_K, TOKEN_TILE)
    tile_of_step = lambda h: jnp.maximum(h - 1, 0)
    prev = lambda *shape: pl.BlockSpec((None,) + shape, lambda h: (layer - 1,) + (0,) * len(shape))
    out_specs, out_shape, scratch = _mixer_outputs(T, tq, tile_of_step)
    return pl.pallas_call(
        functools.partial(_combine_mixer_kernel, layer=layer, tq=tq, n_s=n_s, n_tiles=n_tiles),
        grid=(n_tiles + 1,),
        in_specs=[
            pl.BlockSpec(memory_space=pltpu.SMEM),
            pl.BlockSpec(memory_space=pl.ANY),
            pl.BlockSpec(memory_space=pl.ANY),
            pl.BlockSpec((tq * TILE_SUBLANES, LANES), lambda h: (tile_of_step(h), 0)),
            pl.BlockSpec((tq, TOP_K), lambda h: (tile_of_step(h), 0)),
            prev(D_MODEL, EXPERT_FF), prev(D_MODEL, EXPERT_FF), prev(EXPERT_FF, D_MODEL), prev(1, D_MODEL), prev(1, D_MODEL),
        ] + _mixer_specs(layer),
        out_specs=out_specs,
        out_shape=out_shape,
        scratch_shapes=scratch + [
            pltpu.SMEM((2, tiles_per_step * TOP_K, TOKEN_TILE), jnp.int32),
            pltpu.VMEM((2 * TOP_K * tq * PACK_SUBLANES, LANES), jnp.uint32),
            pltpu.SemaphoreType.DMA((2,)),
            pltpu.SemaphoreType.DMA((2,)),
        ],
        compiler_params=pltpu.CompilerParams(dimension_semantics=("arbitrary",), vmem_limit_bytes=FUSED_VMEM_LIMIT),
        name=f"combine_mixer_l{layer}",
    )(sinks, pos_steps, y_sorted, x1_prev, gate_t, *shared_weights, *mixer_weights)


def _first_index_of_max(vals, iota, n):
    m = jnp.max(vals, axis=0, keepdims=True)
    idx = jnp.min(jnp.where(vals == m, iota, n), axis=0, keepdims=True)
    return m, idx


def _route(xb, wr_ref, bias_ref, ek_ref, rk_ref, gk_ref, cnt_ref, carry):
    tr = xb.shape[0]
    logits = _dot_nt(wr_ref[...], xb)
    scores = _sigmoid(logits)
    biased = scores + bias_ref[...]
    neg_inf = -jnp.inf

    io8 = lax.broadcasted_iota(jnp.int32, (EXPERTS_PER_GROUP, tr), 0)
    group_rows = []
    for g in range(N_EXPERT_GROUPS):
        blk = biased[g * EXPERTS_PER_GROUP:(g + 1) * EXPERTS_PER_GROUP]
        m1, i1 = _first_index_of_max(blk, io8, EXPERTS_PER_GROUP)
        m2 = jnp.max(jnp.where(io8 == i1, neg_inf, blk), axis=0, keepdims=True)
        group_rows.append(m1 + m2)
    gscore = jnp.concatenate(group_rows, axis=0)
    iog = lax.broadcasted_iota(jnp.int32, (N_EXPERT_GROUPS, tr), 0)
    keep = jnp.zeros((N_EXPERT_GROUPS, tr), F32)
    for _ in range(TOPK_GROUPS):
        _, gi = _first_index_of_max(gscore, iog, N_EXPERT_GROUPS)
        hit = iog == gi
        keep = jnp.where(hit, 1.0, keep)
        gscore = jnp.where(hit, neg_inf, gscore)
    masked = jnp.concatenate(
        [jnp.where(keep[g:g + 1] > 0.0, biased[g * EXPERTS_PER_GROUP:(g + 1) * EXPERTS_PER_GROUP], neg_inf)
         for g in range(N_EXPERT_GROUPS)], axis=0)

    ioe = lax.broadcasted_iota(jnp.int32, (N_EXPERTS, tr), 0)
    sel = jnp.zeros((N_EXPERTS, tr), F32)
    e_rows, s_rows, hits = [], [], []
    for _ in range(TOP_K):
        _, ei = _first_index_of_max(masked, ioe, N_EXPERTS)
        hit = ioe == ei
        sel = jnp.where(hit, 1.0, sel)
        masked = jnp.where(hit, neg_inf, masked)
        e_rows.append(ei)
        hits.append(hit)
        s_rows.append(jnp.sum(jnp.where(hit, scores, 0.0), axis=0, keepdims=True))
    sel_scores = jnp.concatenate(s_rows, axis=0)
    gk_ref[...] = sel_scores / jnp.sum(sel_scores, axis=0, keepdims=True) * ROUTED_SCALE
    ek_ref[...] = jnp.concatenate(e_rows, axis=0)

    before = (lax.broadcasted_iota(jnp.int32, (tr, tr), 0) < lax.broadcasted_iota(jnp.int32, (tr, tr), 1))
    prefix = _dot(sel.astype(BF16), jnp.where(before, 1.0, 0.0).astype(BF16))
    rank_full = prefix + carry[...]
    rk_ref[...] = jnp.concatenate(
        [jnp.sum(jnp.where(hit, rank_full, 0.0), axis=0, keepdims=True) for hit in hits], axis=0).astype(jnp.int32)
    total = carry[...] + jnp.sum(sel, axis=1, keepdims=True)
    carry[...] = total
    cnt_ref[...] = jnp.broadcast_to(total, (N_EXPERTS, LANES))


def _router_kernel(x_ref, wr_ref, bias_ref, ek_ref, rk_ref, gk_ref, cnt_ref, carry, *, tr):
    @pl.when(pl.program_id(0) == 0)
    def _():
        carry[...] = jnp.zeros_like(carry)

    _route(_load_token_tiles(x_ref, tr).astype(BF16), wr_ref, bias_ref, ek_ref, rk_ref, gk_ref, cnt_ref, carry)


def _router(x1, w_router_t, bias_col, *, layer):
    T = x1.shape[0] // TILE_SUBLANES
    tr = ROUTER_TILE
    row_spec = pl.BlockSpec((TOP_K, tr), lambda i: (0, i))
    return pl.pallas_call(
        functools.partial(_router_kernel, tr=tr),
        grid=(T // tr,),
        in_specs=[
            pl.BlockSpec((tr * TILE_SUBLANES, LANES), lambda i: (i, 0)),
            pl.BlockSpec((None, N_EXPERTS, D_MODEL), lambda i: (layer, 0, 0)),
            pl.BlockSpec((None, N_EXPERTS, 1), lambda i: (layer, 0, 0)),
        ],
        out_specs=[row_spec, row_spec, row_spec, pl.BlockSpec((N_EXPERTS, LANES), lambda i: (0, 0))],
        out_shape=[
            jax.ShapeDtypeStruct((TOP_K, T), jnp.int32),
            jax.ShapeDtypeStruct((TOP_K, T), jnp.int32),
            jax.ShapeDtypeStruct((TOP_K, T), F32),
            jax.ShapeDtypeStruct((N_EXPERTS, LANES), F32),
        ],
        scratch_shapes=[pltpu.VMEM((N_EXPERTS, 1), F32)],
        compiler_params=pltpu.CompilerParams(dimension_semantics=("arbitrary",), vmem_limit_bytes=VMEM_LIMIT),
        name=f"router_l{layer}",
    )(x1, w_router_t, bias_col)


def _for_each_assignment(fn):
    for k in range(TOP_K):
        for j in range(TOKEN_TILE):
            fn(k, j)


def _dispatch_kernel(fill_ref, nu_ref, pos_hbm, x_ref, xs_hbm, idx_smem, zbuf, sem_idx, sem_out, sem_fill,
                     *, n_tiles, n_blocks):
    i = pl.program_id(0)
    slot = lax.rem(i, 2)

    def idx_copy(tile, sl):
        return pltpu.make_async_copy(pos_hbm.at[tile], idx_smem.at[sl], sem_idx.at[sl])

    def fill_copy(row0, rows):
        n = rows * PACK_SUBLANES
        return pltpu.make_async_copy(
            zbuf.at[pl.ds(0, n), :], xs_hbm.at[pl.ds(pl.multiple_of(row0 * PACK_SUBLANES, PACK_SUBLANES), n), :],
            sem_fill)

    def pad_fill(e, wait):
        row0 = fill_ref[e]
        n_pad = (0 - row0) & (EXPERT_BLOCK - 1)
        piece = EXPERT_BLOCK // 2
        while piece >= 1:
            has = (n_pad & piece) != 0

            @pl.when(has)
            def _(row0=row0, piece=piece):
                fill_copy(0 if wait else row0, piece).wait() if wait else fill_copy(row0, piece).start()
            row0 = row0 + jnp.where(has, piece, 0)
            piece //= 2

    @pl.when(i == 0)
    def _():
        idx_copy(0, 0).start()
        zbuf[...] = jnp.zeros_like(zbuf)
        lax.fori_loop(0, N_EXPERTS, lambda e, c: (pad_fill(e, False), c)[1], 0)
        lax.fori_loop(nu_ref[0], n_blocks, lambda b, c: (fill_copy(b * EXPERT_BLOCK, EXPERT_BLOCK).start(), c)[1], 0)
        lax.fori_loop(0, N_EXPERTS, lambda e, c: (pad_fill(e, True), c)[1], 0)
        lax.fori_loop(nu_ref[0], n_blocks, lambda b, c: (fill_copy(0, EXPERT_BLOCK).wait(), c)[1], 0)

    idx_copy(i, slot).wait()

    @pl.when(i + 1 < n_tiles)
    def _():
        idx_copy(i + 1, 1 - slot).start()

    def send(k, j):
        pltpu.make_async_copy(_packed_token(x_ref, j), _packed_token(xs_hbm, idx_smem[slot, k, j]), sem_out
                              ).start(priority=j % 2)
    _for_each_assignment(send)

    for _ in range(TOP_K):
        pltpu.make_async_copy(x_ref, x_ref, sem_out).wait()


def _dispatch(fill_start, n_used, pos_tiles, x1p, n_sorted_rows, *, layer):
    T = x1p.shape[0] // PACK_SUBLANES
    n_tiles = T // TOKEN_TILE
    return pl.pallas_call(
        functools.partial(_dispatch_kernel, n_tiles=n_tiles, n_blocks=n_sorted_rows // EXPERT_BLOCK),
        grid_spec=pltpu.PrefetchScalarGridSpec(
            num_scalar_prefetch=2,
            grid=(n_tiles,),
            in_specs=[
                pl.BlockSpec(memory_space=pl.ANY),
                pl.BlockSpec((TOKEN_TILE * PACK_SUBLANES, LANES), lambda i, fill, nu: (i, 0)),
            ],
            out_specs=pl.BlockSpec(memory_space=pl.ANY),
            scratch_shapes=[
                pltpu.SMEM((2, TOP_K, TOKEN_TILE), jnp.int32),
                pltpu.VMEM((EXPERT_BLOCK * PACK_SUBLANES, LANES), jnp.uint32),
                pltpu.SemaphoreType.DMA((2,)),
                pltpu.SemaphoreType.DMA(()),
                pltpu.SemaphoreType.DMA(()),
            ],
        ),
        out_shape=jax.ShapeDtypeStruct((n_sorted_rows * PACK_SUBLANES, LANES), jnp.uint32),
        compiler_params=pltpu.CompilerParams(dimension_semantics=("arbitrary",), vmem_limit_bytes=VMEM_LIMIT),
        name=f"dispatch_l{layer}",
    )(fill_start, n_used, pos_tiles, x1p)


def _expert_kernel(be_ref, nu_ref, xs_ref, *refs):
    i = pl.program_id(0)
    n_slots = EXPERT_STEP_BLOCKS
    y_ref = refs[3 * n_slots]
    for sub in range(n_slots):
        wg_ref, wu_ref, wd_ref = refs[3 * sub:3 * sub + 3]
        wgu_bf, wd_bf = refs[3 * n_slots + 1 + 2 * sub:3 * n_slots + 3 + 2 * sub]
        blk = i * n_slots + sub
        row_base = sub * EXPERT_BLOCK

        @pl.when(blk < nu_ref[0])
        def _(blk=blk, row_base=row_base, wg_ref=wg_ref, wu_ref=wu_ref, wd_ref=wd_ref, wgu_bf=wgu_bf, wd_bf=wd_bf):
            new_expert = (i == 0) | (be_ref[blk] != be_ref[jnp.maximum(blk - n_slots, 0)])

            @pl.when(new_expert)
            def _():
                wgu_bf[:, 0:EXPERT_FF] = wg_ref[...].astype(BF16)
                wgu_bf[:, EXPERT_FF:2 * EXPERT_FF] = wu_ref[...].astype(BF16)
                wd_bf[...] = wd_ref[...].astype(BF16)

            h = _dot(_load_packed_tokens(xs_ref, EXPERT_BLOCK, row_base).astype(BF16), wgu_bf[...])
            g = h[:, 0:EXPERT_FF]
            a = (g * _sigmoid(g) * h[:, EXPERT_FF:2 * EXPERT_FF]).astype(BF16)
            _store_packed_tokens(y_ref, _dot(a, wd_bf[...]), row_base)

        @pl.when(blk >= nu_ref[0])
        def _(row_base=row_base):
            y_ref[row_base * PACK_SUBLANES:(row_base + EXPERT_BLOCK) * PACK_SUBLANES, :] = jnp.zeros(
                (EXPERT_BLOCK * PACK_SUBLANES, LANES), jnp.uint32)


def _experts(blk_expert, n_used, xs, w_gate, w_up, w_down, *, layer):
    nb = blk_expert.shape[0]
    n_slots = EXPERT_STEP_BLOCKS
    assert nb % n_slots == 0
    step_rows = n_slots * EXPERT_BLOCK
    w_specs = []
    for sub in range(n_slots):
        index = lambda i, be, nu, sub=sub: (layer * N_EXPERTS + be[jnp.minimum(i * n_slots + sub, nu[0] - 1)], 0, 0)
        w_specs += [pl.BlockSpec((None, D_MODEL, EXPERT_FF), index), pl.BlockSpec((None, D_MODEL, EXPERT_FF), index),
                    pl.BlockSpec((None, EXPERT_FF, D_MODEL), index)]
    in_row_spec = pl.BlockSpec((step_rows * PACK_SUBLANES, LANES),
                               lambda i, be, nu: (jnp.minimum(i, (nu[0] - 1) // n_slots), 0))
    return pl.pallas_call(
        _expert_kernel,
        grid_spec=pltpu.PrefetchScalarGridSpec(
            num_scalar_prefetch=2,
            grid=(nb // n_slots,),
            in_specs=[in_row_spec] + w_specs,
            out_specs=pl.BlockSpec((step_rows * PACK_SUBLANES, LANES), lambda i, be, nu: (i, 0)),
            scratch_shapes=[
                pltpu.VMEM((D_MODEL, 2 * EXPERT_FF), BF16),
                pltpu.VMEM((EXPERT_FF, D_MODEL), BF16),
            ] * n_slots,
        ),
        out_shape=jax.ShapeDtypeStruct((nb * EXPERT_BLOCK * PACK_SUBLANES, LANES), jnp.uint32),
        compiler_params=pltpu.CompilerParams(dimension_semantics=("arbitrary",), vmem_limit_bytes=VMEM_LIMIT),
        name=f"experts_l{layer}",
    )(blk_expert, n_used, xs, *([w_gate, w_up, w_down] * n_slots))


def _combine_kernel(pos_hbm, y_hbm, x_ref, gate_ref, wsg_ref, wsu_ref, wsd_ref, ln2g_ref, ln2b_ref, o_ref,
                    idx_smem, ybuf, sem_idx, sem_y, *, n_tiles):
    i = pl.program_id(0)
    slot = lax.rem(i, 2)
    nslot = 1 - slot
    rows_per_slot = TOP_K * TOKEN_TILE

    def idx_copy(tile, sl):
        return pltpu.make_async_copy(pos_hbm.at[tile], idx_smem.at[sl], sem_idx.at[sl])

    def start_gather(sl):
        def fetch(k, j):
            pltpu.make_async_copy(_packed_token(y_hbm, idx_smem[sl, k, j]),
                                  _packed_token(ybuf, sl * rows_per_slot + k * TOKEN_TILE + j), sem_y.at[sl]
                                  ).start(priority=j % 2)
        _for_each_assignment(fetch)

    @pl.when(i == 0)
    def _():
        idx_copy(0, 0).start()
        idx_copy(0, 0).wait()
        start_gather(0)
        if n_tiles > 1:
            idx_copy(1, 1).start()

    @pl.when(i + 1 < n_tiles)
    def _():
        idx_copy(i + 1, nslot).wait()
        start_gather(nslot)

    @pl.when(i + 2 < n_tiles)
    def _():
        idx_copy(i + 2, slot).start()

    x = _load_token_tiles(x_ref, TOKEN_TILE)
    xb = x.astype(BF16)
    g = _dot(xb, wsg_ref[...])
    a = (g * _sigmoid(g) * _dot(xb, wsu_ref[...])).astype(BF16)
    shared = _dot(a, wsd_ref[...])

    base = slot * rows_per_slot
    slot_rows = ybuf.at[pl.ds(pl.multiple_of(base * PACK_SUBLANES, PACK_SUBLANES), rows_per_slot * PACK_SUBLANES), :]
    pltpu.make_async_copy(slot_rows, slot_rows, sem_y.at[slot]).wait()
    gate = gate_ref[...]
    routed = None
    for k in range(TOP_K):
        term = gate[:, k:k + 1] * _load_packed_tokens(ybuf, TOKEN_TILE, base + k * TOKEN_TILE)
        routed = term if routed is None else routed + term
    o_ref[...] = _layer_norm(ALPHA * x + (routed + shared), ln2g_ref[...], ln2b_ref[...])


def _combine(pos_tiles, y_sorted, x1, gate_t, w_sg, w_su, w_sd, ln2_g, ln2_b, *, layer):
    T = x1.shape[0] // TILE_SUBLANES
    tc = TOKEN_TILE
    n_tiles = T // tc
    per_layer = lambda *shape: pl.BlockSpec((None,) + shape, lambda i: (layer,) + (0,) * len(shape))
    return pl.pallas_call(
        functools.partial(_combine_kernel, n_tiles=n_tiles),
        grid=(n_tiles,),
        in_specs=[
            pl.BlockSpec(memory_space=pl.ANY),
            pl.BlockSpec(memory_space=pl.ANY),
            pl.BlockSpec((tc * TILE_SUBLANES, LANES), lambda i: (i, 0)),
            pl.BlockSpec((tc, TOP_K), lambda i: (i, 0)),
            per_layer(D_MODEL, EXPERT_FF),
            per_layer(D_MODEL, EXPERT_FF),
            per_layer(EXPERT_FF, D_MODEL),
            per_layer(1, D_MODEL),
            per_layer(1, D_MODEL),
        ],
        out_specs=pl.BlockSpec((tc, D_MODEL), lambda i: (i, 0)),
        out_shape=jax.ShapeDtypeStruct((T, D_MODEL), F32),
        scratch_shapes=[
            pltpu.SMEM((2, TOP_K, TOKEN_TILE), jnp.int32),
            pltpu.VMEM((2 * TOP_K * tc * PACK_SUBLANES, LANES), jnp.uint32),
            pltpu.SemaphoreType.DMA((2,)),
            pltpu.SemaphoreType.DMA((2,)),
        ],
        compiler_params=pltpu.CompilerParams(dimension_semantics=("arbitrary",), vmem_limit_bytes=VMEM_LIMIT),
        name=f"combine_l{layer}",
    )(pos_tiles, y_sorted, x1, gate_t, w_sg, w_su, w_sd, ln2_g, ln2_b)


def _dispatch_plan(ek, rk, counts_f):
    T = ek.shape[1]
    blk = EXPERT_BLOCK
    nb = (T * TOP_K) // blk + N_EXPERTS
    experts = jnp.arange(N_EXPERTS, dtype=jnp.int32)
    counts = counts_f[:, 0].astype(jnp.int32)
    pcounts = (counts + blk - 1) // blk * blk
    pends = jnp.sum(jnp.where(experts[None, :] <= experts[:, None], pcounts[None, :], 0), axis=1)
    pstarts = pends - pcounts
    pos = jnp.sum(jnp.where(ek[None] == experts[:, None, None], pstarts[:, None, None], 0), axis=0) + rk
    n_used = (pends[-1] // blk).reshape(1)
    block_row0 = jnp.arange(nb, dtype=jnp.int32) * blk
    blk_expert = jnp.minimum(jnp.sum((pends[None, :] <= block_row0[:, None]).astype(jnp.int32), axis=1), N_EXPERTS - 1)
    fill_start = pstarts + counts
    pos_tiles = pos.reshape(TOP_K, T // TOKEN_TILE, TOKEN_TILE).transpose(1, 0, 2)
    return blk_expert, n_used, fill_start, pos_tiles, nb * blk


def kernel(x, ln0_g, ln0_b, w_in, b_in, w_pool, pool_scale, attn_sinks, w_br_pool, w_br_attn, w_out, ln1_g, ln1_b,
           w_router, router_bias, w_exp_gate, w_exp_up, w_exp_down, w_sh_gate, w_sh_up, w_sh_down, ln2_g, ln2_b):
    B, S, D = x.shape
    depth = w_in.shape[0]
    row = lambda a: a.reshape(a.shape[0], 1, a.shape[1])
    w_in_b, w_pool_b = w_in.astype(BF16), w_pool.astype(BF16)
    w_brp_b, w_bra_b, w_out_b = w_br_pool.astype(BF16), w_br_attn.astype(BF16), w_out.astype(BF16)
    w_router_t = jnp.swapaxes(w_router, 1, 2).astype(BF16)
    bias_col = router_bias.reshape(depth, N_EXPERTS, 1)
    w_sg_b, w_su_b, w_sd_b = w_sh_gate.astype(BF16), w_sh_up.astype(BF16), w_sh_down.astype(BF16)
    w_eg = w_exp_gate.reshape(depth * N_EXPERTS, D, EXPERT_FF)
    w_eu = w_exp_up.reshape(depth * N_EXPERTS, D, EXPERT_FF)
    w_ed = w_exp_down.reshape(depth * N_EXPERTS, EXPERT_FF, D)
    ln0_g2, ln0_b2 = ln0_g.reshape(1, D), ln0_b.reshape(1, D)

    mixer_weights = (w_in_b, row(b_in), w_pool_b, row(pool_scale), w_brp_b, w_bra_b, w_out_b, row(ln1_g), row(ln1_b))
    shared_weights = (w_sg_b, w_su_b, w_sd_b, row(ln2_g), row(ln2_b))
    n_s = S // MIXER_TILE
    x1, x1p = _first_mixer(x, attn_sinks, ln0_g2, ln0_b2, mixer_weights)
    for l in range(depth):
        ek, rk, gk, counts_f = _router(x1, w_router_t, bias_col, layer=l)
        blk_expert, n_used, fill_start, pos_tiles, n_sorted = _dispatch_plan(ek, rk, counts_f)
        xs = _dispatch(fill_start, n_used, pos_tiles, x1p, n_sorted + EXPERT_BLOCK, layer=l)
        y_sorted = _experts(blk_expert, n_used, xs, w_eg, w_eu, w_ed, layer=l)
        if l + 1 < depth:
            x1, x1p = _combine_mixer(pos_tiles, y_sorted, x1, gk.T, shared_weights, attn_sinks, mixer_weights,
                                     layer=l + 1, n_s=n_s)
    return _combine(pos_tiles, y_sorted, x1, gk.T, *shared_weights, layer=depth - 1).reshape(B, S, D)
```
